```python
import jax
import jax.numpy as jnp
from jax import lax
import numpy as np

D_MODEL = 1024
BATCH = 4
SEQ = 4096
DEPTH = 4

HEAD_DIM = 64
D_RWKV = D_MODEL // 2
RWKV_HEADS = D_RWKV // HEAD_DIM
DECAY_LORA = 64
ICL_LORA = 64
GATE_LORA = 128
RWKV_GN_EPS = 64e-5
D_MOBA = D_MODEL // 2
MOBA_HEADS = D_MOBA // HEAD_DIM
MOBA_BLOCK = 256
MOBA_TOPK = 3
MOBA_Q_CHUNK = 32
D_SSM = D_MODEL // 2
SSM_GROUP = 16
SSM_GROUPS = D_SSM // SSM_GROUP
SSM_STATE = 64
N_BRANCHES = 3
N_EXPERT_GROUPS = 4
EXPERTS_PER_GROUP = 8
N_EXPERTS = N_EXPERT_GROUPS * EXPERTS_PER_GROUP
EXPERT_TOPK = 2
D_EXPERT = D_MODEL // 4
MOE_BLOCK = 512
LN_EPS = 1e-5
DEEPNORM_ALPHA = (2 * DEPTH) ** 0.25
DEEPNORM_BETA = (8 * DEPTH) ** -0.25
NEG_INF = -1e30
RWKV_COLS = 3 * D_RWKV + DECAY_LORA + ICL_LORA + GATE_LORA
MOBA_COLS = 3 * D_MOBA
OFF_MOBA = RWKV_COLS
OFF_SSM = OFF_MOBA + MOBA_COLS
OFF_GATE = OFF_SSM + D_SSM
D_IN_PROJ = OFF_GATE + N_BRANCHES * D_MODEL

kernel_name = 'hybrid_rwkv7_moba_s5_hmoe_deepnorm'


def layer_norm(x, w, b):
    xf = x.astype(jnp.float32)
    mu = jnp.mean(xf, -1, keepdims=True)
    var = jnp.mean(jnp.square(xf - mu), -1, keepdims=True)
    y = (xf - mu) * lax.rsqrt(var + LN_EPS) * w.astype(jnp.float32) + b.astype(jnp.float32)
    return y.astype(x.dtype)


def wkv7_scan(r, w, k, v, a, b):
    Bsz, L, H, N = r.shape

    def step(S, inp):
        r_t, w_t, k_t, v_t, a_t, b_t = inp
        sa = jnp.einsum('bhij,bhj->bhi', S, a_t)
        S = S * w_t[:, :, None, :] + sa[..., None] * b_t[:, :, None, :] + v_t[..., None] * k_t[:, :, None, :]
        return S, jnp.einsum('bhij,bhj->bhi', S, r_t)

    xs = tuple(jnp.moveaxis(t, 1, 0) for t in (r, w, k, v, a, b))
    S0 = jnp.zeros((Bsz, H, N, N), jnp.float32)
    _, y = lax.scan(step, S0, xs)
    return jnp.moveaxis(y, 0, 1)


def rwkv7_mixer(z, mu, w0, w2, a0, a2, g2, k_k, k_a, r_k, ln_w, ln_b):
    f32 = jnp.float32
    Bsz, L, _ = z.shape
    z = z.astype(f32)
    z_prev = jnp.pad(z, ((0, 0), (1, 0), (0, 0)))[:, :-1]
    z = z + (z_prev - z) * mu.astype(f32)
    cuts = np.cumsum([D_RWKV, D_RWKV, D_RWKV, DECAY_LORA, ICL_LORA]).tolist()
    r, k, v, xw, xa, xg = jnp.split(z, cuts, axis=-1)
    w_log = -jax.nn.softplus(-(w0.astype(f32) + jnp.tanh(xw) @ w2.astype(f32))) - 0.5
    decay = jnp.exp(-jnp.exp(w_log))
    a = jax.nn.sigmoid(a0.astype(f32) + xa @ a2.astype(f32))
    g = jax.nn.sigmoid(xg) @ g2.astype(f32)

    def heads(t):
        return t.reshape(Bsz, L, RWKV_HEADS, HEAD_DIM)

    kk = heads(k * k_k.astype(f32))
    kk = kk * lax.rsqrt(jnp.maximum(jnp.sum(kk * kk, -1, keepdims=True), 1e-24))
    k = k * (1.0 + (a - 1.0) * k_a.astype(f32))
    rh, kh, vh, ah = heads(r), heads(k), heads(v), heads(a)
    y = wkv7_scan(rh, heads(decay), kh, vh, -kk, kk * ah)
    mean = jnp.mean(y, -1, keepdims=True)
    var = jnp.mean(jnp.square(y - mean), -1, keepdims=True)
    y = ((y - mean) * lax.rsqrt(var + RWKV_GN_EPS)).reshape(Bsz, L, D_RWKV)
    y = y * ln_w.astype(f32) + ln_b.astype(f32)
    bonus = jnp.sum(rh * kh * r_k.astype(f32), -1, keepdims=True) * vh
    return (y + bonus.reshape(Bsz, L, D_RWKV)) * g


def alibi_slopes(n_heads):
    return 2.0 ** (-8.0 * jnp.arange(1, n_heads + 1, dtype=jnp.float32) / n_heads)


def moba_attention(q, k, v):
    f32 = jnp.float32
    Bsz, L, H, Dh = q.shape
    n_blk = -(-L // MOBA_BLOCK)
    Lp = n_blk * MOBA_BLOCK
    k_sel = min(MOBA_TOPK, n_blk)
    pad = ((0, 0), (0, Lp - L), (0, 0), (0, 0))
    q = jnp.pad(q, pad) * (Dh ** -0.5)
    kbt = jnp.pad(k, pad).reshape(Bsz, n_blk, MOBA_BLOCK, H, Dh).transpose(0, 3, 1, 2, 4)
    vbt = jnp.pad(v, pad).reshape(Bsz, n_blk, MOBA_BLOCK, H, Dh).transpose(0, 3, 1, 2, 4)
    kmean = jnp.mean(kbt.astype(f32), axis=3)
    n_chunk = Lp // MOBA_Q_CHUNK
    qc = q.reshape(Bsz, n_chunk, MOBA_Q_CHUNK, H, Dh).transpose(1, 0, 3, 2, 4)
    slopes = alibi_slopes(H)
    bi = jnp.arange(Bsz)[:, None, None, None]
    hi = jnp.arange(H)[None, :, None, None]
    blk_ids = jnp.arange(n_blk)
    offs = jnp.arange(MOBA_BLOCK)

    def one_chunk(args):
        c, qb = args
        q0 = c * MOBA_Q_CHUNK
        blk = q0 // MOBA_BLOCK
        tpos = q0 + jnp.arange(MOBA_Q_CHUNK)
        gate = jnp.einsum('bhqd,bhnd->bhqn', qb.astype(f32), kmean)
        gate = jnp.where(blk_ids < blk, gate, -jnp.inf)
        _, idx = lax.top_k(gate, k_sel)
        valid = idx < blk
        kg = kbt[bi, hi, idx]
        vg = vbt[bi, hi, idx]
        s_sel = jnp.einsum('bhqd,bhqknd->bhqkn', qb, kg, preferred_element_type=f32)
        spos = idx[..., None] * MOBA_BLOCK + offs
        s_sel = s_sel - slopes[:, None, None, None] * (tpos[:, None, None] - spos).astype(f32)
        s_sel = jnp.where(valid[..., None], s_sel, NEG_INF)
        k_own = lax.dynamic_index_in_dim(kbt, blk, axis=2, keepdims=False)
        v_own = lax.dynamic_index_in_dim(vbt, blk, axis=2, keepdims=False)
        s_own = jnp.einsum('bhqd,bhnd->bhqn', qb, k_own, preferred_element_type=f32)
        dist = tpos[:, None] - (blk * MOBA_BLOCK + offs)[None, :]
        s_own = jnp.where(dist >= 0, s_own - slopes[:, None, None] * dist.astype(f32), NEG_INF)
        s = jnp.concatenate([s_sel.reshape(Bsz, H, MOBA_Q_CHUNK, k_sel * MOBA_BLOCK), s_own], -1)
        p = jax.nn.softmax(s, axis=-1)
        p_sel = p[..., :k_sel * MOBA_BLOCK].reshape(Bsz, H, MOBA_Q_CHUNK, k_sel, MOBA_BLOCK).astype(vg.dtype)
        p_own = p[..., k_sel * MOBA_BLOCK:].astype(v_own.dtype)
        return (jnp.einsum('bhqkn,bhqknd->bhqd', p_sel, vg)
                + jnp.einsum('bhqn,bhnd->bhqd', p_own, v_own))

    out = lax.map(one_chunk, (jnp.arange(n_chunk), qc))
    return out.transpose(1, 0, 3, 2, 4).reshape(Bsz, Lp, H * Dh)[:, :L]


def _complex_affine_combine(e1, e2):
    ar1, ai1, br1, bi1 = e1
    ar2, ai2, br2, bi2 = e2
    return (ar2 * ar1 - ai2 * ai1, ar2 * ai1 + ai2 * ar1,
            ar2 * br1 - ai2 * bi1 + br2, ar2 * bi1 + ai2 * br1 + bi2)


def s5_mixer(u, a_re, a_im, b_re, b_im, c_re, c_im, d, log_dt, glu_w, glu_b):
    f32 = jnp.float32
    Bsz, L, _ = u.shape
    u = u.astype(f32)
    a_re = a_re.astype(f32)
    a_im = a_im.astype(f32)
    dt = jnp.exp(log_dt.astype(f32))[:, None]
    mag = jnp.exp(a_re * dt)
    lam_re = mag * jnp.cos(a_im * dt)
    lam_im = mag * jnp.sin(a_im * dt)
    den = a_re * a_re + a_im * a_im
    nr = lam_re - 1.0
    coef_re = (nr * a_re + lam_im * a_im) / den
    coef_im = (lam_im * a_re - nr * a_im) / den
    b_re = b_re.astype(f32)
    b_im = b_im.astype(f32)
    bb_re = coef_re[..., None] * b_re - coef_im[..., None] * b_im
    bb_im = coef_re[..., None] * b_im + coef_im[..., None] * b_re
    ug = u.reshape(Bsz, L, SSM_GROUPS, SSM_GROUP)
    bu_re = jnp.einsum('blgh,gph->blgp', ug, bb_re)
    bu_im = jnp.einsum('blgh,gph->blgp', ug, bb_im)
    lr = jnp.broadcast_to(lam_re, bu_re.shape)
    li = jnp.broadcast_to(lam_im, bu_re.shape)
    _, _, h_re, h_im = lax.associative_scan(_complex_affine_combine, (lr, li, bu_re, bu_im), axis=1)
    y = (jnp.einsum('blgp,ghp->blgh', h_re, c_re.astype(f32))
         - jnp.einsum('blgp,ghp->blgh', h_im, c_im.astype(f32)))
    y = y.reshape(Bsz, L, D_SSM) + d.astype(f32) * u
    y = jax.nn.gelu(y)
    zg = y @ glu_w.astype(f32) + glu_b.astype(f32)
    return zg[..., :D_SSM] * jax.nn.sigmoid(zg[..., D_SSM:])


def hybrid_mixer(x, w_in, rwkv_mu, rwkv_w0, rwkv_w2, rwkv_a0, rwkv_a2, rwkv_g2, rwkv_k_k, rwkv_k_a,
                 rwkv_r_k, rwkv_ln_w, rwkv_ln_b, ssm_a_re, ssm_a_im, ssm_b_re, ssm_b_im, ssm_c_re,
                 ssm_c_im, ssm_d, ssm_log_dt, ssm_glu_w, ssm_glu_b, w_up_rwkv, w_up_moba, w_up_ssm,
                 gate_b, w_out):
    Bsz, L, _ = x.shape
    proj = x @ w_in
    y_a = rwkv7_mixer(proj[..., :OFF_MOBA], rwkv_mu, rwkv_w0, rwkv_w2, rwkv_a0, rwkv_a2, rwkv_g2,
                      rwkv_k_k, rwkv_k_a, rwkv_r_k, rwkv_ln_w, rwkv_ln_b).astype(x.dtype)
    q, k, v = jnp.split(proj[..., OFF_MOBA:OFF_SSM], 3, axis=-1)

    def heads(t):
        return t.reshape(Bsz, L, MOBA_HEADS, HEAD_DIM)

    y_b = moba_attention(heads(q), heads(k), heads(v))
    y_c = s5_mixer(proj[..., OFF_SSM:OFF_GATE], ssm_a_re, ssm_a_im, ssm_b_re, ssm_b_im, ssm_c_re,
                   ssm_c_im, ssm_d, ssm_log_dt, ssm_glu_w, ssm_glu_b).astype(x.dtype)
    gates = jax.nn.sigmoid(proj[..., OFF_GATE:] + gate_b).reshape(Bsz, L, N_BRANCHES, D_MODEL)
    merged = (gates[..., 0, :] * (y_a @ w_up_rwkv)
              + gates[..., 1, :] * (y_b @ w_up_moba)
              + gates[..., 2, :] * (y_c @ w_up_ssm))
    return merged @ w_out


def hier_moe(x, router_group_w, router_group_b, router_expert_w, router_expert_b,
             expert_w_gate, expert_w_up, expert_w_down):
    f32 = jnp.float32
    Bsz, L, D = x.shape
    T = Bsz * L
    TK = T * EXPERT_TOPK
    xt = x.reshape(T, D)
    xf = xt.astype(f32)
    g_logits = xf @ router_group_w.astype(f32) + router_group_b.astype(f32)
    g_sel = jnp.argmax(g_logits, axis=-1)
    p_group = jnp.take_along_axis(jax.nn.softmax(g_logits, -1), g_sel[:, None], axis=-1)
    e_logits = (xf @ router_expert_w.astype(f32) + router_expert_b.astype(f32)).reshape(
        T, N_EXPERT_GROUPS, EXPERTS_PER_GROUP)
    e_logits = jnp.take_along_axis(e_logits, g_sel[:, None, None], axis=1)[:, 0]
    top_logit, top_local = lax.top_k(e_logits, EXPERT_TOPK)
    weight = p_group * jax.nn.softmax(top_logit, axis=-1)
    expert = g_sel[:, None] * EXPERTS_PER_GROUP + top_local
    flat_e = expert.reshape(TK)
    order = jnp.argsort(flat_e)
    e_sorted = flat_e[order]
    tok_sorted = order // EXPERT_TOPK
    counts = jnp.bincount(flat_e, length=N_EXPERTS)
    starts = jnp.cumsum(counts) - counts
    padded = (counts + MOE_BLOCK - 1) // MOE_BLOCK * MOE_BLOCK
    p_ends = jnp.cumsum(padded)
    p_starts = p_ends - padded
    dest = p_starts[e_sorted] + jnp.arange(TK) - starts[e_sorted]
    n_blocks = -(-TK // MOE_BLOCK) + N_EXPERTS
    buf = jnp.zeros((n_blocks * MOE_BLOCK, D), x.dtype).at[dest].set(xt[tok_sorted])
    blk_e = jnp.minimum(jnp.searchsorted(p_ends, jnp.arange(n_blocks) * MOE_BLOCK, side='right'),
                        N_EXPERTS - 1)
    xb = buf.reshape(n_blocks, MOE_BLOCK, D)
    hid = (jax.nn.silu(jnp.einsum('nbd,ndf->nbf', xb, expert_w_gate[blk_e]))
           * jnp.einsum('nbd,ndf->nbf', xb, expert_w_up[blk_e]))
    yb = jnp.einsum('nbf,nfd->nbd', hid, expert_w_down[blk_e]).reshape(n_blocks * MOE_BLOCK, D)
    w_sorted = weight.reshape(TK)[order].astype(x.dtype)
    y = jnp.zeros((T, D), x.dtype).at[tok_sorted].add(yb[dest] * w_sorted[:, None])
    return y.reshape(Bsz, L, D)


def setup_inputs(seed: int = 0) -> dict:
    key = jax.random.key(seed)
    keys = iter(jax.random.split(key, 48))
    f32 = jnp.float32

    def nrm(shape, scale):
        return scale * jax.random.normal(next(keys), shape, f32)

    def near(shape, center, spread):
        return center + spread * jax.random.normal(next(keys), shape, f32)

    Ld = DEPTH
    ratio = jnp.arange(D_RWKV, dtype=f32) / (D_RWKV - 1)
    w0_base = -7.0 + 5.0 * ratio ** 0.85 + 0.5
    a_im_base = jnp.pi * jnp.arange(SSM_STATE, dtype=f32)
    return {
        'x': nrm((BATCH, SEQ, D_MODEL), 1.0),
        'w_in': nrm((Ld, D_MODEL, D_IN_PROJ), D_MODEL ** -0.5),
        'rwkv_mu': jax.random.uniform(next(keys), (Ld, RWKV_COLS), f32),
        'rwkv_w0': w0_base + nrm((Ld, D_RWKV), 0.1),
        'rwkv_w2': nrm((Ld, DECAY_LORA, D_RWKV), 0.5 * DECAY_LORA ** -0.5),
        'rwkv_a0': nrm((Ld, D_RWKV), 0.1),
        'rwkv_a2': nrm((Ld, ICL_LORA, D_RWKV), ICL_LORA ** -0.5),
        'rwkv_g2': nrm((Ld, GATE_LORA, D_RWKV), GATE_LORA ** -0.5),
        'rwkv_k_k': near((Ld, D_RWKV), 0.85, 0.05),
        'rwkv_k_a': near((Ld, D_RWKV), 1.0, 0.05),
        'rwkv_r_k': nrm((Ld, RWKV_HEADS, HEAD_DIM), 0.1),
        'rwkv_ln_w': near((Ld, D_RWKV), 1.0, 0.05),
        'rwkv_ln_b': nrm((Ld, D_RWKV), 0.02),
        'ssm_a_re': near((Ld, SSM_GROUPS, SSM_STATE), -0.5, 0.01),
        'ssm_a_im': jnp.broadcast_to(a_im_base, (Ld, SSM_GROUPS, SSM_STATE)) + nrm((Ld, SSM_GROUPS, SSM_STATE), 0.01),
        'ssm_b_re': nrm((Ld, SSM_GROUPS, SSM_STATE, SSM_GROUP), (2 * SSM_GROUP) ** -0.5),
        'ssm_b_im': nrm((Ld, SSM_GROUPS, SSM_STATE, SSM_GROUP), (2 * SSM_GROUP) ** -0.5),
        'ssm_c_re': nrm((Ld, SSM_GROUPS, SSM_GROUP, SSM_STATE), 0.5),
        'ssm_c_im': nrm((Ld, SSM_GROUPS, SSM_GROUP, SSM_STATE), 0.5),
        'ssm_d': nrm((Ld, D_SSM), 1.0),
        'ssm_log_dt': jax.random.uniform(next(keys), (Ld, SSM_GROUPS), f32,
                                         minval=float(np.log(1e-3)), maxval=float(np.log(1e-1))),
        'ssm_glu_w': nrm((Ld, D_SSM, 2 * D_SSM), D_SSM ** -0.5),
        'ssm_glu_b': nrm((Ld, 2 * D_SSM), 0.02),
        'w_up_rwkv': nrm((Ld, D_RWKV, D_MODEL), D_RWKV ** -0.5),
        'w_up_moba': nrm((Ld, D_MOBA, D_MODEL), D_MOBA ** -0.5),
        'w_up_ssm': nrm((Ld, D_SSM, D_MODEL), D_SSM ** -0.5),
        'gate_b': nrm((Ld, N_BRANCHES * D_MODEL), 0.1),
        'w_out': nrm((Ld, D_MODEL, D_MODEL), DEEPNORM_BETA * D_MODEL ** -0.5),
        'ln1_w': near((Ld, D_MODEL), 1.0, 0.05),
        'ln1_b': nrm((Ld, D_MODEL), 0.02),
        'router_group_w': nrm((Ld, D_MODEL, N_EXPERT_GROUPS), D_MODEL ** -0.5),
        'router_group_b': nrm((Ld, N_EXPERT_GROUPS), 0.01),
        'router_expert_w': nrm((Ld, D_MODEL, N_EXPERTS), D_MODEL ** -0.5),
        'router_expert_b': nrm((Ld, N_EXPERTS), 0.01),
        'expert_w_gate': nrm((Ld, N_EXPERTS, D_MODEL, D_EXPERT), D_MODEL ** -0.5),
        'expert_w_up': nrm((Ld, N_EXPERTS, D_MODEL, D_EXPERT), D_MODEL ** -0.5),
        'expert_w_down': nrm((Ld, N_EXPERTS, D_EXPERT, D_MODEL), DEEPNORM_BETA * D_EXPERT ** -0.5),
        'ln2_w': near((Ld, D_MODEL), 1.0, 0.05),
        'ln2_b': nrm((Ld, D_MODEL), 0.02),
    }


def reference(x, w_in, rwkv_mu, rwkv_w0, rwkv_w2, rwkv_a0, rwkv_a2, rwkv_g2, rwkv_k_k, rwkv_k_a,
              rwkv_r_k, rwkv_ln_w, rwkv_ln_b, ssm_a_re, ssm_a_im, ssm_b_re, ssm_b_im, ssm_c_re,
              ssm_c_im, ssm_d, ssm_log_dt, ssm_glu_w, ssm_glu_b, w_up_rwkv, w_up_moba, w_up_ssm,
              gate_b, w_out, ln1_w, ln1_b, router_group_w, router_group_b, router_expert_w,
              router_expert_b, expert_w_gate, expert_w_up, expert_w_down, ln2_w, ln2_b):
    for l in range(DEPTH):
        h = hybrid_mixer(x, w_in[l], rwkv_mu[l], rwkv_w0[l], rwkv_w2[l], rwkv_a0[l], rwkv_a2[l],
                         rwkv_g2[l], rwkv_k_k[l], rwkv_k_a[l], rwkv_r_k[l], rwkv_ln_w[l], rwkv_ln_b[l],
                         ssm_a_re[l], ssm_a_im[l], ssm_b_re[l], ssm_b_im[l], ssm_c_re[l], ssm_c_im[l],
                         ssm_d[l], ssm_log_dt[l], ssm_glu_w[l], ssm_glu_b[l], w_up_rwkv[l],
                         w_up_moba[l], w_up_ssm[l], gate_b[l], w_out[l])
        x = layer_norm(DEEPNORM_ALPHA * x + h, ln1_w[l], ln1_b[l])
        h = hier_moe(x, router_group_w[l], router_group_b[l], router_expert_w[l], router_expert_b[l],
                     expert_w_gate[l], expert_w_up[l], expert_w_down[l])
        x = layer_norm(DEEPNORM_ALPHA * x + h, ln2_w[l], ln2_b[l])
    return x
```

```python
import functools

import jax
import jax.numpy as jnp
import numpy as np
from jax import lax
from jax.experimental import pallas as pl
from jax.experimental.pallas import tpu as pltpu

f32 = jnp.float32
bf16 = jnp.bfloat16

D_MODEL = 1024
DEPTH = 4
HEAD_DIM = 64
D_RWKV = 512
RWKV_HEADS = D_RWKV // HEAD_DIM
DECAY_LORA = 64
ICL_LORA = 64
GATE_LORA = 128
RWKV_GN_EPS = 64e-5
D_MOBA = 512
MOBA_HEADS = D_MOBA // HEAD_DIM
MOBA_BLOCK = 256
MOBA_TOPK = 3
D_SSM = 512
SSM_GROUP = 16
SSM_GROUPS = D_SSM // SSM_GROUP
SSM_STATE = 64
N_BRANCHES = 3
N_EXPERT_GROUPS = 4
EXPERTS_PER_GROUP = 8
N_EXPERTS = N_EXPERT_GROUPS * EXPERTS_PER_GROUP
D_EXPERT = D_MODEL // 4
LN_EPS = 1e-5
DEEPNORM_ALPHA = (2 * DEPTH) ** 0.25
NEG_INF = -1e30
RWKV_COLS = 3 * D_RWKV + DECAY_LORA + ICL_LORA + GATE_LORA
MOBA_COLS = 3 * D_MOBA
OFF_MOBA = RWKV_COLS
OFF_SSM = OFF_MOBA + MOBA_COLS
OFF_GATE = OFF_SSM + D_SSM

LANES = 128
WKV_CHUNK = 64
VMEM_LIMIT = 48 * 1024 * 1024


def _nt(a, b):
    return lax.dot_general(a, b, (((1,), (1,)), ((), ())), preferred_element_type=f32)


def _tn(a, b):
    return lax.dot_general(a, b, (((0,), (0,)), ((), ())), preferred_element_type=f32)


def _mm(a, b):
    return jnp.dot(a, b, preferred_element_type=f32)


def _split2(x):
    hi = x.astype(bf16)
    lo = (x - hi.astype(f32)).astype(bf16)
    return hi, lo


def _layer_norm(y, w, b):
    mu = jnp.mean(y, axis=-1, keepdims=True)
    d = y - mu
    var = jnp.mean(d * d, axis=-1, keepdims=True)
    return d * lax.rsqrt(var + LN_EPS) * w + b


def _proj_kernel(x_ref, w_ref, o_ref):
    o_ref[...] = _mm(x_ref[...].astype(bf16), w_ref[...])


def _proj_gate_kernel(x_ref, w_ref, b_ref, o_ref):
    o_ref[...] = jax.nn.sigmoid(_mm(x_ref[...].astype(bf16), w_ref[...]) + b_ref[...])


def _project(x2, w, bias=None, tm=512, tn=None):
    m, k = x2.shape
    n = w.shape[1]
    tn = n if tn is None else tn
    in_specs = [pl.BlockSpec((tm, k), lambda i, j: (i, 0)),
                pl.BlockSpec((k, tn), lambda i, j: (0, j))]
    args = [x2, w]
    body = _proj_kernel
    if bias is not None:
        in_specs.append(pl.BlockSpec((1, tn), lambda i, j: (0, j)))
        args.append(bias)
        body = _proj_gate_kernel
    return pl.pallas_call(
        body,
        grid=(m // tm, n // tn),
        in_specs=in_specs,
        out_specs=pl.BlockSpec((tm, tn), lambda i, j: (i, j)),
        out_shape=jax.ShapeDtypeStruct((m, n), f32),
        compiler_params=pltpu.CompilerParams(
            dimension_semantics=("parallel", "parallel"), vmem_limit_bytes=VMEM_LIMIT),
    )(*args)


def _rwkv_kernel(z_ref, mu_ref, w0_ref, w2_ref, a0_ref, a2_ref, g2_ref, kk_ref, ka_ref, rk_ref,
                 lnw_ref, lnb_ref, seg_ref, tri_ref, o_ref,
                 zlast_s, state_s, r_s, k_s, v_s, a_s, b_s, lw_s, g_s, bon_s, y_s):
    tb = z_ref.shape[1]
    n_chunks = tb // WKV_CHUNK
    c64 = WKV_CHUNK

    @pl.when(pl.program_id(1) == 0)
    def _():
        zlast_s[...] = jnp.zeros_like(zlast_s)
        state_s[...] = jnp.zeros_like(state_s)

    seg = seg_ref[...]

    def segsum(x):
        hi, lo = _split2(x)
        return _mm(hi, seg) + _mm(lo, seg)

    z = z_ref[0]
    row = lax.broadcasted_iota(jnp.int32, z.shape, 0)
    z_prev = jnp.where(row == 0, zlast_s[...], pltpu.roll(z, 1, axis=0))
    zlast_s[...] = z[tb - 1:tb, :]
    zz = z + (z_prev - z) * mu_ref[...]
    r = zz[:, 0:D_RWKV]
    k = zz[:, D_RWKV:2 * D_RWKV]
    v = zz[:, 2 * D_RWKV:3 * D_RWKV]
    o1 = 3 * D_RWKV
    xw = zz[:, o1:o1 + DECAY_LORA]
    xa = zz[:, o1 + DECAY_LORA:o1 + DECAY_LORA + ICL_LORA]
    xg = zz[:, o1 + DECAY_LORA + ICL_LORA:]
    wl = w0_ref[...] + _mm(jnp.tanh(xw).astype(bf16), w2_ref[...])
    softplus = jnp.maximum(-wl, 0.0) + jnp.log(1.0 + jnp.exp(-jnp.abs(wl)))
    lw_s[...] = -jnp.exp(-softplus - 0.5)
    a_icl = jax.nn.sigmoid(a0_ref[...] + _mm(xa.astype(bf16), a2_ref[...]))
    g_s[...] = _mm(jax.nn.sigmoid(xg).astype(bf16), g2_ref[...])
    kk = k * kk_ref[...]
    kk = kk * lax.rsqrt(jnp.maximum(segsum(kk * kk), 1e-24))
    k2 = k * (1.0 + (a_icl - 1.0) * ka_ref[...])
    r_s[...] = r
    k_s[...] = k2
    v_s[...] = v
    a_s[...] = -kk
    b_s[...] = kk * a_icl
    bon_s[...] = segsum(r * k2 * rk_ref[...])

    ri = lax.broadcasted_iota(jnp.int32, (c64, c64), 0)
    ci = lax.broadcasted_iota(jnp.int32, (c64, c64), 1)
    strict = ri > ci
    incl = ri >= ci
    eye = jnp.where(ri == ci, 1.0, 0.0).astype(f32)
    tri = tri_ref[...]

    def chunk_body(c, carry):
        rows = pl.ds(pl.multiple_of(c * c64, c64), c64)
        lwc = lw_s[rows, :]
        lhi, llo = _split2(lwc)
        cl = _mm(tri, lhi) + _mm(tri, llo)
        cl_last = cl[c64 - 1:c64, :]
        e_cl = jnp.exp(cl)
        e_prev = jnp.exp(cl - lwc)
        e_neg = jnp.exp(-cl)
        e_tot = jnp.exp(cl_last - cl)
        p_c = jnp.exp(cl_last)
        rc = r_s[rows, :]
        kc = k_s[rows, :]
        vc = v_s[rows, :]
        ac = a_s[rows, :]
        bc = b_s[rows, :]
        a_t = ac * e_prev
        r_t = rc * e_cl
        b_t = bc * e_neg
        k_t = kc * e_neg
        b_h = bc * e_tot
        k_h = kc * e_tot
        for h in range(RWKV_HEADS):
            sl = slice(h * HEAD_DIM, (h + 1) * HEAD_DIM)
            ar = jnp.concatenate([a_t[:, sl], r_t[:, sl]], axis=0).astype(bf16)
            bk = jnp.concatenate([b_t[:, sl], k_t[:, sl]], axis=0).astype(bf16)
            gm = _nt(ar, bk)
            a_ab = jnp.where(strict, gm[:c64, :c64], 0.0)
            a_ak = jnp.where(strict, gm[:c64, c64:], 0.0)
            a_rb = jnp.where(incl, gm[c64:, :c64], 0.0)
            a_rk = jnp.where(incl, gm[c64:, c64:], 0.0)
            x = a_ab
            t_inv = eye + x
            for _ in range(5):
                xb = x.astype(bf16)
                x = _mm(xb, xb)
                t_inv = t_inv + _mm(t_inv.astype(bf16), x.astype(bf16))
            vh = vc[:, sl]
            vb = vh.astype(bf16)
            akv = _mm(a_ak.astype(bf16), vb)
            rhs = jnp.concatenate([a_t[:, sl], akv], axis=1).astype(bf16)
            tw = _mm(t_inv.astype(bf16), rhs)
            s_old = state_s[h]
            wr = jnp.concatenate([tw[:, :c64], r_t[:, sl]], axis=0).astype(bf16)
            wrs = _nt(wr, s_old.astype(bf16))
            u = wrs[:c64] + tw[:, c64:]
            y = wrs[c64:] + _mm(a_rb.astype(bf16), u.astype(bf16)) + _mm(a_rk.astype(bf16), vb)
            uv = jnp.concatenate([u, vh], axis=0).astype(bf16)
            bkh = jnp.concatenate([b_h[:, sl], k_h[:, sl]], axis=0).astype(bf16)
            state_s[h] = s_old * p_c[:, sl] + _tn(uv, bkh)
            y_s[rows, sl] = y
        return carry

    lax.fori_loop(0, n_chunks, chunk_body, 0)

    y = y_s[...]
    mean = segsum(y) * (1.0 / HEAD_DIM)
    d = y - mean
    var = segsum(d * d) * (1.0 / HEAD_DIM)
    yn = d * lax.rsqrt(var + RWKV_GN_EPS) * lnw_ref[...] + lnb_ref[...]
    o_ref[0] = (yn + bon_s[...] * v_s[...]) * g_s[...]


def _rwkv(z3, p, tb=512):
    bsz, seq, _ = z3.shape
    tb = min(tb, seq)
    seg = (np.arange(D_RWKV)[:, None] // HEAD_DIM == np.arange(D_RWKV)[None, :] // HEAD_DIM)
    seg = jnp.asarray(seg, bf16)
    tri = jnp.asarray(np.tril(np.ones((WKV_CHUNK, WKV_CHUNK))), bf16)
    row = lambda a: a.reshape(1, -1).astype(f32)
    args = [z3, row(p['mu']), row(p['w0']), p['w2'].astype(bf16), row(p['a0']), p['a2'].astype(bf16),
            p['g2'].astype(bf16), row(p['k_k']), row(p['k_a']), row(p['r_k']), row(p['ln_w']),
            row(p['ln_b']), seg, tri]
    full = lambda a: pl.BlockSpec(a.shape, lambda b, l: (0,) * a.ndim)
    in_specs = [pl.BlockSpec((1, tb, RWKV_COLS), lambda b, l: (b, l, 0))] + [full(a) for a in args[1:]]
    big = lambda: pltpu.VMEM((tb, D_RWKV), f32)
    return pl.pallas_call(
        _rwkv_kernel,
        grid=(bsz, seq // tb),
        in_specs=in_specs,
        out_specs=pl.BlockSpec((1, tb, D_RWKV), lambda b, l: (b, l, 0)),
        out_shape=jax.ShapeDtypeStruct((bsz, seq, D_RWKV), f32),
        scratch_shapes=[pltpu.VMEM((1, RWKV_COLS), f32),
                        pltpu.VMEM((RWKV_HEADS, HEAD_DIM, HEAD_DIM), f32),
                        big(), big(), big(), big(), big(), big(), big(), big(), big()],
        compiler_params=pltpu.CompilerParams(
            dimension_semantics=("parallel", "arbitrary"), vmem_limit_bytes=VMEM_LIMIT),
    )(*args)


def _moba_kernel(q_ref, k_ref, v_ref, slope_ref, o_ref, kmean_s, m_s, l_s, acc_s):
    blk = MOBA_BLOCK
    seq = k_ref.shape[1]
    n_blk = seq // blk
    qi = pl.program_id(2)

    @pl.when(qi == 0)
    def _():
        kmean_s[...] = jnp.zeros_like(kmean_s)
        for n in range(n_blk):
            kmean_s[n:n + 1, :] = jnp.mean(k_ref[0, n * blk:(n + 1) * blk, :], axis=0, keepdims=True)

    ri = lax.broadcasted_iota(jnp.int32, (blk, blk), 0)
    ci = lax.broadcasted_iota(jnp.int32, (blk, blk), 1)
    rel = (ri - ci).astype(f32)
    lane = lax.broadcasted_iota(jnp.int32, (blk, LANES), 1)
    past = lane < qi

    for hh in range(2):
        sl = slice(hh * HEAD_DIM, (hh + 1) * HEAD_DIM)
        slope = slope_ref[0, hh:hh + 1, :]
        q = q_ref[0][:, sl] * (HEAD_DIM ** -0.5)
        qh, ql = _split2(q)
        kmh, kml = _split2(kmean_s[:, sl])
        gate = _nt(qh, kmh) + _nt(qh, kml) + _nt(ql, kmh)
        gate = jnp.where(past, gate, -jnp.inf)
        cnt = jnp.zeros((blk, LANES), f32)
        for m in range(n_blk - 1):
            gm = gate[:, m:m + 1]
            beats = (gm > gate) | ((gm == gate) & (lane > m))
            cnt = cnt + jnp.where(beats, 1.0, 0.0)
        sel = jnp.where((cnt < float(MOBA_TOPK)) & past, 1.0, 0.0)

        qb = q.astype(bf16)
        start = pl.multiple_of(qi * blk, blk)
        k_own = k_ref[0, pl.ds(start, blk), sl].astype(bf16)
        v_own = v_ref[0, pl.ds(start, blk), sl].astype(bf16)
        s = _nt(qb, k_own)
        s = jnp.where(rel >= 0.0, s - slope * rel, NEG_INF)
        m0 = jnp.max(s, axis=-1, keepdims=True)
        p = jnp.exp(s - m0)
        m_s[...] = m0
        l_s[...] = jnp.sum(p, axis=-1, keepdims=True)
        acc_s[...] = _mm(p.astype(bf16), v_own)

        for j in range(n_blk - 1):
            @pl.when(j < qi)
            def _(j=j):
                kj = k_ref[0, j * blk:(j + 1) * blk, sl].astype(bf16)
                vj = v_ref[0, j * blk:(j + 1) * blk, sl].astype(bf16)
                off = ((qi - j) * blk).astype(f32)
                sj = _nt(qb, kj) - slope * (rel + off)
                sj = jnp.where(sel[:, j:j + 1] > 0.0, sj, NEG_INF)
                m_old = m_s[...]
                m_new = jnp.maximum(m_old, jnp.max(sj, axis=-1, keepdims=True))
                alpha = jnp.exp(m_old - m_new)
                pj = jnp.exp(sj - m_new)
                l_s[...] = l_s[...] * alpha + jnp.sum(pj, axis=-1, keepdims=True)
                acc_s[...] = acc_s[...] * alpha + _mm(pj.astype(bf16), vj)
                m_s[...] = m_new

        o_ref[0, :, sl] = acc_s[...] / l_s[...]


def _moba(qkv3):
    bsz, seq, _ = qkv3.shape
    blk = MOBA_BLOCK
    n_pairs = MOBA_HEADS // 2
    slopes = 2.0 ** (-8.0 * np.arange(1, MOBA_HEADS + 1, dtype=np.float64) / MOBA_HEADS)
    slopes = jnp.asarray(np.broadcast_to(slopes.reshape(n_pairs, 2, 1), (n_pairs, 2, blk)), f32)
    return pl.pallas_call(
        _moba_kernel,
        grid=(bsz, n_pairs, seq // blk),
        in_specs=[pl.BlockSpec((1, blk, LANES), lambda b, p, i: (b, i, p)),
                  pl.BlockSpec((1, seq, LANES), lambda b, p, i: (b, 0, n_pairs + p)),
                  pl.BlockSpec((1, seq, LANES), lambda b, p, i: (b, 0, 2 * n_pairs + p)),
                  pl.BlockSpec((1, 2, blk), lambda b, p, i: (p, 0, 0))],
        out_specs=pl.BlockSpec((1, blk, LANES), lambda b, p, i: (b, i, p)),
        out_shape=jax.ShapeDtypeStruct((bsz, seq, D_MOBA), f32),
        scratch_shapes=[pltpu.VMEM((LANES, LANES), f32),
                        pltpu.VMEM((blk, 1), f32), pltpu.VMEM((blk, 1), f32),
                        pltpu.VMEM((blk, HEAD_DIM), f32)],
        compiler_params=pltpu.CompilerParams(
            dimension_semantics=("parallel", "parallel", "arbitrary"), vmem_limit_bytes=VMEM_LIMIT),
    )(qkv3, qkv3, qkv3, slopes)


def _s5_kernel(u_ref, bd_ref, cd_ref, lam_ref, d_ref, gw_ref, gb_ref, o_ref, h_s, carry_s):
    tb = u_ref.shape[1]
    n_state = lam_ref.shape[1]

    @pl.when(pl.program_id(1) == 0)
    def _():
        carry_s[...] = jnp.zeros_like(carry_s)

    u = u_ref[0]
    h_s[...] = _mm(u.astype(bf16), bd_ref[...])
    lr = lam_ref[0:1, :]
    li = lam_ref[1:2, :]

    def step(t, carry):
        hr, hi = carry
        br = h_s[pl.ds(t, 1), 0:n_state]
        bi = h_s[pl.ds(t, 1), n_state:2 * n_state]
        nr = lr * hr - li * hi + br
        ni = lr * hi + li * hr + bi
        h_s[pl.ds(t, 1), 0:n_state] = nr
        h_s[pl.ds(t, 1), n_state:2 * n_state] = ni
        return nr, ni

    hr, hi = lax.fori_loop(0, tb, step, (carry_s[0:1, :], carry_s[1:2, :]), unroll=8)
    carry_s[0:1, :] = hr
    carry_s[1:2, :] = hi
    y = _mm(h_s[...].astype(bf16), cd_ref[...]) + d_ref[...] * u
    y = 0.5 * y * (1.0 + jnp.tanh(0.7978845608028654 * (y + 0.044715 * (y * y * y))))
    zg = _mm(y.astype(bf16), gw_ref[...]) + gb_ref[...]
    o_ref[0] = zg[:, :D_SSM] * jax.nn.sigmoid(zg[:, D_SSM:])


def _s5_params(p):
    a_re = p['a_re'].astype(f32)
    a_im = p['a_im'].astype(f32)
    dt = jnp.exp(p['log_dt'].astype(f32))[:, None]
    mag = jnp.exp(a_re * dt)
    lam_re = mag * jnp.cos(a_im * dt)
    lam_im = mag * jnp.sin(a_im * dt)
    den = a_re * a_re + a_im * a_im
    nr = lam_re - 1.0
    coef_re = (nr * a_re + lam_im * a_im) / den
    coef_im = (lam_im * a_re - nr * a_im) / den
    b_re = p['b_re'].astype(f32)
    b_im = p['b_im'].astype(f32)
    bb_re = coef_re[..., None] * b_re - coef_im[..., None] * b_im
    bb_im = coef_re[..., None] * b_im + coef_im[..., None] * b_re
    eye = jnp.eye(SSM_GROUPS, dtype=f32)
    n_state = SSM_GROUPS * SSM_STATE
    bd = jnp.concatenate([jnp.einsum('gph,gk->ghkp', bb_re, eye).reshape(D_SSM, n_state),
                          jnp.einsum('gph,gk->ghkp', bb_im, eye).reshape(D_SSM, n_state)], axis=1)
    cd = jnp.concatenate([jnp.einsum('ghp,gk->gpkh', p['c_re'].astype(f32), eye).reshape(n_state, D_SSM),
                          -jnp.einsum('ghp,gk->gpkh', p['c_im'].astype(f32), eye).reshape(n_state, D_SSM)],
                         axis=0)
    lam = jnp.stack([lam_re.reshape(n_state), lam_im.reshape(n_state)], axis=0)
    return bd.astype(bf16), cd.astype(bf16), lam


def _s5(u3, p, tb=256):
    bsz, seq, _ = u3.shape
    tb = min(tb, seq)
    n_state = SSM_GROUPS * SSM_STATE
    bd, cd, lam = _s5_params(p)
    args = [u3, bd, cd, lam, p['d'].reshape(1, -1).astype(f32), p['glu_w'].astype(bf16),
            p['glu_b'].reshape(1, -1).astype(f32)]
    full = lambda a: pl.BlockSpec(a.shape, lambda b, l: (0,) * a.ndim)
    return pl.pallas_call(
        _s5_kernel,
        grid=(bsz, seq // tb),
        in_specs=[pl.BlockSpec((1, tb, D_SSM), lambda b, l: (b, l, 0))] + [full(a) for a in args[1:]],
        out_specs=pl.BlockSpec((1, tb, D_SSM), lambda b, l: (b, l, 0)),
        out_shape=jax.ShapeDtypeStruct((bsz, seq, D_SSM), f32),
        scratch_shapes=[pltpu.VMEM((tb, 2 * n_state), f32), pltpu.VMEM((2, n_state), f32)],
        compiler_params=pltpu.CompilerParams(
            dimension_semantics=("parallel", "arbitrary"), vmem_limit_bytes=VMEM_LIMIT),
    )(*args)


def _merge_kernel(x_ref, ya_ref, yb_ref, yc_ref, g_ref, wa_ref, wb_ref, wc_ref, wo_ref,
                  lnw_ref, lnb_ref, o_ref):
    g = g_ref[...]
    merged = (g[:, 0:D_MODEL] * _mm(ya_ref[...].astype(bf16), wa_ref[...])
              + g[:, D_MODEL:2 * D_MODEL] * _mm(yb_ref[...].astype(bf16), wb_ref[...])
              + g[:, 2 * D_MODEL:] * _mm(yc_ref[...].astype(bf16), wc_ref[...]))
    h = _mm(merged.astype(bf16), wo_ref[...])
    o_ref[...] = _layer_norm(DEEPNORM_ALPHA * x_ref[...] + h, lnw_ref[...], lnb_ref[...])


def _merge(x2, ya, yb, yc, gates, wa, wb, wc, wo, lnw, lnb, tm=512):
    m = x2.shape[0]
    rowblk = lambda n: pl.BlockSpec((tm, n), lambda i: (i, 0))
    full = lambda a: pl.BlockSpec(a.shape, lambda i: (0,) * a.ndim)
    consts = [wa, wb, wc, wo, lnw, lnb]
    return pl.pallas_call(
        _merge_kernel,
        grid=(m // tm,),
        in_specs=[rowblk(D_MODEL), rowblk(D_RWKV), rowblk(D_MOBA), rowblk(D_SSM),
                  rowblk(N_BRANCHES * D_MODEL)] + [full(a) for a in consts],
        out_specs=rowblk(D_MODEL),
        out_shape=jax.ShapeDtypeStruct((m, D_MODEL), f32),
        compiler_params=pltpu.CompilerParams(
            dimension_semantics=("parallel",), vmem_limit_bytes=VMEM_LIMIT),
    )(x2, ya, yb, yc, gates, *consts)


def _moe_kernel(x_ref, rwh_ref, rwl_ref, rb_ref, wg_ref, wu_ref, wd_ref, lnw_ref, lnb_ref, o_ref,
                xb_s, wt_s, acc_s):
    e = pl.program_id(1)
    tm = x_ref.shape[0]
    lane = lax.broadcasted_iota(jnp.int32, (tm, LANES), 1)
    lane_f = lane.astype(f32)

    @pl.when(e == 0)
    def _():
        x = x_ref[...]
        xh, xl = _split2(x)
        xb_s[...] = xh
        rwh = rwh_ref[...]
        logits = _mm(xh, rwh) + _mm(xh, rwl_ref[...]) + _mm(xl, rwh) + rb_ref[...]
        is_grp = (lane >= N_EXPERTS) & (lane < N_EXPERTS + N_EXPERT_GROUPS)
        gl = jnp.where(is_grp, logits, -jnp.inf)
        gmax = jnp.max(gl, axis=-1, keepdims=True)
        gidx = jnp.min(jnp.where(gl == gmax, lane_f, 1e9), axis=-1, keepdims=True) - float(N_EXPERTS)
        p_group = 1.0 / jnp.sum(jnp.where(is_grp, jnp.exp(gl - gmax), 0.0), axis=-1, keepdims=True)
        grp_of_lane = jnp.floor(lane_f * (1.0 / EXPERTS_PER_GROUP))
        in_grp = (lane < N_EXPERTS) & (grp_of_lane == gidx)
        el = jnp.where(in_grp, logits, -jnp.inf)
        m1 = jnp.max(el, axis=-1, keepdims=True)
        i1 = jnp.min(jnp.where(el == m1, lane_f, 1e9), axis=-1, keepdims=True)
        el2 = jnp.where(lane_f == i1, -jnp.inf, el)
        m2 = jnp.max(el2, axis=-1, keepdims=True)
        i2 = jnp.min(jnp.where(el2 == m2, lane_f, 1e9), axis=-1, keepdims=True)
        e2 = jnp.exp(m2 - m1)
        w1 = p_group / (1.0 + e2)
        w2 = p_group * e2 / (1.0 + e2)
        wt_s[...] = jnp.where(lane_f == i1, w1, 0.0) + jnp.where(lane_f == i2, w2, 0.0)
        acc_s[...] = jnp.zeros_like(acc_s)

    xb = xb_s[...]
    gact = _mm(xb, wg_ref[0])
    up = _mm(xb, wu_ref[0])
    wcol = jnp.sum(jnp.where(lane == e, wt_s[...], 0.0), axis=-1, keepdims=True)
    hid = gact * jax.nn.sigmoid(gact) * up * wcol
    acc_s[...] += _mm(hid.astype(bf16), wd_ref[0])

    @pl.when(e == N_EXPERTS - 1)
    def _():
        o_ref[...] = _layer_norm(DEEPNORM_ALPHA * x_ref[...] + acc_s[...], lnw_ref[...], lnb_ref[...])


def _moe(x2, rw, rb, wg, wu, wd, lnw, lnb, tm=1024):
    m = x2.shape[0]
    tm = min(tm, m)
    rwh, rwl = _split2(rw)
    full = lambda a: pl.BlockSpec(a.shape, lambda i, e: (0,) * a.ndim)
    return pl.pallas_call(
        _moe_kernel,
        grid=(m // tm, N_EXPERTS),
        in_specs=[pl.BlockSpec((tm, D_MODEL), lambda i, e: (i, 0)),
                  full(rwh), full(rwl), full(rb),
                  pl.BlockSpec((1, D_MODEL, D_EXPERT), lambda i, e: (e, 0, 0)),
                  pl.BlockSpec((1, D_MODEL, D_EXPERT), lambda i, e: (e, 0, 0)),
                  pl.BlockSpec((1, D_EXPERT, D_MODEL), lambda i, e: (e, 0, 0)),
                  full(lnw), full(lnb)],
        out_specs=pl.BlockSpec((tm, D_MODEL), lambda i, e: (i, 0)),
        out_shape=jax.ShapeDtypeStruct((m, D_MODEL), f32),
        scratch_shapes=[pltpu.VMEM((tm, D_MODEL), bf16), pltpu.VMEM((tm, LANES), f32),
                        pltpu.VMEM((tm, D_MODEL), f32)],
        compiler_params=pltpu.CompilerParams(
            dimension_semantics=("parallel", "arbitrary"), vmem_limit_bytes=VMEM_LIMIT),
    )(x2, rwh, rwl, rb, wg, wu, wd, lnw, lnb)


def _router_weights(router_group_w, router_group_b, router_expert_w, router_expert_b):
    pad = LANES - N_EXPERTS - N_EXPERT_GROUPS
    rw = jnp.concatenate([router_expert_w.astype(f32), router_group_w.astype(f32),
                          jnp.zeros((D_MODEL, pad), f32)], axis=1)
    rb = jnp.concatenate([router_expert_b.astype(f32), router_group_b.astype(f32),
                          jnp.zeros((pad,), f32)]).reshape(1, LANES)
    return rw, rb


def kernel(x, w_in, rwkv_mu, rwkv_w0, rwkv_w2, rwkv_a0, rwkv_a2, rwkv_g2, rwkv_k_k, rwkv_k_a, rwkv_r_k, rwkv_ln_w, rwkv_ln_b, ssm_a_re, ssm_a_im, ssm_b_re, ssm_b_im, ssm_c_re, ssm_c_im, ssm_d, ssm_log_dt, ssm_glu_w, ssm_glu_b, w_up_rwkv, w_up_moba, w_up_ssm, gate_b, w_out, ln1_w, ln1_b, router_group_w, router_group_b, router_expert_w, router_expert_b, expert_w_gate, expert_w_up, expert_w_down, ln2_w, ln2_b):
    bsz, seq, _ = x.shape
    x2 = x.reshape(bsz * seq, D_MODEL).astype(f32)
    row = lambda a: a.reshape(1, -1).astype(f32)
    for l in range(DEPTH):
        w = w_in[l].astype(bf16)
        z = _project(x2, w[:, :OFF_MOBA])
        qkv = _project(x2, w[:, OFF_MOBA:OFF_SSM])
        u = _project(x2, w[:, OFF_SSM:OFF_GATE])
        gates = _project(x2, w[:, OFF_GATE:], bias=row(gate_b[l]), tn=D_MODEL)
        y_a = _rwkv(z.reshape(bsz, seq, RWKV_COLS),
                    dict(mu=rwkv_mu[l], w0=rwkv_w0[l], w2=rwkv_w2[l], a0=rwkv_a0[l], a2=rwkv_a2[l],
                         g2=rwkv_g2[l], k_k=rwkv_k_k[l], k_a=rwkv_k_a[l], r_k=rwkv_r_k[l],
                         ln_w=rwkv_ln_w[l], ln_b=rwkv_ln_b[l]))
        y_b = _moba(qkv.reshape(bsz, seq, MOBA_COLS))
        y_c = _s5(u.reshape(bsz, seq, D_SSM),
                  dict(a_re=ssm_a_re[l], a_im=ssm_a_im[l], b_re=ssm_b_re[l], b_im=ssm_b_im[l],
                       c_re=ssm_c_re[l], c_im=ssm_c_im[l], d=ssm_d[l], log_dt=ssm_log_dt[l],
                       glu_w=ssm_glu_w[l], glu_b=ssm_glu_b[l]))
        x2 = _merge(x2, y_a.reshape(-1, D_RWKV), y_b.reshape(-1, D_MOBA), y_c.reshape(-1, D_SSM), gates,
                    w_up_rwkv[l].astype(bf16), w_up_moba[l].astype(bf16), w_up_ssm[l].astype(bf16),
                    w_out[l].astype(bf16), row(ln1_w[l]), row(ln1_b[l]))
        rw, rb = _router_weights(router_group_w[l], router_group_b[l], router_expert_w[l],
                                 router_expert_b[l])
        x2 = _moe(x2, rw, rb, expert_w_gate[l].astype(bf16), expert_w_up[l].astype(bf16),
                  expert_w_down[l].astype(bf16), row(ln2_w[l]), row(ln2_b[l]))
    return x2.reshape(bsz, seq, D_MODEL)
```

```python
import functools

import jax
import jax.numpy as jnp
import numpy as np
from jax import lax
from jax.experimental import pallas as pl
from jax.experimental.pallas import tpu as pltpu

f32 = jnp.float32
bf16 = jnp.bfloat16

D_MODEL = 1024
DEPTH = 4
HEAD_DIM = 64
D_RWKV = 512
RWKV_HEADS = D_RWKV // HEAD_DIM
DECAY_LORA = 64
ICL_LORA = 64
GATE_LORA = 128
RWKV_GN_EPS = 64e-5
D_MOBA = 512
MOBA_HEADS = D_MOBA // HEAD_DIM
MOBA_BLOCK = 256
MOBA_TOPK = 3
D_SSM = 512
SSM_GROUP = 16
SSM_GROUPS = D_SSM // SSM_GROUP
SSM_STATE = 64
N_BRANCHES = 3
N_EXPERT_GROUPS = 4
EXPERTS_PER_GROUP = 8
N_EXPERTS = N_EXPERT_GROUPS * EXPERTS_PER_GROUP
D_EXPERT = D_MODEL // 4
LN_EPS = 1e-5
DEEPNORM_ALPHA = (2 * DEPTH) ** 0.25
NEG_INF = -1e30
RWKV_COLS = 3 * D_RWKV + DECAY_LORA + ICL_LORA + GATE_LORA
MOBA_COLS = 3 * D_MOBA
OFF_MOBA = RWKV_COLS
OFF_SSM = OFF_MOBA + MOBA_COLS
OFF_GATE = OFF_SSM + D_SSM

LANES = 128
WKV_CHUNK = 64
VMEM_LIMIT = 48 * 1024 * 1024


def _nt(a, b):
    return lax.dot_general(a, b, (((1,), (1,)), ((), ())), preferred_element_type=f32)


def _tn(a, b):
    return lax.dot_general(a, b, (((0,), (0,)), ((), ())), preferred_element_type=f32)


def _mm(a, b):
    return jnp.dot(a, b, preferred_element_type=f32)


def _split2(x):
    hi = x.astype(bf16)
    lo = (x - hi.astype(f32)).astype(bf16)
    return hi, lo


def _layer_norm(y, w, b):
    mu = jnp.mean(y, axis=-1, keepdims=True)
    d = y - mu
    var = jnp.mean(d * d, axis=-1, keepdims=True)
    return d * lax.rsqrt(var + LN_EPS) * w + b


def _proj_kernel(x_ref, w_ref, o_ref):
    o_ref[...] = _mm(x_ref[...].astype(bf16), w_ref[...])


def _proj_gate_kernel(x_ref, w_ref, b_ref, o_ref):
    o_ref[...] = jax.nn.sigmoid(_mm(x_ref[...].astype(bf16), w_ref[...]) + b_ref[...])


def _project(x2, w, bias=None, tm=512, tn=None):
    m, k = x2.shape
    n = w.shape[1]
    tn = n if tn is None else tn
    in_specs = [pl.BlockSpec((tm, k), lambda i, j: (i, 0)),
                pl.BlockSpec((k, tn), lambda i, j: (0, j))]
    args = [x2, w]
    body = _proj_kernel
    if bias is not None:
        in_specs.append(pl.BlockSpec((1, tn), lambda i, j: (0, j)))
        args.append(bias)
        body = _proj_gate_kernel
    return pl.pallas_call(
        body,
        grid=(m // tm, n // tn),
        in_specs=in_specs,
        out_specs=pl.BlockSpec((tm, tn), lambda i, j: (i, j)),
        out_shape=jax.ShapeDtypeStruct((m, n), f32),
        compiler_params=pltpu.CompilerParams(
            dimension_semantics=("parallel", "parallel"), vmem_limit_bytes=VMEM_LIMIT),
    )(*args)


def _rwkv_kernel(z_ref, mu_ref, w0_ref, w2_ref, a0_ref, a2_ref, g2_ref, kk_ref, ka_ref, rk_ref,
                 lnw_ref, lnb_ref, seg_ref, tri_ref, o_ref,
                 zlast_s, state_s, r_s, k_s, v_s, a_s, b_s, lw_s, g_s, bon_s, y_s):
    tb = z_ref.shape[1]
    n_chunks = tb // WKV_CHUNK
    c64 = WKV_CHUNK

    @pl.when(pl.program_id(1) == 0)
    def _():
        zlast_s[...] = jnp.zeros_like(zlast_s)
        state_s[...] = jnp.zeros_like(state_s)

    seg = seg_ref[...]

    def segsum(x):
        hi, lo = _split2(x)
        return _mm(hi, seg) + _mm(lo, seg)

    z = z_ref[0]
    row = lax.broadcasted_iota(jnp.int32, z.shape, 0)
    z_prev = jnp.where(row == 0, zlast_s[...], pltpu.roll(z, 1, axis=0))
    zlast_s[...] = z[tb - 1:tb, :]
    zz = z + (z_prev - z) * mu_ref[...]
    r = zz[:, 0:D_RWKV]
    k = zz[:, D_RWKV:2 * D_RWKV]
    v = zz[:, 2 * D_RWKV:3 * D_RWKV]
    o1 = 3 * D_RWKV
    xw = zz[:, o1:o1 + DECAY_LORA]
    xa = zz[:, o1 + DECAY_LORA:o1 + DECAY_LORA + ICL_LORA]
    xg = zz[:, o1 + DECAY_LORA + ICL_LORA:]
    wl = w0_ref[...] + _mm(jnp.tanh(xw).astype(bf16), w2_ref[...])
    softplus = jnp.maximum(-wl, 0.0) + jnp.log(1.0 + jnp.exp(-jnp.abs(wl)))
    lw_s[...] = -jnp.exp(-softplus - 0.5)
    a_icl = jax.nn.sigmoid(a0_ref[...] + _mm(xa.astype(bf16), a2_ref[...]))
    g_s[...] = _mm(jax.nn.sigmoid(xg).astype(bf16), g2_ref[...])
    kk = k * kk_ref[...]
    kk = kk * lax.rsqrt(jnp.maximum(segsum(kk * kk), 1e-24))
    k2 = k * (1.0 + (a_icl - 1.0) * ka_ref[...])
    r_s[...] = r
    k_s[...] = k2
    v_s[...] = v
    a_s[...] = -kk
    b_s[...] = kk * a_icl
    bon_s[...] = segsum(r * k2 * rk_ref[...])

    ri = lax.broadcasted_iota(jnp.int32, (c64, c64), 0)
    ci = lax.broadcasted_iota(jnp.int32, (c64, c64), 1)
    strict = ri > ci
    incl = ri >= ci
    eye = jnp.where(ri == ci, 1.0, 0.0).astype(f32)
    tri = tri_ref[...]

    def chunk_body(c, carry):
        rows = pl.ds(pl.multiple_of(c * c64, c64), c64)
        lwc = lw_s[rows, :]
        lhi, llo = _split2(lwc)
        cl = _mm(tri, lhi) + _mm(tri, llo)
        cl_last = cl[c64 - 1:c64, :]
        e_cl = jnp.exp(cl)
        e_prev = jnp.exp(cl - lwc)
        e_neg = jnp.exp(-cl)
        e_tot = jnp.exp(cl_last - cl)
        p_c = jnp.exp(cl_last)
        rc = r_s[rows, :]
        kc = k_s[rows, :]
        vc = v_s[rows, :]
        ac = a_s[rows, :]
        bc = b_s[rows, :]
        a_t = ac * e_prev
        r_t = rc * e_cl
        b_t = bc * e_neg
        k_t = kc * e_neg
        b_h = bc * e_tot
        k_h = kc * e_tot
        heads = range(RWKV_HEADS)
        sls = [slice(h * HEAD_DIM, (h + 1) * HEAD_DIM) for h in heads]
        hcat = lambda p, q, axis: [jnp.concatenate([p[:, s], q[:, s]], axis=axis).astype(bf16) for s in sls]
        ar = hcat(a_t, r_t, 0)
        bk = hcat(b_t, k_t, 0)
        gm = [_nt(ar[h], bk[h]) for h in heads]
        a_ab = [jnp.where(strict, g[:c64, :c64], 0.0) for g in gm]
        a_ak = [jnp.where(strict, g[:c64, c64:], 0.0).astype(bf16) for g in gm]
        a_rb = [jnp.where(incl, g[c64:, :c64], 0.0).astype(bf16) for g in gm]
        a_rk = [jnp.where(incl, g[c64:, c64:], 0.0).astype(bf16) for g in gm]
        vh = [vc[:, s] for s in sls]
        vb = [x.astype(bf16) for x in vh]
        akv = [_mm(a_ak[h], vb[h]) for h in heads]
        yloc = [_mm(a_rk[h], vb[h]) for h in heads]
        xs = a_ab
        t_inv = [eye + x for x in xs]
        for _ in range(5):
            xb = [x.astype(bf16) for x in xs]
            xs = [_mm(x, x) for x in xb]
            t_inv = [t + _mm(t.astype(bf16), x.astype(bf16)) for t, x in zip(t_inv, xs)]
        rhs = [jnp.concatenate([a_t[:, sls[h]], akv[h]], axis=1).astype(bf16) for h in heads]
        tw = [_mm(t_inv[h].astype(bf16), rhs[h]) for h in heads]
        s_old = [state_s[h] for h in heads]
        wr = [jnp.concatenate([tw[h][:, :c64], r_t[:, sls[h]]], axis=0).astype(bf16) for h in heads]
        wrs = [_nt(wr[h], s_old[h].astype(bf16)) for h in heads]
        u = [wrs[h][:c64] + tw[h][:, c64:] for h in heads]
        ys = [wrs[h][c64:] + _mm(a_rb[h], u[h].astype(bf16)) + yloc[h] for h in heads]
        uv = [jnp.concatenate([u[h], vh[h]], axis=0).astype(bf16) for h in heads]
        bkh = hcat(b_h, k_h, 0)
        for h in heads:
            state_s[h] = s_old[h] * p_c[:, sls[h]] + _tn(uv[h], bkh[h])
        y_s[rows, :] = jnp.concatenate(ys, axis=1)
        return carry

    lax.fori_loop(0, n_chunks, chunk_body, 0)

    y = y_s[...]
    mean = segsum(y) * (1.0 / HEAD_DIM)
    d = y - mean
    var = segsum(d * d) * (1.0 / HEAD_DIM)
    yn = d * lax.rsqrt(var + RWKV_GN_EPS) * lnw_ref[...] + lnb_ref[...]
    o_ref[0] = (yn + bon_s[...] * v_s[...]) * g_s[...]


def _rwkv(z3, p, tb=512):
    bsz, seq, _ = z3.shape
    tb = min(tb, seq)
    seg = (np.arange(D_RWKV)[:, None] // HEAD_DIM == np.arange(D_RWKV)[None, :] // HEAD_DIM)
    seg = jnp.asarray(seg, bf16)
    tri = jnp.asarray(np.tril(np.ones((WKV_CHUNK, WKV_CHUNK))), bf16)
    row = lambda a: a.reshape(1, -1).astype(f32)
    args = [z3, row(p['mu']), row(p['w0']), p['w2'].astype(bf16), row(p['a0']), p['a2'].astype(bf16),
            p['g2'].astype(bf16), row(p['k_k']), row(p['k_a']), row(p['r_k']), row(p['ln_w']),
            row(p['ln_b']), seg, tri]
    full = lambda a: pl.BlockSpec(a.shape, lambda b, l: (0,) * a.ndim)
    in_specs = [pl.BlockSpec((1, tb, RWKV_COLS), lambda b, l: (b, l, 0))] + [full(a) for a in args[1:]]
    big = lambda: pltpu.VMEM((tb, D_RWKV), f32)
    return pl.pallas_call(
        _rwkv_kernel,
        grid=(bsz, seq // tb),
        in_specs=in_specs,
        out_specs=pl.BlockSpec((1, tb, D_RWKV), lambda b, l: (b, l, 0)),
        out_shape=jax.ShapeDtypeStruct((bsz, seq, D_RWKV), f32),
        scratch_shapes=[pltpu.VMEM((1, RWKV_COLS), f32),
                        pltpu.VMEM((RWKV_HEADS, HEAD_DIM, HEAD_DIM), f32),
                        big(), big(), big(), big(), big(), big(), big(), big(), big()],
        compiler_params=pltpu.CompilerParams(
            dimension_semantics=("parallel", "arbitrary"), vmem_limit_bytes=VMEM_LIMIT),
    )(*args)


MOBA_AUG_POS = 0
MOBA_AUG_BLK = 6
LOG2E = 1.4426950408889634


def _moba_kernel(q_ref, k_ref, v_ref, tmpl_ref, o_ref, kaug_s, vaug_s, kmean_s):
    blk = MOBA_BLOCK
    half = HEAD_DIM
    seq = k_ref.shape[1]
    n_blk = seq // blk
    qi = pl.program_id(2)
    heads = range(2)
    aug0 = [half, 0]

    @pl.when(qi == 0)
    def _():
        lane_l = lax.broadcasted_iota(jnp.int32, (seq, LANES), 1)
        kb = k_ref[0].astype(bf16)
        vb = v_ref[0].astype(bf16)
        for hh in heads:
            in_head = (lane_l < half) if hh == 0 else (lane_l >= half)
            kaug_s[hh] = jnp.where(in_head, kb, tmpl_ref[0, hh])
            ones_lane = jnp.where(lane_l == aug0[hh], 1.0, 0.0).astype(bf16)
            vaug_s[hh] = jnp.where(in_head, vb, ones_lane)
        kmean_s[...] = jnp.zeros_like(kmean_s)
        for n in range(n_blk):
            km = jnp.mean(k_ref[0, n * blk:(n + 1) * blk, :], axis=0, keepdims=True)
            kmean_s[MOBA_AUG_BLK + n:MOBA_AUG_BLK + n + 1, :] = km
            kmean_s[half + MOBA_AUG_BLK + n:half + MOBA_AUG_BLK + n + 1, :] = km

    ri = lax.broadcasted_iota(jnp.int32, (blk, blk), 0)
    ci = lax.broadcasted_iota(jnp.int32, (blk, blk), 1)
    causal = ri >= ci
    lane = lax.broadcasted_iota(jnp.int32, (blk, LANES), 1)
    lane_f = lane.astype(f32)
    qf = q_ref[0] * (HEAD_DIM ** -0.5 * LOG2E)
    kmh, kml = _split2(kmean_s[...])

    q_aug, q_own = [], []
    for hh in heads:
        in_head = (lane < half) if hh == 0 else (lane >= half)
        qm = jnp.where(in_head, qf, 0.0)
        qh, ql = _split2(qm)
        gate = _nt(qh, kmh) + _nt(qh, kml) + _nt(ql, kmh)
        blk_lane = lane - (aug0[hh] + MOBA_AUG_BLK)
        cand = (blk_lane >= 0) & (blk_lane < qi)
        g = jnp.where(cand, gate, -jnp.inf)
        picked = jnp.zeros((blk, LANES), f32)
        for _ in range(MOBA_TOPK):
            mx = jnp.max(g, axis=-1, keepdims=True)
            idx = jnp.min(jnp.where(g == mx, lane_f, 1e9), axis=-1, keepdims=True)
            hit = lane_f == idx
            picked = jnp.where(hit, 1.0, picked)
            g = jnp.where(hit, -jnp.inf, g)
        sel = (picked > 0.0) & cand
        is_pos = (lane >= aug0[hh] + MOBA_AUG_POS) & (lane < aug0[hh] + MOBA_AUG_POS + 3)
        is_blk = (blk_lane >= 0) & (blk_lane < n_blk)
        own = jnp.where(is_pos, 1.0, 0.0)
        aug = jnp.where(is_blk, jnp.where(sel, 0.0, NEG_INF), own)
        q_aug.append(jnp.where(in_head, qf, aug).astype(bf16))
        q_own.append(jnp.where(in_head, qf, own).astype(bf16))

    start = pl.multiple_of(qi * blk, blk)
    s0 = [_nt(q_own[hh], kaug_s[hh, pl.ds(start, blk), :]) for hh in heads]
    s0 = [jnp.where(causal, s, NEG_INF) for s in s0]
    m0 = [jnp.max(s, axis=-1, keepdims=True) for s in s0]
    p0 = [jnp.exp2(s - m).astype(bf16) for s, m in zip(s0, m0)]
    a0 = [_mm(p0[hh], vaug_s[hh, pl.ds(start, blk), :]) for hh in heads]

    def pair_body(t, carry):
        ms, accs = carry
        rows = pl.ds(pl.multiple_of(t * (2 * blk), 2 * blk), 2 * blk)
        s = [_nt(q_aug[hh], kaug_s[hh, rows, :]) for hh in heads]
        m_new = [jnp.maximum(ms[hh], jnp.max(s[hh], axis=-1, keepdims=True)) for hh in heads]
        p = [jnp.exp2(s[hh] - m_new[hh]).astype(bf16) for hh in heads]
        pv = [_mm(p[hh], vaug_s[hh, rows, :]) for hh in heads]
        accs = [accs[hh] * jnp.exp2(ms[hh] - m_new[hh]) + pv[hh] for hh in heads]
        return (tuple(m_new), tuple(accs))

    _, accs = lax.fori_loop(0, (qi + 1) // 2, pair_body, (tuple(m0), tuple(a0)))
    out = [accs[hh] / accs[hh][:, aug0[hh]:aug0[hh] + 1] for hh in heads]
    o_ref[0] = jnp.where(lane < half, out[0], out[1])


def _moba_template(seq):
    n_pairs = MOBA_HEADS // 2
    pos = np.arange(seq, dtype=np.float64)
    tmpl = np.zeros((n_pairs, 2, seq, LANES), np.float32)
    for h in range(MOBA_HEADS):
        hh = h % 2
        a0 = HEAD_DIM if hh == 0 else 0
        slope = 2.0 ** (-8.0 * (h + 1) / MOBA_HEADS)
        rest = slope * LOG2E * pos
        for c in range(3):
            part = rest.astype(np.float32).astype(bf16).astype(np.float64)
            tmpl[h // 2, hh, :, a0 + MOBA_AUG_POS + c] = part
            rest = rest - part
        tmpl[h // 2, hh, np.arange(seq), a0 + MOBA_AUG_BLK + np.arange(seq) // MOBA_BLOCK] = 1.0
    return jnp.asarray(tmpl, bf16)


def _moba(qkv3):
    bsz, seq, _ = qkv3.shape
    blk = MOBA_BLOCK
    n_pairs = MOBA_HEADS // 2
    assert seq % (2 * blk) == 0 and seq // blk <= HEAD_DIM - MOBA_AUG_BLK
    return pl.pallas_call(
        _moba_kernel,
        grid=(bsz, n_pairs, seq // blk),
        in_specs=[pl.BlockSpec((1, blk, LANES), lambda b, p, i: (b, i, p)),
                  pl.BlockSpec((1, seq, LANES), lambda b, p, i: (b, 0, n_pairs + p)),
                  pl.BlockSpec((1, seq, LANES), lambda b, p, i: (b, 0, 2 * n_pairs + p)),
                  pl.BlockSpec((1, 2, seq, LANES), lambda b, p, i: (p, 0, 0, 0))],
        out_specs=pl.BlockSpec((1, blk, LANES), lambda b, p, i: (b, i, p)),
        out_shape=jax.ShapeDtypeStruct((bsz, seq, D_MOBA), f32),
        scratch_shapes=[pltpu.VMEM((2, seq, LANES), bf16), pltpu.VMEM((2, seq, LANES), bf16),
                        pltpu.VMEM((LANES, LANES), f32)],
        compiler_params=pltpu.CompilerParams(
            dimension_semantics=("parallel", "parallel", "arbitrary"), vmem_limit_bytes=VMEM_LIMIT),
    )(qkv3, qkv3, qkv3, _moba_template(seq))


def _s5_kernel(u_ref, bd_ref, cd_ref, lam_ref, d_ref, gw_ref, gb_ref, o_ref, h_s, carry_s):
    tb = u_ref.shape[1]
    n_state = lam_ref.shape[1]

    @pl.when(pl.program_id(1) == 0)
    def _():
        carry_s[...] = jnp.zeros_like(carry_s)

    u = u_ref[0]
    h_s[...] = _mm(u.astype(bf16), bd_ref[...])
    lr = lam_ref[0:1, :]
    li = lam_ref[1:2, :]

    def step(t, carry):
        hr, hi = carry
        br = h_s[pl.ds(t, 1), 0:n_state]
        bi = h_s[pl.ds(t, 1), n_state:2 * n_state]
        nr = lr * hr - li * hi + br
        ni = lr * hi + li * hr + bi
        h_s[pl.ds(t, 1), 0:n_state] = nr
        h_s[pl.ds(t, 1), n_state:2 * n_state] = ni
        return nr, ni

    hr, hi = lax.fori_loop(0, tb, step, (carry_s[0:1, :], carry_s[1:2, :]), unroll=8)
    carry_s[0:1, :] = hr
    carry_s[1:2, :] = hi
    y = _mm(h_s[...].astype(bf16), cd_ref[...]) + d_ref[...] * u
    y = 0.5 * y * (1.0 + jnp.tanh(0.7978845608028654 * (y + 0.044715 * (y * y * y))))
    zg = _mm(y.astype(bf16), gw_ref[...]) + gb_ref[...]
    o_ref[0] = zg[:, :D_SSM] * jax.nn.sigmoid(zg[:, D_SSM:])


def _s5_params(p):
    a_re = p['a_re'].astype(f32)
    a_im = p['a_im'].astype(f32)
    dt = jnp.exp(p['log_dt'].astype(f32))[:, None]
    mag = jnp.exp(a_re * dt)
    lam_re = mag * jnp.cos(a_im * dt)
    lam_im = mag * jnp.sin(a_im * dt)
    den = a_re * a_re + a_im * a_im
    nr = lam_re - 1.0
    coef_re = (nr * a_re + lam_im * a_im) / den
    coef_im = (lam_im * a_re - nr * a_im) / den
    b_re = p['b_re'].astype(f32)
    b_im = p['b_im'].astype(f32)
    bb_re = coef_re[..., None] * b_re - coef_im[..., None] * b_im
    bb_im = coef_re[..., None] * b_im + coef_im[..., None] * b_re
    eye = jnp.eye(SSM_GROUPS, dtype=f32)
    n_state = SSM_GROUPS * SSM_STATE
    bd = jnp.concatenate([jnp.einsum('gph,gk->ghkp', bb_re, eye).reshape(D_SSM, n_state),
                          jnp.einsum('gph,gk->ghkp', bb_im, eye).reshape(D_SSM, n_state)], axis=1)
    cd = jnp.concatenate([jnp.einsum('ghp,gk->gpkh', p['c_re'].astype(f32), eye).reshape(n_state, D_SSM),
                          -jnp.einsum('ghp,gk->gpkh', p['c_im'].astype(f32), eye).reshape(n_state, D_SSM)],
                         axis=0)
    lam = jnp.stack([lam_re.reshape(n_state), lam_im.reshape(n_state)], axis=0)
    return bd.astype(bf16), cd.astype(bf16), lam


def _s5(u3, p, tb=256):
    bsz, seq, _ = u3.shape
    tb = min(tb, seq)
    n_state = SSM_GROUPS * SSM_STATE
    bd, cd, lam = _s5_params(p)
    args = [u3, bd, cd, lam, p['d'].reshape(1, -1).astype(f32), p['glu_w'].astype(bf16),
            p['glu_b'].reshape(1, -1).astype(f32)]
    full = lambda a: pl.BlockSpec(a.shape, lambda b, l: (0,) * a.ndim)
    return pl.pallas_call(
        _s5_kernel,
        grid=(bsz, seq // tb),
        in_specs=[pl.BlockSpec((1, tb, D_SSM), lambda b, l: (b, l, 0))] + [full(a) for a in args[1:]],
        out_specs=pl.BlockSpec((1, tb, D_SSM), lambda b, l: (b, l, 0)),
        out_shape=jax.ShapeDtypeStruct((bsz, seq, D_SSM), f32),
        scratch_shapes=[pltpu.VMEM((tb, 2 * n_state), f32), pltpu.VMEM((2, n_state), f32)],
        compiler_params=pltpu.CompilerParams(
            dimension_semantics=("parallel", "arbitrary"), vmem_limit_bytes=VMEM_LIMIT),
    )(*args)


def _merge_kernel(x_ref, ya_ref, yb_ref, yc_ref, g_ref, wa_ref, wb_ref, wc_ref, wo_ref,
                  lnw_ref, lnb_ref, o_ref):
    g = g_ref[...]
    merged = (g[:, 0:D_MODEL] * _mm(ya_ref[...].astype(bf16), wa_ref[...])
              + g[:, D_MODEL:2 * D_MODEL] * _mm(yb_ref[...].astype(bf16), wb_ref[...])
              + g[:, 2 * D_MODEL:] * _mm(yc_ref[...].astype(bf16), wc_ref[...]))
    h = _mm(merged.astype(bf16), wo_ref[...])
    o_ref[...] = _layer_norm(DEEPNORM_ALPHA * x_ref[...] + h, lnw_ref[...], lnb_ref[...])


def _merge(x2, ya, yb, yc, gates, wa, wb, wc, wo, lnw, lnb, tm=512):
    m = x2.shape[0]
    rowblk = lambda n: pl.BlockSpec((tm, n), lambda i: (i, 0))
    full = lambda a: pl.BlockSpec(a.shape, lambda i: (0,) * a.ndim)
    consts = [wa, wb, wc, wo, lnw, lnb]
    return pl.pallas_call(
        _merge_kernel,
        grid=(m // tm,),
        in_specs=[rowblk(D_MODEL), rowblk(D_RWKV), rowblk(D_MOBA), rowblk(D_SSM),
                  rowblk(N_BRANCHES * D_MODEL)] + [full(a) for a in consts],
        out_specs=rowblk(D_MODEL),
        out_shape=jax.ShapeDtypeStruct((m, D_MODEL), f32),
        compiler_params=pltpu.CompilerParams(
            dimension_semantics=("parallel",), vmem_limit_bytes=VMEM_LIMIT),
    )(x2, ya, yb, yc, gates, *consts)


def _moe_kernel(x_ref, rwh_ref, rwl_ref, rb_ref, wg_ref, wu_ref, wd_ref, lnw_ref, lnb_ref, o_ref,
                xb_s, wt_s, acc_s):
    e = pl.program_id(1)
    tm = x_ref.shape[0]
    lane = lax.broadcasted_iota(jnp.int32, (tm, LANES), 1)
    lane_f = lane.astype(f32)

    @pl.when(e == 0)
    def _():
        x = x_ref[...]
        xh, xl = _split2(x)
        xb_s[...] = xh
        rwh = rwh_ref[...]
        logits = _mm(xh, rwh) + _mm(xh, rwl_ref[...]) + _mm(xl, rwh) + rb_ref[...]
        is_grp = (lane >= N_EXPERTS) & (lane < N_EXPERTS + N_EXPERT_GROUPS)
        gl = jnp.where(is_grp, logits, -jnp.inf)
        gmax = jnp.max(gl, axis=-1, keepdims=True)
        gidx = jnp.min(jnp.where(gl == gmax, lane_f, 1e9), axis=-1, keepdims=True) - float(N_EXPERTS)
        p_group = 1.0 / jnp.sum(jnp.where(is_grp, jnp.exp(gl - gmax), 0.0), axis=-1, keepdims=True)
        grp_of_lane = jnp.floor(lane_f * (1.0 / EXPERTS_PER_GROUP))
        in_grp = (lane < N_EXPERTS) & (grp_of_lane == gidx)
        el = jnp.where(in_grp, logits, -jnp.inf)
        m1 = jnp.max(el, axis=-1, keepdims=True)
        i1 = jnp.min(jnp.where(el == m1, lane_f, 1e9), axis=-1, keepdims=True)
        el2 = jnp.where(lane_f == i1, -jnp.inf, el)
        m2 = jnp.max(el2, axis=-1, keepdims=True)
        i2 = jnp.min(jnp.where(el2 == m2, lane_f, 1e9), axis=-1, keepdims=True)
        e2 = jnp.exp(m2 - m1)
        w1 = p_group / (1.0 + e2)
        w2 = p_group * e2 / (1.0 + e2)
        wt_s[...] = jnp.where(lane_f == i1, w1, 0.0) + jnp.where(lane_f == i2, w2, 0.0)
        acc_s[...] = jnp.zeros_like(acc_s)

    xb = xb_s[...]
    gact = _mm(xb, wg_ref[0])
    up = _mm(xb, wu_ref[0])
    wcol = jnp.sum(jnp.where(lane == e, wt_s[...], 0.0), axis=-1, keepdims=True)
    hid = gact * jax.nn.sigmoid(gact) * up * wcol
    acc_s[...] += _mm(hid.astype(bf16), wd_ref[0])

    @pl.when(e == N_EXPERTS - 1)
    def _():
        o_ref[...] = _layer_norm(DEEPNORM_ALPHA * x_ref[...] + acc_s[...], lnw_ref[...], lnb_ref[...])


def _moe(x2, rw, rb, wg, wu, wd, lnw, lnb, tm=1024):
    m = x2.shape[0]
    tm = min(tm, m)
    rwh, rwl = _split2(rw)
    full = lambda a: pl.BlockSpec(a.shape, lambda i, e: (0,) * a.ndim)
    return pl.pallas_call(
        _moe_kernel,
        grid=(m // tm, N_EXPERTS),
        in_specs=[pl.BlockSpec((tm, D_MODEL), lambda i, e: (i, 0)),
                  full(rwh), full(rwl), full(rb),
                  pl.BlockSpec((1, D_MODEL, D_EXPERT), lambda i, e: (e, 0, 0)),
                  pl.BlockSpec((1, D_MODEL, D_EXPERT), lambda i, e: (e, 0, 0)),
                  pl.BlockSpec((1, D_EXPERT, D_MODEL), lambda i, e: (e, 0, 0)),
                  full(lnw), full(lnb)],
        out_specs=pl.BlockSpec((tm, D_MODEL), lambda i, e: (i, 0)),
        out_shape=jax.ShapeDtypeStruct((m, D_MODEL), f32),
        scratch_shapes=[pltpu.VMEM((tm, D_MODEL), bf16), pltpu.VMEM((tm, LANES), f32),
                        pltpu.VMEM((tm, D_MODEL), f32)],
        compiler_params=pltpu.CompilerParams(
            dimension_semantics=("parallel", "arbitrary"), vmem_limit_bytes=VMEM_LIMIT),
    )(x2, rwh, rwl, rb, wg, wu, wd, lnw, lnb)


def _router_weights(router_group_w, router_group_b, router_expert_w, router_expert_b):
    pad = LANES - N_EXPERTS - N_EXPERT_GROUPS
    rw = jnp.concatenate([router_expert_w.astype(f32), router_group_w.astype(f32),
                          jnp.zeros((D_MODEL, pad), f32)], axis=1)
    rb = jnp.concatenate([router_expert_b.astype(f32), router_group_b.astype(f32),
                          jnp.zeros((pad,), f32)]).reshape(1, LANES)
    return rw, rb


def kernel(x, w_in, rwkv_mu, rwkv_w0, rwkv_w2, rwkv_a0, rwkv_a2, rwkv_g2, rwkv_k_k, rwkv_k_a, rwkv_r_k, rwkv_ln_w, rwkv_ln_b, ssm_a_re, ssm_a_im, ssm_b_re, ssm_b_im, ssm_c_re, ssm_c_im, ssm_d, ssm_log_dt, ssm_glu_w, ssm_glu_b, w_up_rwkv, w_up_moba, w_up_ssm, gate_b, w_out, ln1_w, ln1_b, router_group_w, router_group_b, router_expert_w, router_expert_b, expert_w_gate, expert_w_up, expert_w_down, ln2_w, ln2_b):
    bsz, seq, _ = x.shape
    x2 = x.reshape(bsz * seq, D_MODEL).astype(f32)
    row = lambda a: a.reshape(1, -1).astype(f32)
    for l in range(DEPTH):
        w = w_in[l].astype(bf16)
        z = _project(x2, w[:, :OFF_MOBA])
        qkv = _project(x2, w[:, OFF_MOBA:OFF_SSM])
        u = _project(x2, w[:, OFF_SSM:OFF_GATE])
        gates = _project(x2, w[:, OFF_GATE:], bias=row(gate_b[l]), tn=D_MODEL)
        y_a = _rwkv(z.reshape(bsz, seq, RWKV_COLS),
                    dict(mu=rwkv_mu[l], w0=rwkv_w0[l], w2=rwkv_w2[l], a0=rwkv_a0[l], a2=rwkv_a2[l],
                         g2=rwkv_g2[l], k_k=rwkv_k_k[l], k_a=rwkv_k_a[l], r_k=rwkv_r_k[l],
                         ln_w=rwkv_ln_w[l], ln_b=rwkv_ln_b[l]))
        y_b = _moba(qkv.reshape(bsz, seq, MOBA_COLS))
        y_c = _s5(u.reshape(bsz, seq, D_SSM),
                  dict(a_re=ssm_a_re[l], a_im=ssm_a_im[l], b_re=ssm_b_re[l], b_im=ssm_b_im[l],
                       c_re=ssm_c_re[l], c_im=ssm_c_im[l], d=ssm_d[l], log_dt=ssm_log_dt[l],
                       glu_w=ssm_glu_w[l], glu_b=ssm_glu_b[l]))
        x2 = _merge(x2, y_a.reshape(-1, D_RWKV), y_b.reshape(-1, D_MOBA), y_c.reshape(-1, D_SSM), gates,
                    w_up_rwkv[l].astype(bf16), w_up_moba[l].astype(bf16), w_up_ssm[l].astype(bf16),
                    w_out[l].astype(bf16), row(ln1_w[l]), row(ln1_b[l]))
        rw, rb = _router_weights(router_group_w[l], router_group_b[l], router_expert_w[l],
                                 router_expert_b[l])
        x2 = _moe(x2, rw, rb, expert_w_gate[l].astype(bf16), expert_w_up[l].astype(bf16),
                  expert_w_down[l].astype(bf16), row(ln2_w[l]), row(ln2_b[l]))
    return x2.reshape(bsz, seq, D_MODEL)
```

```python
import functools

import jax
import jax.numpy as jnp
import numpy as np
from jax import lax
from jax.experimental import pallas as pl
from jax.experimental.pallas import tpu as pltpu

f32 = jnp.float32
bf16 = jnp.bfloat16

D_MODEL = 1024
DEPTH = 4
HEAD_DIM = 64
D_RWKV = 512
RWKV_HEADS = D_RWKV // HEAD_DIM
DECAY_LORA = 64
ICL_LORA = 64
GATE_LORA = 128
RWKV_GN_EPS = 64e-5
D_MOBA = 512
MOBA_HEADS = D_MOBA // HEAD_DIM
MOBA_BLOCK = 256
MOBA_TOPK = 3
D_SSM = 512
SSM_GROUP = 16
SSM_GROUPS = D_SSM // SSM_GROUP
SSM_STATE = 64
N_BRANCHES = 3
N_EXPERT_GROUPS = 4
EXPERTS_PER_GROUP = 8
N_EXPERTS = N_EXPERT_GROUPS * EXPERTS_PER_GROUP
D_EXPERT = D_MODEL // 4
LN_EPS = 1e-5
DEEPNORM_ALPHA = (2 * DEPTH) ** 0.25
NEG_INF = -1e30
RWKV_COLS = 3 * D_RWKV + DECAY_LORA + ICL_LORA + GATE_LORA
MOBA_COLS = 3 * D_MOBA
OFF_MOBA = RWKV_COLS
OFF_SSM = OFF_MOBA + MOBA_COLS
OFF_GATE = OFF_SSM + D_SSM

LANES = 128
WKV_CHUNK = 64
VMEM_LIMIT = 48 * 1024 * 1024


def _nt(a, b):
    return lax.dot_general(a, b, (((1,), (1,)), ((), ())), preferred_element_type=f32)


def _tn(a, b):
    return lax.dot_general(a, b, (((0,), (0,)), ((), ())), preferred_element_type=f32)


def _mm(a, b):
    return jnp.dot(a, b, preferred_element_type=f32)


def _split2(x):
    hi = x.astype(bf16)
    lo = (x - hi.astype(f32)).astype(bf16)
    return hi, lo


def _layer_norm(y, w, b):
    mu = jnp.mean(y, axis=-1, keepdims=True)
    d = y - mu
    var = jnp.mean(d * d, axis=-1, keepdims=True)
    return d * lax.rsqrt(var + LN_EPS) * w + b


def _proj_kernel(x_ref, w_ref, o_ref):
    o_ref[...] = _mm(x_ref[...].astype(bf16), w_ref[...])


def _project(x2, w, tm=512):
    m, k = x2.shape
    n = w.shape[1]
    return pl.pallas_call(
        _proj_kernel,
        grid=(m // tm,),
        in_specs=[pl.BlockSpec((tm, k), lambda i: (i, 0)), pl.BlockSpec((k, n), lambda i: (0, 0))],
        out_specs=pl.BlockSpec((tm, n), lambda i: (i, 0)),
        out_shape=jax.ShapeDtypeStruct((m, n), f32),
        compiler_params=pltpu.CompilerParams(
            dimension_semantics=("parallel",), vmem_limit_bytes=VMEM_LIMIT),
    )(x2, w)


def _rwkv_kernel(z_ref, mu_ref, w0_ref, w2_ref, a0_ref, a2_ref, g2_ref, kk_ref, ka_ref, rk_ref,
                 lnw_ref, lnb_ref, seg_ref, tri_ref, o_ref,
                 zlast_s, state_s, r_s, k_s, v_s, a_s, b_s, lw_s, g_s, bon_s, y_s):
    tb = z_ref.shape[1]
    n_chunks = tb // WKV_CHUNK
    c64 = WKV_CHUNK

    @pl.when(pl.program_id(1) == 0)
    def _():
        zlast_s[...] = jnp.zeros_like(zlast_s)
        state_s[...] = jnp.zeros_like(state_s)

    seg = seg_ref[...]

    def segsum(x):
        hi, lo = _split2(x)
        return _mm(hi, seg) + _mm(lo, seg)

    z = z_ref[0]
    row = lax.broadcasted_iota(jnp.int32, z.shape, 0)
    z_prev = jnp.where(row == 0, zlast_s[...], pltpu.roll(z, 1, axis=0))
    zlast_s[...] = z[tb - 1:tb, :]
    zz = z + (z_prev - z) * mu_ref[...]
    r = zz[:, 0:D_RWKV]
    k = zz[:, D_RWKV:2 * D_RWKV]
    v = zz[:, 2 * D_RWKV:3 * D_RWKV]
    o1 = 3 * D_RWKV
    xw = zz[:, o1:o1 + DECAY_LORA]
    xa = zz[:, o1 + DECAY_LORA:o1 + DECAY_LORA + ICL_LORA]
    xg = zz[:, o1 + DECAY_LORA + ICL_LORA:]
    wl = w0_ref[...] + _mm(jnp.tanh(xw).astype(bf16), w2_ref[...])
    softplus = jnp.maximum(-wl, 0.0) + jnp.log(1.0 + jnp.exp(-jnp.abs(wl)))
    lw_s[...] = -jnp.exp(-softplus - 0.5)
    a_icl = jax.nn.sigmoid(a0_ref[...] + _mm(xa.astype(bf16), a2_ref[...]))
    g_s[...] = _mm(jax.nn.sigmoid(xg).astype(bf16), g2_ref[...])
    kk = k * kk_ref[...]
    kk = kk * lax.rsqrt(jnp.maximum(segsum(kk * kk), 1e-24))
    k2 = k * (1.0 + (a_icl - 1.0) * ka_ref[...])
    r_s[...] = r
    k_s[...] = k2
    v_s[...] = v
    a_s[...] = -kk
    b_s[...] = kk * a_icl
    bon_s[...] = segsum(r * k2 * rk_ref[...])

    ri = lax.broadcasted_iota(jnp.int32, (c64, c64), 0)
    ci = lax.broadcasted_iota(jnp.int32, (c64, c64), 1)
    strict = ri > ci
    incl = ri >= ci
    eye = jnp.where(ri == ci, 1.0, 0.0).astype(f32)
    tri = tri_ref[...]

    def chunk_body(c, carry):
        rows = pl.ds(pl.multiple_of(c * c64, c64), c64)
        lwc = lw_s[rows, :]
        lhi, llo = _split2(lwc)
        cl = _mm(tri, lhi) + _mm(tri, llo)
        cl_last = cl[c64 - 1:c64, :]
        e_cl = jnp.exp(cl)
        e_prev = jnp.exp(cl - lwc)
        e_neg = jnp.exp(-cl)
        e_tot = jnp.exp(cl_last - cl)
        p_c = jnp.exp(cl_last)
        rc = r_s[rows, :]
        kc = k_s[rows, :]
        vc = v_s[rows, :]
        ac = a_s[rows, :]
        bc = b_s[rows, :]
        a_t = ac * e_prev
        r_t = rc * e_cl
        b_t = bc * e_neg
        k_t = kc * e_neg
        b_h = bc * e_tot
        k_h = kc * e_tot
        heads = range(RWKV_HEADS)
        sls = [slice(h * HEAD_DIM, (h + 1) * HEAD_DIM) for h in heads]
        hcat = lambda p, q, axis: [jnp.concatenate([p[:, s], q[:, s]], axis=axis).astype(bf16) for s in sls]
        ar = hcat(a_t, r_t, 0)
        bk = hcat(b_t, k_t, 0)
        gm = [_nt(ar[h], bk[h]) for h in heads]
        a_ab = [jnp.where(strict, g[:c64, :c64], 0.0) for g in gm]
        a_ak = [jnp.where(strict, g[:c64, c64:], 0.0).astype(bf16) for g in gm]
        a_rb = [jnp.where(incl, g[c64:, :c64], 0.0).astype(bf16) for g in gm]
        a_rk = [jnp.where(incl, g[c64:, c64:], 0.0).astype(bf16) for g in gm]
        vh = [vc[:, s] for s in sls]
        vb = [x.astype(bf16) for x in vh]
        akv = [_mm(a_ak[h], vb[h]) for h in heads]
        yloc = [_mm(a_rk[h], vb[h]) for h in heads]
        xs = a_ab
        t_inv = [eye + x for x in xs]
        for _ in range(5):
            xb = [x.astype(bf16) for x in xs]
            xs = [_mm(x, x) for x in xb]
            t_inv = [t + _mm(t.astype(bf16), x.astype(bf16)) for t, x in zip(t_inv, xs)]
        rhs = [jnp.concatenate([a_t[:, sls[h]], akv[h]], axis=1).astype(bf16) for h in heads]
        tw = [_mm(t_inv[h].astype(bf16), rhs[h]) for h in heads]
        s_old = [state_s[h] for h in heads]
        wr = [jnp.concatenate([tw[h][:, :c64], r_t[:, sls[h]]], axis=0).astype(bf16) for h in heads]
        wrs = [_nt(wr[h], s_old[h].astype(bf16)) for h in heads]
        u = [wrs[h][:c64] + tw[h][:, c64:] for h in heads]
        ys = [wrs[h][c64:] + _mm(a_rb[h], u[h].astype(bf16)) + yloc[h] for h in heads]
        uv = [jnp.concatenate([u[h], vh[h]], axis=0).astype(bf16) for h in heads]
        bkh = hcat(b_h, k_h, 0)
        for h in heads:
            state_s[h] = s_old[h] * p_c[:, sls[h]] + _tn(uv[h], bkh[h])
        y_s[rows, :] = jnp.concatenate(ys, axis=1)
        return carry

    lax.fori_loop(0, n_chunks, chunk_body, 0)

    y = y_s[...]
    mean = segsum(y) * (1.0 / HEAD_DIM)
    d = y - mean
    var = segsum(d * d) * (1.0 / HEAD_DIM)
    yn = d * lax.rsqrt(var + RWKV_GN_EPS) * lnw_ref[...] + lnb_ref[...]
    o_ref[0] = (yn + bon_s[...] * v_s[...]) * g_s[...]


def _rwkv(z3, p, tb=512):
    bsz, seq, _ = z3.shape
    tb = min(tb, seq)
    seg = (np.arange(D_RWKV)[:, None] // HEAD_DIM == np.arange(D_RWKV)[None, :] // HEAD_DIM)
    seg = jnp.asarray(seg, bf16)
    tri = jnp.asarray(np.tril(np.ones((WKV_CHUNK, WKV_CHUNK))), bf16)
    row = lambda a: a.reshape(1, -1).astype(f32)
    args = [z3, row(p['mu']), row(p['w0']), p['w2'].astype(bf16), row(p['a0']), p['a2'].astype(bf16),
            p['g2'].astype(bf16), row(p['k_k']), row(p['k_a']), row(p['r_k']), row(p['ln_w']),
            row(p['ln_b']), seg, tri]
    full = lambda a: pl.BlockSpec(a.shape, lambda b, l: (0,) * a.ndim)
    in_specs = [pl.BlockSpec((1, tb, RWKV_COLS), lambda b, l: (b, l, 0))] + [full(a) for a in args[1:]]
    big = lambda: pltpu.VMEM((tb, D_RWKV), f32)
    return pl.pallas_call(
        _rwkv_kernel,
        grid=(bsz, seq // tb),
        in_specs=in_specs,
        out_specs=pl.BlockSpec((1, tb, D_RWKV), lambda b, l: (b, l, 0)),
        out_shape=jax.ShapeDtypeStruct((bsz, seq, D_RWKV), f32),
        scratch_shapes=[pltpu.VMEM((1, RWKV_COLS), f32),
                        pltpu.VMEM((RWKV_HEADS, HEAD_DIM, HEAD_DIM), f32),
                        big(), big(), big(), big(), big(), big(), big(), big(), big()],
        compiler_params=pltpu.CompilerParams(
            dimension_semantics=("parallel", "arbitrary"), vmem_limit_bytes=VMEM_LIMIT),
    )(*args)


MOBA_AUG_POS = 0
MOBA_AUG_BLK = 6
LOG2E = 1.4426950408889634


MOBA_STEP_HEADS = 4


def _moba_kernel(q_ref, k_ref, v_ref, tmpl_ref, o_ref, kaug_s, vaug_s, kmean_s):
    blk = MOBA_BLOCK
    half = HEAD_DIM
    seq = k_ref.shape[1]
    n_blk = seq // blk
    qi = pl.program_id(2)
    heads = range(MOBA_STEP_HEADS)
    tile = [slice((hh // 2) * LANES, (hh // 2 + 1) * LANES) for hh in heads]
    aug0 = [half if hh % 2 == 0 else 0 for hh in heads]

    @pl.when(qi == 0)
    def _():
        lane_l = lax.broadcasted_iota(jnp.int32, (seq, LANES), 1)
        for hh in heads:
            kb = k_ref[0, :, tile[hh]].astype(bf16)
            vb = v_ref[0, :, tile[hh]].astype(bf16)
            in_head = (lane_l < half) if hh % 2 == 0 else (lane_l >= half)
            kaug_s[hh] = jnp.where(in_head, kb, tmpl_ref[0, hh])
            ones_lane = jnp.where(lane_l == aug0[hh], 1.0, 0.0).astype(bf16)
            vaug_s[hh] = jnp.where(in_head, vb, ones_lane)
        kmean_s[...] = jnp.zeros_like(kmean_s)
        for n in range(n_blk):
            km = jnp.mean(k_ref[0, n * blk:(n + 1) * blk, :], axis=0, keepdims=True)
            for tt in range(MOBA_STEP_HEADS // 2):
                kt = km[:, tt * LANES:(tt + 1) * LANES]
                kmean_s[tt, MOBA_AUG_BLK + n:MOBA_AUG_BLK + n + 1, :] = kt
                kmean_s[tt, half + MOBA_AUG_BLK + n:half + MOBA_AUG_BLK + n + 1, :] = kt

    ri = lax.broadcasted_iota(jnp.int32, (blk, blk), 0)
    ci = lax.broadcasted_iota(jnp.int32, (blk, blk), 1)
    causal = ri >= ci
    lane = lax.broadcasted_iota(jnp.int32, (blk, LANES), 1)
    lane_f = lane.astype(f32)
    qf = [q_ref[0, :, tile[hh]] * (HEAD_DIM ** -0.5 * LOG2E) for hh in heads]
    in_head = [(lane < half) if hh % 2 == 0 else (lane >= half) for hh in heads]
    blk_lane = [lane - (aug0[hh] + MOBA_AUG_BLK) for hh in heads]
    cand = [(blk_lane[hh] >= 0) & (blk_lane[hh] < qi) for hh in heads]

    qs = [_split2(jnp.where(in_head[hh], qf[hh], 0.0)) for hh in heads]
    kms = [_split2(kmean_s[tt]) for tt in range(MOBA_STEP_HEADS // 2)]
    gate = [_nt(qs[hh][0], kms[hh // 2][0]) + _nt(qs[hh][0], kms[hh // 2][1])
            + _nt(qs[hh][1], kms[hh // 2][0]) for hh in heads]
    g = [jnp.where(cand[hh], gate[hh], -jnp.inf) for hh in heads]
    picked = [jnp.zeros((blk, LANES), f32) for _ in heads]
    for _ in range(MOBA_TOPK):
        mx = [jnp.max(x, axis=-1, keepdims=True) for x in g]
        idx = [jnp.min(jnp.where(g[hh] == mx[hh], lane_f, 1e9), axis=-1, keepdims=True) for hh in heads]
        hit = [lane_f == i for i in idx]
        picked = [jnp.where(hit[hh], 1.0, picked[hh]) for hh in heads]
        g = [jnp.where(hit[hh], -jnp.inf, g[hh]) for hh in heads]
    q_aug, q_own = [], []
    for hh in heads:
        sel = (picked[hh] > 0.0) & cand[hh]
        is_pos = (lane >= aug0[hh] + MOBA_AUG_POS) & (lane < aug0[hh] + MOBA_AUG_POS + 3)
        is_blk = (blk_lane[hh] >= 0) & (blk_lane[hh] < n_blk)
        own = jnp.where(is_pos, 1.0, 0.0)
        aug = jnp.where(is_blk, jnp.where(sel, 0.0, NEG_INF), own)
        q_aug.append(jnp.where(in_head[hh], qf[hh], aug).astype(bf16))
        q_own.append(jnp.where(in_head[hh], qf[hh], own).astype(bf16))

    start = pl.multiple_of(qi * blk, blk)
    s0 = [_nt(q_own[hh], kaug_s[hh, pl.ds(start, blk), :]) for hh in heads]
    s0 = [jnp.where(causal, s, NEG_INF) for s in s0]
    m0 = [jnp.max(s, axis=-1, keepdims=True) for s in s0]
    p0 = [jnp.exp2(s - m).astype(bf16) for s, m in zip(s0, m0)]
    a0 = [_mm(p0[hh], vaug_s[hh, pl.ds(start, blk), :]) for hh in heads]

    def pair_body(t, carry):
        ms, accs = carry
        rows = pl.ds(pl.multiple_of(t * (2 * blk), 2 * blk), 2 * blk)
        s = [_nt(q_aug[hh], kaug_s[hh, rows, :]) for hh in heads]
        m_new = [jnp.maximum(ms[hh], jnp.max(s[hh], axis=-1, keepdims=True)) for hh in heads]
        p = [jnp.exp2(s[hh] - m_new[hh]).astype(bf16) for hh in heads]
        pv = [_mm(p[hh], vaug_s[hh, rows, :]) for hh in heads]
        accs = [accs[hh] * jnp.exp2(ms[hh] - m_new[hh]) + pv[hh] for hh in heads]
        return (tuple(m_new), tuple(accs))

    _, accs = lax.fori_loop(0, (qi + 1) // 2, pair_body, (tuple(m0), tuple(a0)))
    out = [accs[hh] / accs[hh][:, aug0[hh]:aug0[hh] + 1] for hh in heads]
    for tt in range(MOBA_STEP_HEADS // 2):
        o_ref[0, :, tt * LANES:(tt + 1) * LANES] = jnp.where(lane < half, out[2 * tt], out[2 * tt + 1])


def _moba_template(seq):
    pos = np.arange(seq, dtype=np.float64)
    tmpl = np.zeros((MOBA_HEADS, seq, LANES), np.float32)
    for h in range(MOBA_HEADS):
        a0 = HEAD_DIM if h % 2 == 0 else 0
        slope = 2.0 ** (-8.0 * (h + 1) / MOBA_HEADS)
        rest = slope * LOG2E * pos
        for c in range(3):
            part = rest.astype(np.float32).astype(bf16).astype(np.float64)
            tmpl[h, :, a0 + MOBA_AUG_POS + c] = part
            rest = rest - part
        tmpl[h, np.arange(seq), a0 + MOBA_AUG_BLK + np.arange(seq) // MOBA_BLOCK] = 1.0
    return jnp.asarray(tmpl.reshape(MOBA_HEADS // MOBA_STEP_HEADS, MOBA_STEP_HEADS, seq, LANES), bf16)


def _moba(qkv3):
    bsz, seq, _ = qkv3.shape
    blk = MOBA_BLOCK
    sh = MOBA_STEP_HEADS
    n_grp = MOBA_HEADS // sh
    wid = sh * HEAD_DIM
    assert seq % (2 * blk) == 0 and seq // blk <= HEAD_DIM - MOBA_AUG_BLK
    return pl.pallas_call(
        _moba_kernel,
        grid=(bsz, n_grp, seq // blk),
        in_specs=[pl.BlockSpec((1, blk, wid), lambda b, p, i: (b, i, p)),
                  pl.BlockSpec((1, seq, wid), lambda b, p, i: (b, 0, n_grp + p)),
                  pl.BlockSpec((1, seq, wid), lambda b, p, i: (b, 0, 2 * n_grp + p)),
                  pl.BlockSpec((1, sh, seq, LANES), lambda b, p, i: (p, 0, 0, 0))],
        out_specs=pl.BlockSpec((1, blk, wid), lambda b, p, i: (b, i, p)),
        out_shape=jax.ShapeDtypeStruct((bsz, seq, D_MOBA), f32),
        scratch_shapes=[pltpu.VMEM((sh, seq, LANES), bf16), pltpu.VMEM((sh, seq, LANES), bf16),
                        pltpu.VMEM((sh // 2, LANES, LANES), f32)],
        compiler_params=pltpu.CompilerParams(
            dimension_semantics=("parallel", "parallel", "arbitrary"), vmem_limit_bytes=VMEM_LIMIT),
    )(qkv3, qkv3, qkv3, _moba_template(seq))


S5_SUPER = 4
S5_SEGS = 8
S5_COLS = 512


def _s5_kernel(u_ref, perm_ref, bd_ref, cre_ref, cim_ref, lam_ref, d_ref, gw_ref, gb_ref, o_ref,
               hr_s, hi_s, carry_s):
    tb = u_ref.shape[1]
    seg = tb // S5_SEGS
    n_state = lam_ref.shape[1]
    sgw = n_state // S5_SUPER
    uw = D_SSM // S5_SUPER

    @pl.when(pl.program_id(1) == 0)
    def _():
        carry_s[...] = jnp.zeros_like(carry_s)

    u = u_ref[0]
    perm = perm_ref[...]
    ub = _mm(perm, u.astype(bf16)).astype(bf16)
    for sg in range(S5_SUPER):
        bu = _mm(ub[:, sg * uw:(sg + 1) * uw], bd_ref[sg])
        for j in range(sgw // LANES):
            hr_s[sg * (sgw // LANES) + j] = bu[:, j * LANES:(j + 1) * LANES]
            hi_s[sg * (sgw // LANES) + j] = bu[:, sgw + j * LANES:sgw + (j + 1) * LANES]

    tpc = S5_COLS // LANES

    def load(ref, cc, idx):
        return jnp.concatenate([ref[cc * tpc + j, idx, :] for j in range(tpc)], axis=1)

    def store(ref, cc, idx, val):
        for j in range(tpc):
            ref[cc * tpc + j, idx, :] = val[:, j * LANES:(j + 1) * LANES]

    for cc in range(n_state // S5_COLS):
        cols = slice(cc * S5_COLS, (cc + 1) * S5_COLS)
        lr = lam_ref[0:1, cols]
        li = lam_ref[1:2, cols]

        def local(r, carry, cc=cc, lr=lr, li=li):
            hr, hi = carry
            idx = pl.ds(pl.multiple_of(r * S5_SEGS, S5_SEGS), S5_SEGS)
            nr = lr * hr - li * hi + load(hr_s, cc, idx)
            ni = lr * hi + li * hr + load(hi_s, cc, idx)
            store(hr_s, cc, idx, nr)
            store(hi_s, cc, idx, ni)
            return nr, ni

        zero = jnp.zeros((S5_SEGS, S5_COLS), f32)
        er, ei = lax.fori_loop(0, seg, local, (zero, zero), unroll=4)
        pr, pi = lr, li
        for _ in range(seg.bit_length() - 1):
            pr, pi = pr * pr - pi * pi, 2.0 * pr * pi
        cr = carry_s[0:1, cols]
        ci = carry_s[1:2, cols]
        in_r, in_i = [], []
        for s in range(S5_SEGS):
            in_r.append(cr)
            in_i.append(ci)
            cr, ci = pr * cr - pi * ci + er[s:s + 1, :], pr * ci + pi * cr + ei[s:s + 1, :]
        carry_s[0:1, cols] = cr
        carry_s[1:2, cols] = ci

        def fix(r, carry, cc=cc, lr=lr, li=li):
            qr, qi = carry
            qr, qi = lr * qr - li * qi, lr * qi + li * qr
            idx = pl.ds(pl.multiple_of(r * S5_SEGS, S5_SEGS), S5_SEGS)
            store(hr_s, cc, idx, load(hr_s, cc, idx) + qr)
            store(hi_s, cc, idx, load(hi_s, cc, idx) + qi)
            return qr, qi

        lax.fori_loop(0, seg, fix, (jnp.concatenate(in_r, axis=0), jnp.concatenate(in_i, axis=0)),
                      unroll=4)

    slab = lambda ref, sg: jnp.concatenate(
        [ref[sg * (sgw // LANES) + j] for j in range(sgw // LANES)], axis=1).astype(bf16)
    ys = [_mm(slab(hr_s, sg), cre_ref[sg]) + _mm(slab(hi_s, sg), cim_ref[sg]) for sg in range(S5_SUPER)]
    yh, yl = _split2(jnp.concatenate(ys, axis=1))
    y = _tn(perm, yh) + _tn(perm, yl) + d_ref[...] * u
    y = 0.5 * y * (1.0 + jnp.tanh(0.7978845608028654 * (y + 0.044715 * (y * y * y))))
    zg = _mm(y.astype(bf16), gw_ref[...]) + gb_ref[...]
    o_ref[0] = zg[:, :D_SSM] * jax.nn.sigmoid(zg[:, D_SSM:])


def _s5_params(p):
    a_re = p['a_re'].astype(f32)
    a_im = p['a_im'].astype(f32)
    dt = jnp.exp(p['log_dt'].astype(f32))[:, None]
    mag = jnp.exp(a_re * dt)
    lam_re = mag * jnp.cos(a_im * dt)
    lam_im = mag * jnp.sin(a_im * dt)
    den = a_re * a_re + a_im * a_im
    nr = lam_re - 1.0
    coef_re = (nr * a_re + lam_im * a_im) / den
    coef_im = (lam_im * a_re - nr * a_im) / den
    b_re = p['b_re'].astype(f32)
    b_im = p['b_im'].astype(f32)
    bb_re = coef_re[..., None] * b_re - coef_im[..., None] * b_im
    bb_im = coef_re[..., None] * b_im + coef_im[..., None] * b_re
    gps = SSM_GROUPS // S5_SUPER
    eye = jnp.eye(gps, dtype=f32)
    n_state = SSM_GROUPS * SSM_STATE
    sgw = n_state // S5_SUPER
    uw = D_SSM // S5_SUPER

    def in_map(bb):
        bb = bb.reshape(S5_SUPER, gps, SSM_STATE, SSM_GROUP)
        return jnp.einsum('sgph,gk->sghkp', bb, eye).reshape(S5_SUPER, uw, sgw)

    def out_map(c):
        c = c.reshape(S5_SUPER, gps, SSM_GROUP, SSM_STATE)
        return jnp.einsum('sghp,gk->sgpkh', c, eye).reshape(S5_SUPER, sgw, uw)

    bd = jnp.concatenate([in_map(bb_re), in_map(bb_im)], axis=2)
    cre = out_map(p['c_re'].astype(f32))
    cim = -out_map(p['c_im'].astype(f32))
    lam = jnp.stack([lam_re.reshape(n_state), lam_im.reshape(n_state)], axis=0)
    return bd.astype(bf16), cre.astype(bf16), cim.astype(bf16), lam


def _s5(u3, p, tb=512):
    bsz, seq, _ = u3.shape
    tb = min(tb, seq)
    seg = tb // S5_SEGS
    assert tb % S5_SEGS == 0 and seg & (seg - 1) == 0
    n_state = SSM_GROUPS * SSM_STATE
    bd, cre, cim, lam = _s5_params(p)
    src = (np.arange(tb) % S5_SEGS) * seg + np.arange(tb) // S5_SEGS
    perm = jnp.asarray(np.arange(tb)[None, :] == src[:, None], bf16)
    args = [u3, perm, bd, cre, cim, lam, p['d'].reshape(1, -1).astype(f32), p['glu_w'].astype(bf16),
            p['glu_b'].reshape(1, -1).astype(f32)]
    full = lambda a: pl.BlockSpec(a.shape, lambda b, l: (0,) * a.ndim)
    return pl.pallas_call(
        _s5_kernel,
        grid=(bsz, seq // tb),
        in_specs=[pl.BlockSpec((1, tb, D_SSM), lambda b, l: (b, l, 0))] + [full(a) for a in args[1:]],
        out_specs=pl.BlockSpec((1, tb, D_SSM), lambda b, l: (b, l, 0)),
        out_shape=jax.ShapeDtypeStruct((bsz, seq, D_SSM), f32),
        scratch_shapes=[pltpu.VMEM((n_state // LANES, tb, LANES), f32),
                        pltpu.VMEM((n_state // LANES, tb, LANES), f32),
                        pltpu.VMEM((2, n_state), f32)],
        compiler_params=pltpu.CompilerParams(
            dimension_semantics=("parallel", "arbitrary"), vmem_limit_bytes=VMEM_LIMIT),
    )(*args)


def _merge_kernel(x_ref, ya_ref, yb_ref, yc_ref, wg_ref, gb_ref, wa_ref, wb_ref, wc_ref, wo_ref,
                  lnw_ref, lnb_ref, o_ref):
    x = x_ref[...]
    xb = x.astype(bf16)
    merged = None
    for br, (y_ref, w_ref) in enumerate(((ya_ref, wa_ref), (yb_ref, wb_ref), (yc_ref, wc_ref))):
        cols = slice(br * D_MODEL, (br + 1) * D_MODEL)
        gate = jax.nn.sigmoid(_mm(xb, wg_ref[:, cols]) + gb_ref[:, cols])
        term = gate * _mm(y_ref[...].astype(bf16), w_ref[...])
        merged = term if merged is None else merged + term
    h = _mm(merged.astype(bf16), wo_ref[...])
    o_ref[...] = _layer_norm(DEEPNORM_ALPHA * x + h, lnw_ref[...], lnb_ref[...])


def _merge(x2, ya, yb, yc, wg, gb, wa, wb, wc, wo, lnw, lnb, tm=256):
    m = x2.shape[0]
    rowblk = lambda n: pl.BlockSpec((tm, n), lambda i: (i, 0))
    full = lambda a: pl.BlockSpec(a.shape, lambda i: (0,) * a.ndim)
    consts = [wg, gb, wa, wb, wc, wo, lnw, lnb]
    return pl.pallas_call(
        _merge_kernel,
        grid=(m // tm,),
        in_specs=[rowblk(D_MODEL), rowblk(D_RWKV), rowblk(D_MOBA), rowblk(D_SSM)]
        + [full(a) for a in consts],
        out_specs=rowblk(D_MODEL),
        out_shape=jax.ShapeDtypeStruct((m, D_MODEL), f32),
        compiler_params=pltpu.CompilerParams(
            dimension_semantics=("parallel",), vmem_limit_bytes=VMEM_LIMIT),
    )(x2, ya, yb, yc, *consts)


def _moe_kernel(x_ref, rwh_ref, rwl_ref, rb_ref, wg_ref, wu_ref, wd_ref, lnw_ref, lnb_ref, o_ref,
                xb_s, wt_s, acc_s):
    e = pl.program_id(1)
    tm = x_ref.shape[0]
    lane = lax.broadcasted_iota(jnp.int32, (tm, LANES), 1)
    lane_f = lane.astype(f32)

    @pl.when(e == 0)
    def _():
        x = x_ref[...]
        xh, xl = _split2(x)
        xb_s[...] = xh
        rwh = rwh_ref[...]
        logits = _mm(xh, rwh) + _mm(xh, rwl_ref[...]) + _mm(xl, rwh) + rb_ref[...]
        is_grp = (lane >= N_EXPERTS) & (lane < N_EXPERTS + N_EXPERT_GROUPS)
        gl = jnp.where(is_grp, logits, -jnp.inf)
        gmax = jnp.max(gl, axis=-1, keepdims=True)
        gidx = jnp.min(jnp.where(gl == gmax, lane_f, 1e9), axis=-1, keepdims=True) - float(N_EXPERTS)
        p_group = 1.0 / jnp.sum(jnp.where(is_grp, jnp.exp(gl - gmax), 0.0), axis=-1, keepdims=True)
        grp_of_lane = jnp.floor(lane_f * (1.0 / EXPERTS_PER_GROUP))
        in_grp = (lane < N_EXPERTS) & (grp_of_lane == gidx)
        el = jnp.where(in_grp, logits, -jnp.inf)
        m1 = jnp.max(el, axis=-1, keepdims=True)
        i1 = jnp.min(jnp.where(el == m1, lane_f, 1e9), axis=-1, keepdims=True)
        el2 = jnp.where(lane_f == i1, -jnp.inf, el)
        m2 = jnp.max(el2, axis=-1, keepdims=True)
        i2 = jnp.min(jnp.where(el2 == m2, lane_f, 1e9), axis=-1, keepdims=True)
        e2 = jnp.exp(m2 - m1)
        w1 = p_group / (1.0 + e2)
        w2 = p_group * e2 / (1.0 + e2)
        wt_s[...] = jnp.where(lane_f == i1, w1, 0.0) + jnp.where(lane_f == i2, w2, 0.0)
        acc_s[...] = jnp.zeros_like(acc_s)

    xb = xb_s[...]
    gact = _mm(xb, wg_ref[0])
    up = _mm(xb, wu_ref[0])
    wcol = jnp.sum(jnp.where(lane == e, wt_s[...], 0.0), axis=-1, keepdims=True)
    hid = gact * jax.nn.sigmoid(gact) * up * wcol
    acc_s[...] += _mm(hid.astype(bf16), wd_ref[0])

    @pl.when(e == N_EXPERTS - 1)
    def _():
        o_ref[...] = _layer_norm(DEEPNORM_ALPHA * x_ref[...] + acc_s[...], lnw_ref[...], lnb_ref[...])


def _moe(x2, rw, rb, wg, wu, wd, lnw, lnb, tm=1024):
    m = x2.shape[0]
    tm = min(tm, m)
    rwh, rwl = _split2(rw)
    full = lambda a: pl.BlockSpec(a.shape, lambda i, e: (0,) * a.ndim)
    return pl.pallas_call(
        _moe_kernel,
        grid=(m // tm, N_EXPERTS),
        in_specs=[pl.BlockSpec((tm, D_MODEL), lambda i, e: (i, 0)),
                  full(rwh), full(rwl), full(rb),
                  pl.BlockSpec((1, D_MODEL, D_EXPERT), lambda i, e: (e, 0, 0)),
                  pl.BlockSpec((1, D_MODEL, D_EXPERT), lambda i, e: (e, 0, 0)),
                  pl.BlockSpec((1, D_EXPERT, D_MODEL), lambda i, e: (e, 0, 0)),
                  full(lnw), full(lnb)],
        out_specs=pl.BlockSpec((tm, D_MODEL), lambda i, e: (i, 0)),
        out_shape=jax.ShapeDtypeStruct((m, D_MODEL), f32),
        scratch_shapes=[pltpu.VMEM((tm, D_MODEL), bf16), pltpu.VMEM((tm, LANES), f32),
                        pltpu.VMEM((tm, D_MODEL), f32)],
        compiler_params=pltpu.CompilerParams(
            dimension_semantics=("parallel", "arbitrary"), vmem_limit_bytes=VMEM_LIMIT),
    )(x2, rwh, rwl, rb, wg, wu, wd, lnw, lnb)


def _router_weights(router_group_w, router_group_b, router_expert_w, router_expert_b):
    pad = LANES - N_EXPERTS - N_EXPERT_GROUPS
    rw = jnp.concatenate([router_expert_w.astype(f32), router_group_w.astype(f32),
                          jnp.zeros((D_MODEL, pad), f32)], axis=1)
    rb = jnp.concatenate([router_expert_b.astype(f32), router_group_b.astype(f32),
                          jnp.zeros((pad,), f32)]).reshape(1, LANES)
    return rw, rb


def kernel(x, w_in, rwkv_mu, rwkv_w0, rwkv_w2, rwkv_a0, rwkv_a2, rwkv_g2, rwkv_k_k, rwkv_k_a, rwkv_r_k, rwkv_ln_w, rwkv_ln_b, ssm_a_re, ssm_a_im, ssm_b_re, ssm_b_im, ssm_c_re, ssm_c_im, ssm_d, ssm_log_dt, ssm_glu_w, ssm_glu_b, w_up_rwkv, w_up_moba, w_up_ssm, gate_b, w_out, ln1_w, ln1_b, router_group_w, router_group_b, router_expert_w, router_expert_b, expert_w_gate, expert_w_up, expert_w_down, ln2_w, ln2_b):
    bsz, seq, _ = x.shape
    x2 = x.reshape(bsz * seq, D_MODEL).astype(f32)
    row = lambda a: a.reshape(1, -1).astype(f32)
    for l in range(DEPTH):
        w = w_in[l].astype(bf16)
        z = _project(x2, w[:, :OFF_MOBA])
        qkv = _project(x2, w[:, OFF_MOBA:OFF_SSM])
        u = _project(x2, w[:, OFF_SSM:OFF_GATE])
        y_a = _rwkv(z.reshape(bsz, seq, RWKV_COLS),
                    dict(mu=rwkv_mu[l], w0=rwkv_w0[l], w2=rwkv_w2[l], a0=rwkv_a0[l], a2=rwkv_a2[l],
                         g2=rwkv_g2[l], k_k=rwkv_k_k[l], k_a=rwkv_k_a[l], r_k=rwkv_r_k[l],
                         ln_w=rwkv_ln_w[l], ln_b=rwkv_ln_b[l]))
        y_b = _moba(qkv.reshape(bsz, seq, MOBA_COLS))
        y_c = _s5(u.reshape(bsz, seq, D_SSM),
                  dict(a_re=ssm_a_re[l], a_im=ssm_a_im[l], b_re=ssm_b_re[l], b_im=ssm_b_im[l],
                       c_re=ssm_c_re[l], c_im=ssm_c_im[l], d=ssm_d[l], log_dt=ssm_log_dt[l],
                       glu_w=ssm_glu_w[l], glu_b=ssm_glu_b[l]))
        x2 = _merge(x2, y_a.reshape(-1, D_RWKV), y_b.reshape(-1, D_MOBA), y_c.reshape(-1, D_SSM),
                    w[:, OFF_GATE:], row(gate_b[l]),
                    w_up_rwkv[l].astype(bf16), w_up_moba[l].astype(bf16), w_up_ssm[l].astype(bf16),
                    w_out[l].astype(bf16), row(ln1_w[l]), row(ln1_b[l]))
        rw, rb = _router_weights(router_group_w[l], router_group_b[l], router_expert_w[l],
                                 router_expert_b[l])
        x2 = _moe(x2, rw, rb, expert_w_gate[l].astype(bf16), expert_w_up[l].astype(bf16),
                  expert_w_down[l].astype(bf16), row(ln2_w[l]), row(ln2_b[l]))
    return x2.reshape(bsz, seq, D_MODEL)
```

```python
import functools

import jax
import jax.numpy as jnp
import numpy as np
from jax import lax
from jax.experimental import pallas as pl
from jax.experimental.pallas import tpu as pltpu

f32 = jnp.float32
bf16 = jnp.bfloat16

D_MODEL = 1024
DEPTH = 4
HEAD_DIM = 64
D_RWKV = 512
RWKV_HEADS = D_RWKV // HEAD_DIM
DECAY_LORA = 64
ICL_LORA = 64
GATE_LORA = 128
RWKV_GN_EPS = 64e-5
D_MOBA = 512
MOBA_HEADS = D_MOBA // HEAD_DIM
MOBA_BLOCK = 256
MOBA_TOPK = 3
D_SSM = 512
SSM_GROUP = 16
SSM_GROUPS = D_SSM // SSM_GROUP
SSM_STATE = 64
N_BRANCHES = 3
N_EXPERT_GROUPS = 4
EXPERTS_PER_GROUP = 8
N_EXPERTS = N_EXPERT_GROUPS * EXPERTS_PER_GROUP
D_EXPERT = D_MODEL // 4
LN_EPS = 1e-5
DEEPNORM_ALPHA = (2 * DEPTH) ** 0.25
NEG_INF = -1e30
RWKV_COLS = 3 * D_RWKV + DECAY_LORA + ICL_LORA + GATE_LORA
MOBA_COLS = 3 * D_MOBA
OFF_MOBA = RWKV_COLS
OFF_SSM = OFF_MOBA + MOBA_COLS
OFF_GATE = OFF_SSM + D_SSM

LANES = 128
WKV_CHUNK = 64
VMEM_LIMIT = 48 * 1024 * 1024


def _nt(a, b):
    return lax.dot_general(a, b, (((1,), (1,)), ((), ())), preferred_element_type=f32)


def _tn(a, b):
    return lax.dot_general(a, b, (((0,), (0,)), ((), ())), preferred_element_type=f32)


def _mm(a, b):
    return jnp.dot(a, b, preferred_element_type=f32)


def _split2(x):
    hi = x.astype(bf16)
    lo = (x - hi.astype(f32)).astype(bf16)
    return hi, lo


def _layer_norm(y, w, b):
    mu = jnp.mean(y, axis=-1, keepdims=True)
    d = y - mu
    var = jnp.mean(d * d, axis=-1, keepdims=True)
    return d * lax.rsqrt(var + LN_EPS) * w + b


def _proj_kernel(x_ref, w_ref, o_ref):
    o_ref[...] = _mm(x_ref[...].astype(bf16), w_ref[...])


def _project(x2, w, tm=512):
    m, k = x2.shape
    n = w.shape[1]
    return pl.pallas_call(
        _proj_kernel,
        grid=(m // tm,),
        in_specs=[pl.BlockSpec((tm, k), lambda i: (i, 0)), pl.BlockSpec((k, n), lambda i: (0, 0))],
        out_specs=pl.BlockSpec((tm, n), lambda i: (i, 0)),
        out_shape=jax.ShapeDtypeStruct((m, n), f32),
        compiler_params=pltpu.CompilerParams(
            dimension_semantics=("parallel",), vmem_limit_bytes=VMEM_LIMIT),
    )(x2, w)


def _rwkv_kernel(z_ref, mu_ref, w0_ref, w2_ref, a0_ref, a2_ref, g2_ref, kk_ref, ka_ref, rk_ref,
                 lnw_ref, lnb_ref, seg_ref, tri_ref, o_ref,
                 zlast_s, state_s, r_s, k_s, v_s, a_s, b_s, lw_s, g_s, bon_s, y_s):
    tb = z_ref.shape[1]
    n_chunks = tb // WKV_CHUNK
    c64 = WKV_CHUNK

    @pl.when(pl.program_id(1) == 0)
    def _():
        zlast_s[...] = jnp.zeros_like(zlast_s)
        state_s[...] = jnp.zeros_like(state_s)

    seg = seg_ref[...]

    def segsum(x):
        hi, lo = _split2(x)
        return _mm(hi, seg) + _mm(lo, seg)

    z = z_ref[0]
    row = lax.broadcasted_iota(jnp.int32, z.shape, 0)
    z_prev = jnp.where(row == 0, zlast_s[...], pltpu.roll(z, 1, axis=0))
    zlast_s[...] = z[tb - 1:tb, :]
    zz = z + (z_prev - z) * mu_ref[...]
    r = zz[:, 0:D_RWKV]
    k = zz[:, D_RWKV:2 * D_RWKV]
    v = zz[:, 2 * D_RWKV:3 * D_RWKV]
    o1 = 3 * D_RWKV
    xw = zz[:, o1:o1 + DECAY_LORA]
    xa = zz[:, o1 + DECAY_LORA:o1 + DECAY_LORA + ICL_LORA]
    xg = zz[:, o1 + DECAY_LORA + ICL_LORA:]
    wl = w0_ref[...] + _mm(jnp.tanh(xw).astype(bf16), w2_ref[...])
    softplus = jnp.maximum(-wl, 0.0) + jnp.log(1.0 + jnp.exp(-jnp.abs(wl)))
    lw_s[...] = -jnp.exp(-softplus - 0.5)
    a_icl = jax.nn.sigmoid(a0_ref[...] + _mm(xa.astype(bf16), a2_ref[...]))
    g_s[...] = _mm(jax.nn.sigmoid(xg).astype(bf16), g2_ref[...])
    kk = k * kk_ref[...]
    kk = kk * lax.rsqrt(jnp.maximum(segsum(kk * kk), 1e-24))
    k2 = k * (1.0 + (a_icl - 1.0) * ka_ref[...])
    r_s[...] = r
    k_s[...] = k2
    v_s[...] = v
    a_s[...] = -kk
    b_s[...] = kk * a_icl
    bon_s[...] = segsum(r * k2 * rk_ref[...])

    ri = lax.broadcasted_iota(jnp.int32, (c64, c64), 0)
    ci = lax.broadcasted_iota(jnp.int32, (c64, c64), 1)
    strict = ri > ci
    incl = ri >= ci
    eye = jnp.where(ri == ci, 1.0, 0.0).astype(f32)
    tri = tri_ref[...]

    def chunk_body(c, carry):
        rows = pl.ds(pl.multiple_of(c * c64, c64), c64)
        lwc = lw_s[rows, :]
        lhi, llo = _split2(lwc)
        cl = _mm(tri, lhi) + _mm(tri, llo)
        cl_last = cl[c64 - 1:c64, :]
        e_cl = jnp.exp(cl)
        e_prev = jnp.exp(cl - lwc)
        e_neg = jnp.exp(-cl)
        e_tot = jnp.exp(cl_last - cl)
        p_c = jnp.exp(cl_last)
        rc = r_s[rows, :]
        kc = k_s[rows, :]
        vc = v_s[rows, :]
        ac = a_s[rows, :]
        bc = b_s[rows, :]
        a_t = ac * e_prev
        r_t = rc * e_cl
        b_t = bc * e_neg
        k_t = kc * e_neg
        b_h = bc * e_tot
        k_h = kc * e_tot
        heads = range(RWKV_HEADS)
        sls = [slice(h * HEAD_DIM, (h + 1) * HEAD_DIM) for h in heads]
        hcat = lambda p, q, axis: [jnp.concatenate([p[:, s], q[:, s]], axis=axis).astype(bf16) for s in sls]
        ar = hcat(a_t, r_t, 0)
        bk = hcat(b_t, k_t, 0)
        gm = [_nt(ar[h], bk[h]) for h in heads]
        a_ab = [jnp.where(strict, g[:c64, :c64], 0.0) for g in gm]
        a_ak = [jnp.where(strict, g[:c64, c64:], 0.0).astype(bf16) for g in gm]
        a_rb = [jnp.where(incl, g[c64:, :c64], 0.0).astype(bf16) for g in gm]
        a_rk = [jnp.where(incl, g[c64:, c64:], 0.0).astype(bf16) for g in gm]
        vh = [vc[:, s] for s in sls]
        vb = [x.astype(bf16) for x in vh]
        akv = [_mm(a_ak[h], vb[h]) for h in heads]
        yloc = [_mm(a_rk[h], vb[h]) for h in heads]
        xs = a_ab
        t_inv = [eye + x for x in xs]
        for _ in range(5):
            xb = [x.astype(bf16) for x in xs]
            xs = [_mm(x, x) for x in xb]
            t_inv = [t + _mm(t.astype(bf16), x.astype(bf16)) for t, x in zip(t_inv, xs)]
        rhs = [jnp.concatenate([a_t[:, sls[h]], akv[h]], axis=1).astype(bf16) for h in heads]
        tw = [_mm(t_inv[h].astype(bf16), rhs[h]) for h in heads]
        s_old = [state_s[h] for h in heads]
        wr = [jnp.concatenate([tw[h][:, :c64], r_t[:, sls[h]]], axis=0).astype(bf16) for h in heads]
        wrs = [_nt(wr[h], s_old[h].astype(bf16)) for h in heads]
        u = [wrs[h][:c64] + tw[h][:, c64:] for h in heads]
        ys = [wrs[h][c64:] + _mm(a_rb[h], u[h].astype(bf16)) + yloc[h] for h in heads]
        uv = [jnp.concatenate([u[h], vh[h]], axis=0).astype(bf16) for h in heads]
        bkh = hcat(b_h, k_h, 0)
        for h in heads:
            state_s[h] = s_old[h] * p_c[:, sls[h]] + _tn(uv[h], bkh[h])
        y_s[rows, :] = jnp.concatenate(ys, axis=1)
        return carry

    lax.fori_loop(0, n_chunks, chunk_body, 0)

    y = y_s[...]
    mean = segsum(y) * (1.0 / HEAD_DIM)
    d = y - mean
    var = segsum(d * d) * (1.0 / HEAD_DIM)
    yn = d * lax.rsqrt(var + RWKV_GN_EPS) * lnw_ref[...] + lnb_ref[...]
    o_ref[0] = (yn + bon_s[...] * v_s[...]) * g_s[...]


def _rwkv(z3, p, tb=512):
    bsz, seq, _ = z3.shape
    tb = min(tb, seq)
    seg = (np.arange(D_RWKV)[:, None] // HEAD_DIM == np.arange(D_RWKV)[None, :] // HEAD_DIM)
    seg = jnp.asarray(seg, bf16)
    tri = jnp.asarray(np.tril(np.ones((WKV_CHUNK, WKV_CHUNK))), bf16)
    row = lambda a: a.reshape(1, -1).astype(f32)
    args = [z3, row(p['mu']), row(p['w0']), p['w2'].astype(bf16), row(p['a0']), p['a2'].astype(bf16),
            p['g2'].astype(bf16), row(p['k_k']), row(p['k_a']), row(p['r_k']), row(p['ln_w']),
            row(p['ln_b']), seg, tri]
    full = lambda a: pl.BlockSpec(a.shape, lambda b, l: (0,) * a.ndim)
    in_specs = [pl.BlockSpec((1, tb, RWKV_COLS), lambda b, l: (b, l, 0))] + [full(a) for a in args[1:]]
    big = lambda: pltpu.VMEM((tb, D_RWKV), f32)
    return pl.pallas_call(
        _rwkv_kernel,
        grid=(bsz, seq // tb),
        in_specs=in_specs,
        out_specs=pl.BlockSpec((1, tb, D_RWKV), lambda b, l: (b, l, 0)),
        out_shape=jax.ShapeDtypeStruct((bsz, seq, D_RWKV), f32),
        scratch_shapes=[pltpu.VMEM((1, RWKV_COLS), f32),
                        pltpu.VMEM((RWKV_HEADS, HEAD_DIM, HEAD_DIM), f32),
                        big(), big(), big(), big(), big(), big(), big(), big(), big()],
        compiler_params=pltpu.CompilerParams(
            dimension_semantics=("parallel", "arbitrary"), vmem_limit_bytes=VMEM_LIMIT),
    )(*args)


MOBA_AUG_POS = 0
MOBA_AUG_BLK = 6
LOG2E = 1.4426950408889634


MOBA_STEP_HEADS = 4


def _moba_kernel(q_ref, k_ref, v_ref, tmpl_ref, o_ref, kaug_s, vaug_s, kmean_s):
    blk = MOBA_BLOCK
    half = HEAD_DIM
    seq = k_ref.shape[1]
    n_blk = seq // blk
    qi = pl.program_id(2)
    heads = range(MOBA_STEP_HEADS)
    tile = [slice((hh // 2) * LANES, (hh // 2 + 1) * LANES) for hh in heads]
    aug0 = [half if hh % 2 == 0 else 0 for hh in heads]

    @pl.when(qi == 0)
    def _():
        lane_l = lax.broadcasted_iota(jnp.int32, (seq, LANES), 1)
        for hh in heads:
            kb = k_ref[0, :, tile[hh]].astype(bf16)
            vb = v_ref[0, :, tile[hh]].astype(bf16)
            in_head = (lane_l < half) if hh % 2 == 0 else (lane_l >= half)
            kaug_s[hh] = jnp.where(in_head, kb, tmpl_ref[0, hh])
            ones_lane = jnp.where(lane_l == aug0[hh], 1.0, 0.0).astype(bf16)
            vaug_s[hh] = jnp.where(in_head, vb, ones_lane)
        kmean_s[...] = jnp.zeros_like(kmean_s)
        for n in range(n_blk):
            km = jnp.mean(k_ref[0, n * blk:(n + 1) * blk, :], axis=0, keepdims=True)
            for tt in range(MOBA_STEP_HEADS // 2):
                kt = km[:, tt * LANES:(tt + 1) * LANES]
                kmean_s[tt, MOBA_AUG_BLK + n:MOBA_AUG_BLK + n + 1, :] = kt
                kmean_s[tt, half + MOBA_AUG_BLK + n:half + MOBA_AUG_BLK + n + 1, :] = kt

    ri = lax.broadcasted_iota(jnp.int32, (blk, blk), 0)
    ci = lax.broadcasted_iota(jnp.int32, (blk, blk), 1)
    causal = ri >= ci
    lane = lax.broadcasted_iota(jnp.int32, (blk, LANES), 1)
    lane_f = lane.astype(f32)
    qf = [q_ref[0, :, tile[hh]] * (HEAD_DIM ** -0.5 * LOG2E) for hh in heads]
    in_head = [(lane < half) if hh % 2 == 0 else (lane >= half) for hh in heads]
    blk_lane = [lane - (aug0[hh] + MOBA_AUG_BLK) for hh in heads]
    cand = [(blk_lane[hh] >= 0) & (blk_lane[hh] < qi) for hh in heads]

    qs = [_split2(jnp.where(in_head[hh], qf[hh], 0.0)) for hh in heads]
    kms = [_split2(kmean_s[tt]) for tt in range(MOBA_STEP_HEADS // 2)]
    gate = [_nt(qs[hh][0], kms[hh // 2][0]) + _nt(qs[hh][0], kms[hh // 2][1])
            + _nt(qs[hh][1], kms[hh // 2][0]) for hh in heads]
    g = [jnp.where(cand[hh], gate[hh], -jnp.inf) for hh in heads]
    picked = [jnp.zeros((blk, LANES), f32) for _ in heads]
    for _ in range(MOBA_TOPK):
        mx = [jnp.max(x, axis=-1, keepdims=True) for x in g]
        idx = [jnp.min(jnp.where(g[hh] == mx[hh], lane_f, 1e9), axis=-1, keepdims=True) for hh in heads]
        hit = [lane_f == i for i in idx]
        picked = [jnp.where(hit[hh], 1.0, picked[hh]) for hh in heads]
        g = [jnp.where(hit[hh], -jnp.inf, g[hh]) for hh in heads]
    q_aug, q_own = [], []
    for hh in heads:
        sel = (picked[hh] > 0.0) & cand[hh]
        is_pos = (lane >= aug0[hh] + MOBA_AUG_POS) & (lane < aug0[hh] + MOBA_AUG_POS + 3)
        is_blk = (blk_lane[hh] >= 0) & (blk_lane[hh] < n_blk)
        own = jnp.where(is_pos, 1.0, 0.0)
        aug = jnp.where(is_blk, jnp.where(sel, 0.0, NEG_INF), own)
        q_aug.append(jnp.where(in_head[hh], qf[hh], aug).astype(bf16))
        q_own.append(jnp.where(in_head[hh], qf[hh], own).astype(bf16))

    start = pl.multiple_of(qi * blk, blk)
    s0 = [_nt(q_own[hh], kaug_s[hh, pl.ds(start, blk), :]) for hh in heads]
    s0 = [jnp.where(causal, s, NEG_INF) for s in s0]
    m0 = [jnp.max(s, axis=-1, keepdims=True) for s in s0]
    p0 = [jnp.exp2(s - m).astype(bf16) for s, m in zip(s0, m0)]
    a0 = [_mm(p0[hh], vaug_s[hh, pl.ds(start, blk), :]) for hh in heads]

    def pair_body(t, carry):
        ms, accs = carry
        rows = pl.ds(pl.multiple_of(t * (2 * blk), 2 * blk), 2 * blk)
        s = [_nt(q_aug[hh], kaug_s[hh, rows, :]) for hh in heads]
        m_new = [jnp.maximum(ms[hh], jnp.max(s[hh], axis=-1, keepdims=True)) for hh in heads]
        p = [jnp.exp2(s[hh] - m_new[hh]).astype(bf16) for hh in heads]
        pv = [_mm(p[hh], vaug_s[hh, rows, :]) for hh in heads]
        accs = [accs[hh] * jnp.exp2(ms[hh] - m_new[hh]) + pv[hh] for hh in heads]
        return (tuple(m_new), tuple(accs))

    _, accs = lax.fori_loop(0, (qi + 1) // 2, pair_body, (tuple(m0), tuple(a0)))
    out = [accs[hh] / accs[hh][:, aug0[hh]:aug0[hh] + 1] for hh in heads]
    for tt in range(MOBA_STEP_HEADS // 2):
        o_ref[0, :, tt * LANES:(tt + 1) * LANES] = jnp.where(lane < half, out[2 * tt], out[2 * tt + 1])


def _moba_template(seq):
    pos = np.arange(seq, dtype=np.float64)
    tmpl = np.zeros((MOBA_HEADS, seq, LANES), np.float32)
    for h in range(MOBA_HEADS):
        a0 = HEAD_DIM if h % 2 == 0 else 0
        slope = 2.0 ** (-8.0 * (h + 1) / MOBA_HEADS)
        rest = slope * LOG2E * pos
        for c in range(3):
            part = rest.astype(np.float32).astype(bf16).astype(np.float64)
            tmpl[h, :, a0 + MOBA_AUG_POS + c] = part
            rest = rest - part
        tmpl[h, np.arange(seq), a0 + MOBA_AUG_BLK + np.arange(seq) // MOBA_BLOCK] = 1.0
    return jnp.asarray(tmpl.reshape(MOBA_HEADS // MOBA_STEP_HEADS, MOBA_STEP_HEADS, seq, LANES), bf16)


def _moba(qkv3):
    bsz, seq, _ = qkv3.shape
    blk = MOBA_BLOCK
    sh = MOBA_STEP_HEADS
    n_grp = MOBA_HEADS // sh
    wid = sh * HEAD_DIM
    assert seq % (2 * blk) == 0 and seq // blk <= HEAD_DIM - MOBA_AUG_BLK
    return pl.pallas_call(
        _moba_kernel,
        grid=(bsz, n_grp, seq // blk),
        in_specs=[pl.BlockSpec((1, blk, wid), lambda b, p, i: (b, i, p)),
                  pl.BlockSpec((1, seq, wid), lambda b, p, i: (b, 0, n_grp + p)),
                  pl.BlockSpec((1, seq, wid), lambda b, p, i: (b, 0, 2 * n_grp + p)),
                  pl.BlockSpec((1, sh, seq, LANES), lambda b, p, i: (p, 0, 0, 0))],
        out_specs=pl.BlockSpec((1, blk, wid), lambda b, p, i: (b, i, p)),
        out_shape=jax.ShapeDtypeStruct((bsz, seq, D_MOBA), f32),
        scratch_shapes=[pltpu.VMEM((sh, seq, LANES), bf16), pltpu.VMEM((sh, seq, LANES), bf16),
                        pltpu.VMEM((sh // 2, LANES, LANES), f32)],
        compiler_params=pltpu.CompilerParams(
            dimension_semantics=("parallel", "parallel", "arbitrary"), vmem_limit_bytes=VMEM_LIMIT),
    )(qkv3, qkv3, qkv3, _moba_template(seq))


S5_SUPER = 4
S5_SEGS = 8
S5_COLS = 512


def _s5_kernel(u_ref, perm_ref, bd_ref, cre_ref, cim_ref, lam_ref, d_ref, gw_ref, gb_ref, o_ref,
               hr_s, hi_s, carry_s):
    tb = u_ref.shape[1]
    seg = tb // S5_SEGS
    n_state = lam_ref.shape[1]
    sgw = n_state // S5_SUPER
    uw = D_SSM // S5_SUPER

    @pl.when(pl.program_id(1) == 0)
    def _():
        carry_s[...] = jnp.zeros_like(carry_s)

    u = u_ref[0]
    perm = perm_ref[...]
    ub = _mm(perm, u.astype(bf16)).astype(bf16)
    for sg in range(S5_SUPER):
        bu = _mm(ub[:, sg * uw:(sg + 1) * uw], bd_ref[sg])
        for j in range(sgw // LANES):
            hr_s[sg * (sgw // LANES) + j] = bu[:, j * LANES:(j + 1) * LANES]
            hi_s[sg * (sgw // LANES) + j] = bu[:, sgw + j * LANES:sgw + (j + 1) * LANES]

    tpc = S5_COLS // LANES

    def load(ref, cc, idx):
        return jnp.concatenate([ref[cc * tpc + j, idx, :] for j in range(tpc)], axis=1)

    def store(ref, cc, idx, val):
        for j in range(tpc):
            ref[cc * tpc + j, idx, :] = val[:, j * LANES:(j + 1) * LANES]

    for cc in range(n_state // S5_COLS):
        cols = slice(cc * S5_COLS, (cc + 1) * S5_COLS)
        lr = lam_ref[0:1, cols]
        li = lam_ref[1:2, cols]

        def local(r, carry, cc=cc, lr=lr, li=li):
            hr, hi = carry
            idx = pl.ds(pl.multiple_of(r * S5_SEGS, S5_SEGS), S5_SEGS)
            nr = lr * hr - li * hi + load(hr_s, cc, idx)
            ni = lr * hi + li * hr + load(hi_s, cc, idx)
            store(hr_s, cc, idx, nr)
            store(hi_s, cc, idx, ni)
            return nr, ni

        zero = jnp.zeros((S5_SEGS, S5_COLS), f32)
        er, ei = lax.fori_loop(0, seg, local, (zero, zero), unroll=4)
        pr, pi = lr, li
        for _ in range(seg.bit_length() - 1):
            pr, pi = pr * pr - pi * pi, 2.0 * pr * pi
        cr = carry_s[0:1, cols]
        ci = carry_s[1:2, cols]
        in_r, in_i = [], []
        for s in range(S5_SEGS):
            in_r.append(cr)
            in_i.append(ci)
            cr, ci = pr * cr - pi * ci + er[s:s + 1, :], pr * ci + pi * cr + ei[s:s + 1, :]
        carry_s[0:1, cols] = cr
        carry_s[1:2, cols] = ci

        def fix(r, carry, cc=cc, lr=lr, li=li):
            qr, qi = carry
            qr, qi = lr * qr - li * qi, lr * qi + li * qr
            idx = pl.ds(pl.multiple_of(r * S5_SEGS, S5_SEGS), S5_SEGS)
            store(hr_s, cc, idx, load(hr_s, cc, idx) + qr)
            store(hi_s, cc, idx, load(hi_s, cc, idx) + qi)
            return qr, qi

        lax.fori_loop(0, seg, fix, (jnp.concatenate(in_r, axis=0), jnp.concatenate(in_i, axis=0)),
                      unroll=4)

    slab = lambda ref, sg: jnp.concatenate(
        [ref[sg * (sgw // LANES) + j] for j in range(sgw // LANES)], axis=1).astype(bf16)
    ys = [_mm(slab(hr_s, sg), cre_ref[sg]) + _mm(slab(hi_s, sg), cim_ref[sg]) for sg in range(S5_SUPER)]
    yh, yl = _split2(jnp.concatenate(ys, axis=1))
    y = _tn(perm, yh) + _tn(perm, yl) + d_ref[...] * u
    y = 0.5 * y * (1.0 + jnp.tanh(0.7978845608028654 * (y + 0.044715 * (y * y * y))))
    zg = _mm(y.astype(bf16), gw_ref[...]) + gb_ref[...]
    o_ref[0] = zg[:, :D_SSM] * jax.nn.sigmoid(zg[:, D_SSM:])


def _s5_params(p):
    a_re = p['a_re'].astype(f32)
    a_im = p['a_im'].astype(f32)
    dt = jnp.exp(p['log_dt'].astype(f32))[:, None]
    mag = jnp.exp(a_re * dt)
    lam_re = mag * jnp.cos(a_im * dt)
    lam_im = mag * jnp.sin(a_im * dt)
    den = a_re * a_re + a_im * a_im
    nr = lam_re - 1.0
    coef_re = (nr * a_re + lam_im * a_im) / den
    coef_im = (lam_im * a_re - nr * a_im) / den
    b_re = p['b_re'].astype(f32)
    b_im = p['b_im'].astype(f32)
    bb_re = coef_re[..., None] * b_re - coef_im[..., None] * b_im
    bb_im = coef_re[..., None] * b_im + coef_im[..., None] * b_re
    gps = SSM_GROUPS // S5_SUPER
    eye = jnp.eye(gps, dtype=f32)
    n_state = SSM_GROUPS * SSM_STATE
    sgw = n_state // S5_SUPER
    uw = D_SSM // S5_SUPER

    def in_map(bb):
        bb = bb.reshape(S5_SUPER, gps, SSM_STATE, SSM_GROUP)
        return jnp.einsum('sgph,gk->sghkp', bb, eye).reshape(S5_SUPER, uw, sgw)

    def out_map(c):
        c = c.reshape(S5_SUPER, gps, SSM_GROUP, SSM_STATE)
        return jnp.einsum('sghp,gk->sgpkh', c, eye).reshape(S5_SUPER, sgw, uw)

    bd = jnp.concatenate([in_map(bb_re), in_map(bb_im)], axis=2)
    cre = out_map(p['c_re'].astype(f32))
    cim = -out_map(p['c_im'].astype(f32))
    lam = jnp.stack([lam_re.reshape(n_state), lam_im.reshape(n_state)], axis=0)
    return bd.astype(bf16), cre.astype(bf16), cim.astype(bf16), lam


def _s5(u3, p, tb=512):
    bsz, seq, _ = u3.shape
    tb = min(tb, seq)
    seg = tb // S5_SEGS
    assert tb % S5_SEGS == 0 and seg & (seg - 1) == 0
    n_state = SSM_GROUPS * SSM_STATE
    bd, cre, cim, lam = _s5_params(p)
    src = (np.arange(tb) % S5_SEGS) * seg + np.arange(tb) // S5_SEGS
    perm = jnp.asarray(np.arange(tb)[None, :] == src[:, None], bf16)
    args = [u3, perm, bd, cre, cim, lam, p['d'].reshape(1, -1).astype(f32), p['glu_w'].astype(bf16),
            p['glu_b'].reshape(1, -1).astype(f32)]
    full = lambda a: pl.BlockSpec(a.shape, lambda b, l: (0,) * a.ndim)
    return pl.pallas_call(
        _s5_kernel,
        grid=(bsz, seq // tb),
        in_specs=[pl.BlockSpec((1, tb, D_SSM), lambda b, l: (b, l, 0))] + [full(a) for a in args[1:]],
        out_specs=pl.BlockSpec((1, tb, D_SSM), lambda b, l: (b, l, 0)),
        out_shape=jax.ShapeDtypeStruct((bsz, seq, D_SSM), f32),
        scratch_shapes=[pltpu.VMEM((n_state // LANES, tb, LANES), f32),
                        pltpu.VMEM((n_state // LANES, tb, LANES), f32),
                        pltpu.VMEM((2, n_state), f32)],
        compiler_params=pltpu.CompilerParams(
            dimension_semantics=("parallel", "arbitrary"), vmem_limit_bytes=VMEM_LIMIT),
    )(*args)


def _merge_kernel(x_ref, ya_ref, yb_ref, yc_ref, wg_ref, gb_ref, wa_ref, wb_ref, wc_ref, wo_ref,
                  lnw_ref, lnb_ref, o_ref):
    x = x_ref[...]
    xb = x.astype(bf16)
    merged = None
    for br, (y_ref, w_ref) in enumerate(((ya_ref, wa_ref), (yb_ref, wb_ref), (yc_ref, wc_ref))):
        cols = slice(br * D_MODEL, (br + 1) * D_MODEL)
        gate = jax.nn.sigmoid(_mm(xb, wg_ref[:, cols]) + gb_ref[:, cols])
        term = gate * _mm(y_ref[...].astype(bf16), w_ref[...])
        merged = term if merged is None else merged + term
    h = _mm(merged.astype(bf16), wo_ref[...])
    o_ref[...] = _layer_norm(DEEPNORM_ALPHA * x + h, lnw_ref[...], lnb_ref[...])


def _merge(x2, ya, yb, yc, wg, gb, wa, wb, wc, wo, lnw, lnb, tm=256):
    m = x2.shape[0]
    rowblk = lambda n: pl.BlockSpec((tm, n), lambda i: (i, 0))
    full = lambda a: pl.BlockSpec(a.shape, lambda i: (0,) * a.ndim)
    consts = [wg, gb, wa, wb, wc, wo, lnw, lnb]
    return pl.pallas_call(
        _merge_kernel,
        grid=(m // tm,),
        in_specs=[rowblk(D_MODEL), rowblk(D_RWKV), rowblk(D_MOBA), rowblk(D_SSM)]
        + [full(a) for a in consts],
        out_specs=rowblk(D_MODEL),
        out_shape=jax.ShapeDtypeStruct((m, D_MODEL), f32),
        compiler_params=pltpu.CompilerParams(
            dimension_semantics=("parallel",), vmem_limit_bytes=VMEM_LIMIT),
    )(x2, ya, yb, yc, *consts)


MOE_BLOCK = 512
MOE_TILE = 256
META_E, META_RANK, META_W = 0, 2, 4


def _moe_route_kernel(x_ref, rwh_ref, rwl_ref, rb_ref, tri_ref, meta_ref, cnt_ref, run_s):
    tm = x_ref.shape[0]
    lane = lax.broadcasted_iota(jnp.int32, (tm, LANES), 1)
    lane_f = lane.astype(f32)

    @pl.when(pl.program_id(0) == 0)
    def _():
        run_s[...] = jnp.zeros_like(run_s)

    xh, xl = _split2(x_ref[...])
    rwh = rwh_ref[...]
    logits = _mm(xh, rwh) + _mm(xh, rwl_ref[...]) + _mm(xl, rwh) + rb_ref[...]
    is_grp = (lane >= N_EXPERTS) & (lane < N_EXPERTS + N_EXPERT_GROUPS)
    gl = jnp.where(is_grp, logits, -jnp.inf)
    gmax = jnp.max(gl, axis=-1, keepdims=True)
    gidx = jnp.min(jnp.where(gl == gmax, lane_f, 1e9), axis=-1, keepdims=True) - float(N_EXPERTS)
    p_group = 1.0 / jnp.sum(jnp.where(is_grp, jnp.exp(gl - gmax), 0.0), axis=-1, keepdims=True)
    grp_of_lane = jnp.floor(lane_f * (1.0 / EXPERTS_PER_GROUP))
    in_grp = (lane < N_EXPERTS) & (grp_of_lane == gidx)
    el = jnp.where(in_grp, logits, -jnp.inf)
    m1 = jnp.max(el, axis=-1, keepdims=True)
    i1 = jnp.min(jnp.where(el == m1, lane_f, 1e9), axis=-1, keepdims=True)
    el2 = jnp.where(lane_f == i1, -jnp.inf, el)
    m2 = jnp.max(el2, axis=-1, keepdims=True)
    i2 = jnp.min(jnp.where(el2 == m2, lane_f, 1e9), axis=-1, keepdims=True)
    e2 = jnp.exp(m2 - m1)
    w1 = p_group / (1.0 + e2)
    w2 = p_group * e2 / (1.0 + e2)

    oh1 = lane_f == i1
    oh2 = lane_f == i2
    tri = tri_ref[...]
    c1 = _mm(tri, jnp.where(oh1, 1.0, 0.0).astype(bf16))
    c2 = _mm(tri, jnp.where(oh2, 1.0, 0.0).astype(bf16))
    run = run_s[...]
    tot1 = c1[tm - 1:tm, :]
    r1 = jnp.sum(jnp.where(oh1, run + c1 - 1.0, 0.0), axis=-1, keepdims=True)
    r2 = jnp.sum(jnp.where(oh2, run + tot1 + c2 - 1.0, 0.0), axis=-1, keepdims=True)
    run = run + tot1 + c2[tm - 1:tm, :]
    run_s[...] = run
    cnt_ref[...] = run
    meta = jnp.zeros((tm, LANES), f32)
    for pos, val in ((META_E, i1), (META_E + 1, i2), (META_RANK, r1), (META_RANK + 1, r2),
                     (META_W, w1), (META_W + 1, w2)):
        meta = jnp.where(lane == pos, val, meta)
    meta_ref[...] = meta


def _moe_scatter_kernel(d1_ref, d2_ref, x_hbm, zero_hbm, buf_hbm, sem):
    del zero_hbm
    base = pl.program_id(0) * MOE_TILE

    def row_copy(t, d_ref):
        return pltpu.make_async_copy(x_hbm.at[pl.ds(base + t, 1), :],
                                     buf_hbm.at[pl.ds(d_ref[0, 0, t], 1), :], sem)

    def issue(t, c):
        row_copy(t, d1_ref).start()
        row_copy(t, d2_ref).start()
        return c

    def drain(t, c):
        row_copy(t, d1_ref).wait()
        row_copy(t, d2_ref).wait()
        return c

    lax.fori_loop(0, MOE_TILE, issue, 0)
    lax.fori_loop(0, MOE_TILE, drain, 0)


def _moe_expert_kernel(blk_e_ref, n_used_ref, buf_ref, wg_ref, wu_ref, wd_ref, o_ref):
    del blk_e_ref

    @pl.when(pl.program_id(0) < n_used_ref[0])
    def _():
        xb = buf_ref[...].astype(bf16)
        gact = _mm(xb, wg_ref[0].astype(bf16))
        up = _mm(xb, wu_ref[0].astype(bf16))
        hid = gact * jax.nn.sigmoid(gact) * up
        o_ref[...] = _mm(hid.astype(bf16), wd_ref[0].astype(bf16))

    @pl.when(pl.program_id(0) >= n_used_ref[0])
    def _():
        o_ref[...] = jnp.zeros_like(o_ref)


def _moe_combine_kernel(d1_ref, d2_ref, x_ref, meta_ref, lnw_ref, lnb_ref, yb_hbm, o_ref, rows_s, sem):
    def row_copy(t, d_ref, k):
        return pltpu.make_async_copy(yb_hbm.at[pl.ds(d_ref[0, 0, t], 1), :],
                                     rows_s.at[k, pl.ds(t, 1), :], sem)

    def issue(t, c):
        row_copy(t, d1_ref, 0).start()
        row_copy(t, d2_ref, 1).start()
        return c

    def drain(t, c):
        row_copy(t, d1_ref, 0).wait()
        row_copy(t, d2_ref, 1).wait()
        return c

    lax.fori_loop(0, MOE_TILE, issue, 0)
    lax.fori_loop(0, MOE_TILE, drain, 0)
    meta = meta_ref[...]
    h = meta[:, META_W:META_W + 1] * rows_s[0] + meta[:, META_W + 1:META_W + 2] * rows_s[1]
    o_ref[...] = _layer_norm(DEEPNORM_ALPHA * x_ref[...] + h, lnw_ref[...], lnb_ref[...])


def _moe(x2, rw, rb, wg, wu, wd, lnw, lnb, tm=512):
    m = x2.shape[0]
    tm = min(tm, m)
    rwh, rwl = _split2(rw)
    tri = jnp.asarray(np.tril(np.ones((tm, tm))), bf16)
    full = lambda a: pl.BlockSpec(a.shape, lambda i: (0,) * a.ndim)
    meta, cnt = pl.pallas_call(
        _moe_route_kernel,
        grid=(m // tm,),
        in_specs=[pl.BlockSpec((tm, D_MODEL), lambda i: (i, 0)), full(rwh), full(rwl), full(rb), full(tri)],
        out_specs=[pl.BlockSpec((tm, LANES), lambda i: (i, 0)), pl.BlockSpec((1, LANES), lambda i: (0, 0))],
        out_shape=[jax.ShapeDtypeStruct((m, LANES), f32), jax.ShapeDtypeStruct((1, LANES), f32)],
        scratch_shapes=[pltpu.VMEM((1, LANES), f32)],
        compiler_params=pltpu.CompilerParams(
            dimension_semantics=("arbitrary",), vmem_limit_bytes=VMEM_LIMIT),
    )(x2, rwh, rwl, rb, tri)

    n_slots = m * 2
    n_blocks = n_slots // MOE_BLOCK + N_EXPERTS
    counts = cnt[0, :N_EXPERTS].astype(jnp.int32)
    padded = (counts + MOE_BLOCK - 1) // MOE_BLOCK * MOE_BLOCK
    p_ends = jnp.cumsum(padded)
    p_starts = p_ends - padded
    blk_e = jnp.minimum(jnp.searchsorted(p_ends, jnp.arange(n_blocks) * MOE_BLOCK, side='right'),
                        N_EXPERTS - 1).astype(jnp.int32)
    n_used = (p_ends[-1:] // MOE_BLOCK).astype(jnp.int32)
    ids = meta[:, :META_W].astype(jnp.int32)
    dest = p_starts[ids[:, META_E:META_E + 2]] + ids[:, META_RANK:META_RANK + 2]
    d1 = dest[:, 0].reshape(m // MOE_TILE, 1, MOE_TILE)
    d2 = dest[:, 1].reshape(m // MOE_TILE, 1, MOE_TILE)
    smem_tile = pl.BlockSpec((1, 1, MOE_TILE), lambda i: (i, 0, 0), memory_space=pltpu.SMEM)
    any_spec = pl.BlockSpec(memory_space=pl.ANY)

    buf = pl.pallas_call(
        _moe_scatter_kernel,
        grid=(m // MOE_TILE,),
        in_specs=[smem_tile, smem_tile, any_spec, any_spec],
        out_specs=any_spec,
        out_shape=jax.ShapeDtypeStruct((n_blocks * MOE_BLOCK, D_MODEL), f32),
        scratch_shapes=[pltpu.SemaphoreType.DMA(())],
        input_output_aliases={3: 0},
        compiler_params=pltpu.CompilerParams(dimension_semantics=("arbitrary",)),
    )(d1, d2, x2, jnp.zeros((n_blocks * MOE_BLOCK, D_MODEL), f32))

    used = lambda i, nu: jnp.minimum(i, nu[0] - 1)
    yb = pl.pallas_call(
        _moe_expert_kernel,
        grid_spec=pltpu.PrefetchScalarGridSpec(
            num_scalar_prefetch=2,
            grid=(n_blocks,),
            in_specs=[pl.BlockSpec((MOE_BLOCK, D_MODEL), lambda i, be, nu: (used(i, nu), 0)),
                      pl.BlockSpec((1, D_MODEL, D_EXPERT), lambda i, be, nu: (be[used(i, nu)], 0, 0)),
                      pl.BlockSpec((1, D_MODEL, D_EXPERT), lambda i, be, nu: (be[used(i, nu)], 0, 0)),
                      pl.BlockSpec((1, D_EXPERT, D_MODEL), lambda i, be, nu: (be[used(i, nu)], 0, 0))],
            out_specs=pl.BlockSpec((MOE_BLOCK, D_MODEL), lambda i, be, nu: (i, 0))),
        out_shape=jax.ShapeDtypeStruct((n_blocks * MOE_BLOCK, D_MODEL), f32),
        compiler_params=pltpu.CompilerParams(
            dimension_semantics=("arbitrary",), vmem_limit_bytes=VMEM_LIMIT),
    )(blk_e, n_used, buf, wg, wu, wd)

    rowblk = lambda n: pl.BlockSpec((MOE_TILE, n), lambda i: (i, 0))
    return pl.pallas_call(
        _moe_combine_kernel,
        grid=(m // MOE_TILE,),
        in_specs=[smem_tile, smem_tile, rowblk(D_MODEL), rowblk(LANES), full(lnw), full(lnb), any_spec],
        out_specs=rowblk(D_MODEL),
        out_shape=jax.ShapeDtypeStruct((m, D_MODEL), f32),
        scratch_shapes=[pltpu.VMEM((2, MOE_TILE, D_MODEL), f32), pltpu.SemaphoreType.DMA(())],
        compiler_params=pltpu.CompilerParams(
            dimension_semantics=("arbitrary",), vmem_limit_bytes=VMEM_LIMIT),
    )(d1, d2, x2, meta, lnw, lnb, yb)


def _router_weights(router_group_w, router_group_b, router_expert_w, router_expert_b):
    pad = LANES - N_EXPERTS - N_EXPERT_GROUPS
    rw = jnp.concatenate([router_expert_w.astype(f32), router_group_w.astype(f32),
                          jnp.zeros((D_MODEL, pad), f32)], axis=1)
    rb = jnp.concatenate([router_expert_b.astype(f32), router_group_b.astype(f32),
                          jnp.zeros((pad,), f32)]).reshape(1, LANES)
    return rw, rb


def kernel(x, w_in, rwkv_mu, rwkv_w0, rwkv_w2, rwkv_a0, rwkv_a2, rwkv_g2, rwkv_k_k, rwkv_k_a, rwkv_r_k, rwkv_ln_w, rwkv_ln_b, ssm_a_re, ssm_a_im, ssm_b_re, ssm_b_im, ssm_c_re, ssm_c_im, ssm_d, ssm_log_dt, ssm_glu_w, ssm_glu_b, w_up_rwkv, w_up_moba, w_up_ssm, gate_b, w_out, ln1_w, ln1_b, router_group_w, router_group_b, router_expert_w, router_expert_b, expert_w_gate, expert_w_up, expert_w_down, ln2_w, ln2_b):
    bsz, seq, _ = x.shape
    x2 = x.reshape(bsz * seq, D_MODEL).astype(f32)
    row = lambda a: a.reshape(1, -1).astype(f32)
    for l in range(DEPTH):
        w = w_in[l].astype(bf16)
        z = _project(x2, w[:, :OFF_MOBA])
        qkv = _project(x2, w[:, OFF_MOBA:OFF_SSM])
        u = _project(x2, w[:, OFF_SSM:OFF_GATE])
        y_a = _rwkv(z.reshape(bsz, seq, RWKV_COLS),
                    dict(mu=rwkv_mu[l], w0=rwkv_w0[l], w2=rwkv_w2[l], a0=rwkv_a0[l], a2=rwkv_a2[l],
                         g2=rwkv_g2[l], k_k=rwkv_k_k[l], k_a=rwkv_k_a[l], r_k=rwkv_r_k[l],
                         ln_w=rwkv_ln_w[l], ln_b=rwkv_ln_b[l]))
        y_b = _moba(qkv.reshape(bsz, seq, MOBA_COLS))
        y_c = _s5(u.reshape(bsz, seq, D_SSM),
                  dict(a_re=ssm_a_re[l], a_im=ssm_a_im[l], b_re=ssm_b_re[l], b_im=ssm_b_im[l],
                       c_re=ssm_c_re[l], c_im=ssm_c_im[l], d=ssm_d[l], log_dt=ssm_log_dt[l],
                       glu_w=ssm_glu_w[l], glu_b=ssm_glu_b[l]))
        x2 = _merge(x2, y_a.reshape(-1, D_RWKV), y_b.reshape(-1, D_MOBA), y_c.reshape(-1, D_SSM),
                    w[:, OFF_GATE:], row(gate_b[l]),
                    w_up_rwkv[l].astype(bf16), w_up_moba[l].astype(bf16), w_up_ssm[l].astype(bf16),
                    w_out[l].astype(bf16), row(ln1_w[l]), row(ln1_b[l]))
        rw, rb = _router_weights(router_group_w[l], router_group_b[l], router_expert_w[l],
                                 router_expert_b[l])
        x2 = _moe(x2, rw, rb, expert_w_gate[l], expert_w_up[l], expert_w_down[l],
                  row(ln2_w[l]), row(ln2_b[l]))
    return x2.reshape(bsz, seq, D_MODEL)
```

```python
import functools

import jax
import jax.numpy as jnp
import numpy as np
from jax import lax
from jax.experimental import pallas as pl
from jax.experimental.pallas import tpu as pltpu

f32 = jnp.float32
bf16 = jnp.bfloat16

D_MODEL = 1024
DEPTH = 4
HEAD_DIM = 64
D_RWKV = 512
RWKV_HEADS = D_RWKV // HEAD_DIM
DECAY_LORA = 64
ICL_LORA = 64
GATE_LORA = 128
RWKV_GN_EPS = 64e-5
D_MOBA = 512
MOBA_HEADS = D_MOBA // HEAD_DIM
MOBA_BLOCK = 256
MOBA_TOPK = 3
D_SSM = 512
SSM_GROUP = 16
SSM_GROUPS = D_SSM // SSM_GROUP
SSM_STATE = 64
N_BRANCHES = 3
N_EXPERT_GROUPS = 4
EXPERTS_PER_GROUP = 8
N_EXPERTS = N_EXPERT_GROUPS * EXPERTS_PER_GROUP
D_EXPERT = D_MODEL // 4
LN_EPS = 1e-5
DEEPNORM_ALPHA = (2 * DEPTH) ** 0.25
NEG_INF = -1e30
RWKV_COLS = 3 * D_RWKV + DECAY_LORA + ICL_LORA + GATE_LORA
MOBA_COLS = 3 * D_MOBA
OFF_MOBA = RWKV_COLS
OFF_SSM = OFF_MOBA + MOBA_COLS
OFF_GATE = OFF_SSM + D_SSM

LANES = 128
WKV_CHUNK = 64
WKV_PAIR = 2
VMEM_LIMIT = 48 * 1024 * 1024


def _nt(a, b):
    return lax.dot_general(a, b, (((1,), (1,)), ((), ())), preferred_element_type=f32)


def _tn(a, b):
    return lax.dot_general(a, b, (((0,), (0,)), ((), ())), preferred_element_type=f32)


def _mm(a, b):
    return jnp.dot(a, b, preferred_element_type=f32)


def _split2(x):
    hi = x.astype(bf16)
    lo = (x - hi.astype(f32)).astype(bf16)
    return hi, lo


def _layer_norm(y, w, b):
    mu = jnp.mean(y, axis=-1, keepdims=True)
    d = y - mu
    var = jnp.mean(d * d, axis=-1, keepdims=True)
    return d * lax.rsqrt(var + LN_EPS) * w + b


def _proj_kernel(x_ref, w_ref, o_ref):
    o_ref[...] = _mm(x_ref[...].astype(bf16), w_ref[...])


def _project(x2, w, tm=512):
    m, k = x2.shape
    n = w.shape[1]
    return pl.pallas_call(
        _proj_kernel,
        grid=(m // tm,),
        in_specs=[pl.BlockSpec((tm, k), lambda i: (i, 0)), pl.BlockSpec((k, n), lambda i: (0, 0))],
        out_specs=pl.BlockSpec((tm, n), lambda i: (i, 0)),
        out_shape=jax.ShapeDtypeStruct((m, n), f32),
        compiler_params=pltpu.CompilerParams(
            dimension_semantics=("parallel",), vmem_limit_bytes=VMEM_LIMIT),
    )(x2, w)


def _rwkv_kernel(z_ref, mu_ref, w0_ref, w2_ref, a0_ref, a2_ref, g2_ref, kk_ref, ka_ref, rk_ref,
                 lnw_ref, lnb_ref, seg_ref, tri_ref, o_ref,
                 zlast_s, state_s, r_s, k_s, v_s, a_s, b_s, lw_s, g_s, bon_s, y_s,
                 rt_s, bh_s, kh_s, tw_s, arb_s, pc_s):
    tb = z_ref.shape[1]
    n_chunks = tb // WKV_CHUNK
    c64 = WKV_CHUNK

    @pl.when(pl.program_id(1) == 0)
    def _():
        zlast_s[...] = jnp.zeros_like(zlast_s)
        state_s[...] = jnp.zeros_like(state_s)

    seg = seg_ref[...]

    def segsum(x):
        hi, lo = _split2(x)
        return _mm(hi, seg) + _mm(lo, seg)

    z = z_ref[0]
    row = lax.broadcasted_iota(jnp.int32, z.shape, 0)
    z_prev = jnp.where(row == 0, zlast_s[...], pltpu.roll(z, 1, axis=0))
    zlast_s[...] = z[tb - 1:tb, :]
    zz = z + (z_prev - z) * mu_ref[...]
    r = zz[:, 0:D_RWKV]
    k = zz[:, D_RWKV:2 * D_RWKV]
    v = zz[:, 2 * D_RWKV:3 * D_RWKV]
    o1 = 3 * D_RWKV
    xw = zz[:, o1:o1 + DECAY_LORA]
    xa = zz[:, o1 + DECAY_LORA:o1 + DECAY_LORA + ICL_LORA]
    xg = zz[:, o1 + DECAY_LORA + ICL_LORA:]
    wl = w0_ref[...] + _mm(jnp.tanh(xw).astype(bf16), w2_ref[...])
    softplus = jnp.maximum(-wl, 0.0) + jnp.log(1.0 + jnp.exp(-jnp.abs(wl)))
    lw_s[...] = -jnp.exp(-softplus - 0.5)
    a_icl = jax.nn.sigmoid(a0_ref[...] + _mm(xa.astype(bf16), a2_ref[...]))
    g_s[...] = _mm(jax.nn.sigmoid(xg).astype(bf16), g2_ref[...])
    kk = k * kk_ref[...]
    kk = kk * lax.rsqrt(jnp.maximum(segsum(kk * kk), 1e-24))
    k2 = k * (1.0 + (a_icl - 1.0) * ka_ref[...])
    r_s[...] = r
    k_s[...] = k2
    v_s[...] = v
    a_s[...] = -kk
    b_s[...] = kk * a_icl
    bon_s[...] = segsum(r * k2 * rk_ref[...])

    ri = lax.broadcasted_iota(jnp.int32, (c64, c64), 0)
    ci = lax.broadcasted_iota(jnp.int32, (c64, c64), 1)
    strict = ri > ci
    incl = ri >= ci
    eye = jnp.where(ri == ci, 1.0, 0.0).astype(f32)
    tri = tri_ref[...]

    heads = range(RWKV_HEADS)
    sls = [slice(h * HEAD_DIM, (h + 1) * HEAD_DIM) for h in heads]

    def chunk_terms(c):
        rows = pl.ds(pl.multiple_of(c * c64, c64), c64)
        lwc = lw_s[rows, :]
        lhi, llo = _split2(lwc)
        cl = _mm(tri, lhi) + _mm(tri, llo)
        cl_last = cl[c64 - 1:c64, :]
        e_neg = jnp.exp(-cl)
        e_tot = jnp.exp(cl_last - cl)
        ac = a_s[rows, :]
        bc = b_s[rows, :]
        kc = k_s[rows, :]
        r_t = r_s[rows, :] * jnp.exp(cl)
        rt_s[rows, :] = r_t
        bh_s[rows, :] = bc * e_tot
        kh_s[rows, :] = kc * e_tot
        pc_s[pl.ds(c, 1), :] = jnp.exp(cl_last)
        return rows, ac * jnp.exp(cl - lwc), r_t, bc * e_neg, kc * e_neg, v_s[rows, :]

    def intra_body(c2, carry):
        terms = [chunk_terms(c2 * WKV_PAIR + j) for j in range(WKV_PAIR)]
        probs = [(j, h) for j in range(WKV_PAIR) for h in heads]
        cat = lambda j, p, q, h: jnp.concatenate([terms[j][p][:, sls[h]], terms[j][q][:, sls[h]]],
                                                 axis=0).astype(bf16)
        gm = [_nt(cat(j, 1, 2, h), cat(j, 3, 4, h)) for j, h in probs]
        a_ab = [jnp.where(strict, g[:c64, :c64], 0.0) for g in gm]
        a_ak = [jnp.where(strict, g[:c64, c64:], 0.0).astype(bf16) for g in gm]
        a_rb = [jnp.where(incl, g[c64:, :c64], 0.0).astype(bf16) for g in gm]
        a_rk = [jnp.where(incl, g[c64:, c64:], 0.0).astype(bf16) for g in gm]
        vb = [terms[j][5][:, sls[h]].astype(bf16) for j, h in probs]
        akv = [_mm(a, v) for a, v in zip(a_ak, vb)]
        yloc = [_mm(a, v) for a, v in zip(a_rk, vb)]
        xs = a_ab
        t_inv = [eye + x for x in xs]
        for _ in range(5):
            xb = [x.astype(bf16) for x in xs]
            xs = [_mm(x, x) for x in xb]
            t_inv = [t + _mm(t.astype(bf16), x.astype(bf16)) for t, x in zip(t_inv, xs)]
        rhs = [jnp.concatenate([terms[j][1][:, sls[h]], akv[n]], axis=1).astype(bf16)
               for n, (j, h) in enumerate(probs)]
        tw = [_mm(t.astype(bf16), r) for t, r in zip(t_inv, rhs)]
        for j in range(WKV_PAIR):
            mine = slice(j * RWKV_HEADS, (j + 1) * RWKV_HEADS)
            rows = terms[j][0]
            tw_s[rows, :] = jnp.concatenate(tw[mine], axis=1)
            arb_s[rows, :] = jnp.concatenate(a_rb[mine], axis=1)
            y_s[rows, :] = jnp.concatenate(yloc[mine], axis=1)
        return carry

    lax.fori_loop(0, n_chunks // WKV_PAIR, intra_body, 0)

    def state_body(c, carry):
        rows = pl.ds(pl.multiple_of(c * c64, c64), c64)
        twc = tw_s[rows, :]
        arbc = arb_s[rows, :]
        r_t = rt_s[rows, :]
        b_h = bh_s[rows, :]
        k_h = kh_s[rows, :]
        vc = v_s[rows, :]
        yl = y_s[rows, :]
        p_c = pc_s[pl.ds(c, 1), :]
        s_old = [state_s[h] for h in heads]
        wr = [jnp.concatenate([twc[:, h * LANES:h * LANES + c64], r_t[:, sls[h]]], axis=0).astype(bf16)
              for h in heads]
        wrs = [_nt(wr[h], s_old[h].astype(bf16)) for h in heads]
        u = [wrs[h][:c64] + twc[:, h * LANES + c64:(h + 1) * LANES] for h in heads]
        ys = [wrs[h][c64:] + _mm(arbc[:, sls[h]], u[h].astype(bf16)) + yl[:, sls[h]] for h in heads]
        uv = [jnp.concatenate([u[h], vc[:, sls[h]]], axis=0).astype(bf16) for h in heads]
        bkh = [jnp.concatenate([b_h[:, sls[h]], k_h[:, sls[h]]], axis=0).astype(bf16) for h in heads]
        for h in heads:
            state_s[h] = s_old[h] * p_c[:, sls[h]] + _tn(uv[h], bkh[h])
        y_s[rows, :] = jnp.concatenate(ys, axis=1)
        return carry

    lax.fori_loop(0, n_chunks, state_body, 0)

    y = y_s[...]
    mean = segsum(y) * (1.0 / HEAD_DIM)
    d = y - mean
    var = segsum(d * d) * (1.0 / HEAD_DIM)
    yn = d * lax.rsqrt(var + RWKV_GN_EPS) * lnw_ref[...] + lnb_ref[...]
    o_ref[0] = (yn + bon_s[...] * v_s[...]) * g_s[...]


def _rwkv(z3, p, tb=512):
    bsz, seq, _ = z3.shape
    tb = min(tb, seq)
    seg = (np.arange(D_RWKV)[:, None] // HEAD_DIM == np.arange(D_RWKV)[None, :] // HEAD_DIM)
    seg = jnp.asarray(seg, bf16)
    tri = jnp.asarray(np.tril(np.ones((WKV_CHUNK, WKV_CHUNK))), bf16)
    row = lambda a: a.reshape(1, -1).astype(f32)
    args = [z3, row(p['mu']), row(p['w0']), p['w2'].astype(bf16), row(p['a0']), p['a2'].astype(bf16),
            p['g2'].astype(bf16), row(p['k_k']), row(p['k_a']), row(p['r_k']), row(p['ln_w']),
            row(p['ln_b']), seg, tri]
    full = lambda a: pl.BlockSpec(a.shape, lambda b, l: (0,) * a.ndim)
    in_specs = [pl.BlockSpec((1, tb, RWKV_COLS), lambda b, l: (b, l, 0))] + [full(a) for a in args[1:]]
    big = lambda: pltpu.VMEM((tb, D_RWKV), f32)
    return pl.pallas_call(
        _rwkv_kernel,
        grid=(bsz, seq // tb),
        in_specs=in_specs,
        out_specs=pl.BlockSpec((1, tb, D_RWKV), lambda b, l: (b, l, 0)),
        out_shape=jax.ShapeDtypeStruct((bsz, seq, D_RWKV), f32),
        scratch_shapes=[pltpu.VMEM((1, RWKV_COLS), f32),
                        pltpu.VMEM((RWKV_HEADS, HEAD_DIM, HEAD_DIM), f32),
                        big(), big(), big(), big(), big(), big(), big(), big(), big(),
                        big(), big(), big(), pltpu.VMEM((tb, RWKV_HEADS * LANES), f32),
                        pltpu.VMEM((tb, D_RWKV), bf16), pltpu.VMEM((tb // WKV_CHUNK, D_RWKV), f32)],
        compiler_params=pltpu.CompilerParams(
            dimension_semantics=("parallel", "arbitrary"), vmem_limit_bytes=VMEM_LIMIT),
    )(*args)


MOBA_AUG_POS = 0
MOBA_AUG_BLK = 6
LOG2E = 1.4426950408889634


MOBA_STEP_HEADS = 4


def _moba_kernel(q_ref, k_ref, v_ref, tmpl_ref, o_ref, kaug_s, vaug_s, kmean_s):
    blk = MOBA_BLOCK
    half = HEAD_DIM
    seq = k_ref.shape[1]
    n_blk = seq // blk
    qi = pl.program_id(2)
    heads = range(MOBA_STEP_HEADS)
    tile = [slice((hh // 2) * LANES, (hh // 2 + 1) * LANES) for hh in heads]
    aug0 = [half if hh % 2 == 0 else 0 for hh in heads]

    @pl.when(qi == 0)
    def _():
        lane_l = lax.broadcasted_iota(jnp.int32, (seq, LANES), 1)
        for hh in heads:
            kb = k_ref[0, :, tile[hh]].astype(bf16)
            vb = v_ref[0, :, tile[hh]].astype(bf16)
            in_head = (lane_l < half) if hh % 2 == 0 else (lane_l >= half)
            kaug_s[hh] = jnp.where(in_head, kb, tmpl_ref[0, hh])
            ones_lane = jnp.where(lane_l == aug0[hh], 1.0, 0.0).astype(bf16)
            vaug_s[hh] = jnp.where(in_head, vb, ones_lane)
        kmean_s[...] = jnp.zeros_like(kmean_s)
        for n in range(n_blk):
            km = jnp.mean(k_ref[0, n * blk:(n + 1) * blk, :], axis=0, keepdims=True)
            for tt in range(MOBA_STEP_HEADS // 2):
                kt = km[:, tt * LANES:(tt + 1) * LANES]
                kmean_s[tt, MOBA_AUG_BLK + n:MOBA_AUG_BLK + n + 1, :] = kt
                kmean_s[tt, half + MOBA_AUG_BLK + n:half + MOBA_AUG_BLK + n + 1, :] = kt

    ri = lax.broadcasted_iota(jnp.int32, (blk, blk), 0)
    ci = lax.broadcasted_iota(jnp.int32, (blk, blk), 1)
    causal = ri >= ci
    lane = lax.broadcasted_iota(jnp.int32, (blk, LANES), 1)
    lane_f = lane.astype(f32)
    qf = [q_ref[0, :, tile[hh]] * (HEAD_DIM ** -0.5 * LOG2E) for hh in heads]
    in_head = [(lane < half) if hh % 2 == 0 else (lane >= half) for hh in heads]
    blk_lane = [lane - (aug0[hh] + MOBA_AUG_BLK) for hh in heads]
    cand = [(blk_lane[hh] >= 0) & (blk_lane[hh] < qi) for hh in heads]

    qs = [_split2(jnp.where(in_head[hh], qf[hh], 0.0)) for hh in heads]
    kms = [_split2(kmean_s[tt]) for tt in range(MOBA_STEP_HEADS // 2)]
    gate = [_nt(qs[hh][0], kms[hh // 2][0]) + _nt(qs[hh][0], kms[hh // 2][1])
            + _nt(qs[hh][1], kms[hh // 2][0]) for hh in heads]
    g = [jnp.where(cand[hh], gate[hh], -jnp.inf) for hh in heads]
    picked = [jnp.zeros((blk, LANES), f32) for _ in heads]
    for _ in range(MOBA_TOPK):
        mx = [jnp.max(x, axis=-1, keepdims=True) for x in g]
        idx = [jnp.min(jnp.where(g[hh] == mx[hh], lane_f, 1e9), axis=-1, keepdims=True) for hh in heads]
        hit = [lane_f == i for i in idx]
        picked = [jnp.where(hit[hh], 1.0, picked[hh]) for hh in heads]
        g = [jnp.where(hit[hh], -jnp.inf, g[hh]) for hh in heads]
    q_aug, q_own = [], []
    for hh in heads:
        sel = (picked[hh] > 0.0) & cand[hh]
        is_pos = (lane >= aug0[hh] + MOBA_AUG_POS) & (lane < aug0[hh] + MOBA_AUG_POS + 3)
        is_blk = (blk_lane[hh] >= 0) & (blk_lane[hh] < n_blk)
        own = jnp.where(is_pos, 1.0, 0.0)
        aug = jnp.where(is_blk, jnp.where(sel, 0.0, NEG_INF), own)
        q_aug.append(jnp.where(in_head[hh], qf[hh], aug).astype(bf16))
        q_own.append(jnp.where(in_head[hh], qf[hh], own).astype(bf16))

    start = pl.multiple_of(qi * blk, blk)
    s0 = [_nt(q_own[hh], kaug_s[hh, pl.ds(start, blk), :]) for hh in heads]
    s0 = [jnp.where(causal, s, NEG_INF) for s in s0]
    m0 = [jnp.max(s, axis=-1, keepdims=True) for s in s0]
    p0 = [jnp.exp2(s - m).astype(bf16) for s, m in zip(s0, m0)]
    a0 = [_mm(p0[hh], vaug_s[hh, pl.ds(start, blk), :]) for hh in heads]

    def pair_body(t, carry):
        ms, accs = carry
        rows = pl.ds(pl.multiple_of(t * (2 * blk), 2 * blk), 2 * blk)
        s = [_nt(q_aug[hh], kaug_s[hh, rows, :]) for hh in heads]
        m_new = [jnp.maximum(ms[hh], jnp.max(s[hh], axis=-1, keepdims=True)) for hh in heads]
        p = [jnp.exp2(s[hh] - m_new[hh]).astype(bf16) for hh in heads]
        pv = [_mm(p[hh], vaug_s[hh, rows, :]) for hh in heads]
        accs = [accs[hh] * jnp.exp2(ms[hh] - m_new[hh]) + pv[hh] for hh in heads]
        return (tuple(m_new), tuple(accs))

    _, accs = lax.fori_loop(0, (qi + 1) // 2, pair_body, (tuple(m0), tuple(a0)))
    out = [accs[hh] / accs[hh][:, aug0[hh]:aug0[hh] + 1] for hh in heads]
    for tt in range(MOBA_STEP_HEADS // 2):
        o_ref[0, :, tt * LANES:(tt + 1) * LANES] = jnp.where(lane < half, out[2 * tt], out[2 * tt + 1])


def _moba_template(seq):
    pos = np.arange(seq, dtype=np.float64)
    tmpl = np.zeros((MOBA_HEADS, seq, LANES), np.float32)
    for h in range(MOBA_HEADS):
        a0 = HEAD_DIM if h % 2 == 0 else 0
        slope = 2.0 ** (-8.0 * (h + 1) / MOBA_HEADS)
        rest = slope * LOG2E * pos
        for c in range(3):
            part = rest.astype(np.float32).astype(bf16).astype(np.float64)
            tmpl[h, :, a0 + MOBA_AUG_POS + c] = part
            rest = rest - part
        tmpl[h, np.arange(seq), a0 + MOBA_AUG_BLK + np.arange(seq) // MOBA_BLOCK] = 1.0
    return jnp.asarray(tmpl.reshape(MOBA_HEADS // MOBA_STEP_HEADS, MOBA_STEP_HEADS, seq, LANES), bf16)


def _moba(qkv3):
    bsz, seq, _ = qkv3.shape
    blk = MOBA_BLOCK
    sh = MOBA_STEP_HEADS
    n_grp = MOBA_HEADS // sh
    wid = sh * HEAD_DIM
    assert seq % (2 * blk) == 0 and seq // blk <= HEAD_DIM - MOBA_AUG_BLK
    return pl.pallas_call(
        _moba_kernel,
        grid=(bsz, n_grp, seq // blk),
        in_specs=[pl.BlockSpec((1, blk, wid), lambda b, p, i: (b, i, p)),
                  pl.BlockSpec((1, seq, wid), lambda b, p, i: (b, 0, n_grp + p)),
                  pl.BlockSpec((1, seq, wid), lambda b, p, i: (b, 0, 2 * n_grp + p)),
                  pl.BlockSpec((1, sh, seq, LANES), lambda b, p, i: (p, 0, 0, 0))],
        out_specs=pl.BlockSpec((1, blk, wid), lambda b, p, i: (b, i, p)),
        out_shape=jax.ShapeDtypeStruct((bsz, seq, D_MOBA), f32),
        scratch_shapes=[pltpu.VMEM((sh, seq, LANES), bf16), pltpu.VMEM((sh, seq, LANES), bf16),
                        pltpu.VMEM((sh // 2, LANES, LANES), f32)],
        compiler_params=pltpu.CompilerParams(
            dimension_semantics=("parallel", "parallel", "arbitrary"), vmem_limit_bytes=VMEM_LIMIT),
    )(qkv3, qkv3, qkv3, _moba_template(seq))


S5_SUPER = 4
S5_SEGS = 8
S5_COLS = 512


def _s5_kernel(u_ref, perm_ref, bd_ref, cre_ref, cim_ref, lam_ref, d_ref, gw_ref, gb_ref, o_ref,
               hr_s, hi_s, carry_s):
    tb = u_ref.shape[1]
    seg = tb // S5_SEGS
    n_state = lam_ref.shape[1]
    sgw = n_state // S5_SUPER
    uw = D_SSM // S5_SUPER

    @pl.when(pl.program_id(1) == 0)
    def _():
        carry_s[...] = jnp.zeros_like(carry_s)

    u = u_ref[0]
    perm = perm_ref[...]
    ub = _mm(perm, u.astype(bf16)).astype(bf16)
    for sg in range(S5_SUPER):
        bu = _mm(ub[:, sg * uw:(sg + 1) * uw], bd_ref[sg])
        for j in range(sgw // LANES):
            hr_s[sg * (sgw // LANES) + j] = bu[:, j * LANES:(j + 1) * LANES]
            hi_s[sg * (sgw // LANES) + j] = bu[:, sgw + j * LANES:sgw + (j + 1) * LANES]

    tpc = S5_COLS // LANES

    def load(ref, cc, idx):
        return jnp.concatenate([ref[cc * tpc + j, idx, :] for j in range(tpc)], axis=1)

    def store(ref, cc, idx, val):
        for j in range(tpc):
            ref[cc * tpc + j, idx, :] = val[:, j * LANES:(j + 1) * LANES]

    for cc in range(n_state // S5_COLS):
        cols = slice(cc * S5_COLS, (cc + 1) * S5_COLS)
        lr = lam_ref[0:1, cols]
        li = lam_ref[1:2, cols]

        def local(r, carry, cc=cc, lr=lr, li=li):
            hr, hi = carry
            idx = pl.ds(pl.multiple_of(r * S5_SEGS, S5_SEGS), S5_SEGS)
            nr = lr * hr - li * hi + load(hr_s, cc, idx)
            ni = lr * hi + li * hr + load(hi_s, cc, idx)
            store(hr_s, cc, idx, nr)
            store(hi_s, cc, idx, ni)
            return nr, ni

        zero = jnp.zeros((S5_SEGS, S5_COLS), f32)
        er, ei = lax.fori_loop(0, seg, local, (zero, zero), unroll=4)
        pr, pi = lr, li
        for _ in range(seg.bit_length() - 1):
            pr, pi = pr * pr - pi * pi, 2.0 * pr * pi
        cr = carry_s[0:1, cols]
        ci = carry_s[1:2, cols]
        in_r, in_i = [], []
        for s in range(S5_SEGS):
            in_r.append(cr)
            in_i.append(ci)
            cr, ci = pr * cr - pi * ci + er[s:s + 1, :], pr * ci + pi * cr + ei[s:s + 1, :]
        carry_s[0:1, cols] = cr
        carry_s[1:2, cols] = ci

        def fix(r, carry, cc=cc, lr=lr, li=li):
            qr, qi = carry
            qr, qi = lr * qr - li * qi, lr * qi + li * qr
            idx = pl.ds(pl.multiple_of(r * S5_SEGS, S5_SEGS), S5_SEGS)
            store(hr_s, cc, idx, load(hr_s, cc, idx) + qr)
            store(hi_s, cc, idx, load(hi_s, cc, idx) + qi)
            return qr, qi

        lax.fori_loop(0, seg, fix, (jnp.concatenate(in_r, axis=0), jnp.concatenate(in_i, axis=0)),
                      unroll=4)

    slab = lambda ref, sg: jnp.concatenate(
        [ref[sg * (sgw // LANES) + j] for j in range(sgw // LANES)], axis=1).astype(bf16)
    ys = [_mm(slab(hr_s, sg), cre_ref[sg]) + _mm(slab(hi_s, sg), cim_ref[sg]) for sg in range(S5_SUPER)]
    yh, yl = _split2(jnp.concatenate(ys, axis=1))
    y = _tn(perm, yh) + _tn(perm, yl) + d_ref[...] * u
    y = 0.5 * y * (1.0 + jnp.tanh(0.7978845608028654 * (y + 0.044715 * (y * y * y))))
    zg = _mm(y.astype(bf16), gw_ref[...]) + gb_ref[...]
    o_ref[0] = zg[:, :D_SSM] * jax.nn.sigmoid(zg[:, D_SSM:])


def _s5_params(p):
    a_re = p['a_re'].astype(f32)
    a_im = p['a_im'].astype(f32)
    dt = jnp.exp(p['log_dt'].astype(f32))[:, None]
    mag = jnp.exp(a_re * dt)
    lam_re = mag * jnp.cos(a_im * dt)
    lam_im = mag * jnp.sin(a_im * dt)
    den = a_re * a_re + a_im * a_im
    nr = lam_re - 1.0
    coef_re = (nr * a_re + lam_im * a_im) / den
    coef_im = (lam_im * a_re - nr * a_im) / den
    b_re = p['b_re'].astype(f32)
    b_im = p['b_im'].astype(f32)
    bb_re = coef_re[..., None] * b_re - coef_im[..., None] * b_im
    bb_im = coef_re[..., None] * b_im + coef_im[..., None] * b_re
    gps = SSM_GROUPS // S5_SUPER
    eye = jnp.eye(gps, dtype=f32)
    n_state = SSM_GROUPS * SSM_STATE
    sgw = n_state // S5_SUPER
    uw = D_SSM // S5_SUPER

    def in_map(bb):
        bb = bb.reshape(S5_SUPER, gps, SSM_STATE, SSM_GROUP)
        return jnp.einsum('sgph,gk->sghkp', bb, eye).reshape(S5_SUPER, uw, sgw)

    def out_map(c):
        c = c.reshape(S5_SUPER, gps, SSM_GROUP, SSM_STATE)
        return jnp.einsum('sghp,gk->sgpkh', c, eye).reshape(S5_SUPER, sgw, uw)

    bd = jnp.concatenate([in_map(bb_re), in_map(bb_im)], axis=2)
    cre = out_map(p['c_re'].astype(f32))
    cim = -out_map(p['c_im'].astype(f32))
    lam = jnp.stack([lam_re.reshape(n_state), lam_im.reshape(n_state)], axis=0)
    return bd.astype(bf16), cre.astype(bf16), cim.astype(bf16), lam


def _s5(u3, p, tb=512):
    bsz, seq, _ = u3.shape
    tb = min(tb, seq)
    seg = tb // S5_SEGS
    assert tb % S5_SEGS == 0 and seg & (seg - 1) == 0
    n_state = SSM_GROUPS * SSM_STATE
    bd, cre, cim, lam = _s5_params(p)
    src = (np.arange(tb) % S5_SEGS) * seg + np.arange(tb) // S5_SEGS
    perm = jnp.asarray(np.arange(tb)[None, :] == src[:, None], bf16)
    args = [u3, perm, bd, cre, cim, lam, p['d'].reshape(1, -1).astype(f32), p['glu_w'].astype(bf16),
            p['glu_b'].reshape(1, -1).astype(f32)]
    full = lambda a: pl.BlockSpec(a.shape, lambda b, l: (0,) * a.ndim)
    return pl.pallas_call(
        _s5_kernel,
        grid=(bsz, seq // tb),
        in_specs=[pl.BlockSpec((1, tb, D_SSM), lambda b, l: (b, l, 0))] + [full(a) for a in args[1:]],
        out_specs=pl.BlockSpec((1, tb, D_SSM), lambda b, l: (b, l, 0)),
        out_shape=jax.ShapeDtypeStruct((bsz, seq, D_SSM), f32),
        scratch_shapes=[pltpu.VMEM((n_state // LANES, tb, LANES), f32),
                        pltpu.VMEM((n_state // LANES, tb, LANES), f32),
                        pltpu.VMEM((2, n_state), f32)],
        compiler_params=pltpu.CompilerParams(
            dimension_semantics=("parallel", "arbitrary"), vmem_limit_bytes=VMEM_LIMIT),
    )(*args)


def _merge_kernel(x_ref, ya_ref, yb_ref, yc_ref, wg_ref, gb_ref, wa_ref, wb_ref, wc_ref, wo_ref,
                  lnw_ref, lnb_ref, o_ref):
    x = x_ref[...]
    xb = x.astype(bf16)
    merged = None
    for br, (y_ref, w_ref) in enumerate(((ya_ref, wa_ref), (yb_ref, wb_ref), (yc_ref, wc_ref))):
        cols = slice(br * D_MODEL, (br + 1) * D_MODEL)
        gate = jax.nn.sigmoid(_mm(xb, wg_ref[:, cols]) + gb_ref[:, cols])
        term = gate * _mm(y_ref[...].astype(bf16), w_ref[...])
        merged = term if merged is None else merged + term
    h = _mm(merged.astype(bf16), wo_ref[...])
    o_ref[...] = _layer_norm(DEEPNORM_ALPHA * x + h, lnw_ref[...], lnb_ref[...])


def _merge(x2, ya, yb, yc, wg, gb, wa, wb, wc, wo, lnw, lnb, tm=256):
    m = x2.shape[0]
    rowblk = lambda n: pl.BlockSpec((tm, n), lambda i: (i, 0))
    full = lambda a: pl.BlockSpec(a.shape, lambda i: (0,) * a.ndim)
    consts = [wg, gb, wa, wb, wc, wo, lnw, lnb]
    return pl.pallas_call(
        _merge_kernel,
        grid=(m // tm,),
        in_specs=[rowblk(D_MODEL), rowblk(D_RWKV), rowblk(D_MOBA), rowblk(D_SSM)]
        + [full(a) for a in consts],
        out_specs=rowblk(D_MODEL),
        out_shape=jax.ShapeDtypeStruct((m, D_MODEL), f32),
        compiler_params=pltpu.CompilerParams(
            dimension_semantics=("parallel",), vmem_limit_bytes=VMEM_LIMIT),
    )(x2, ya, yb, yc, *consts)


MOE_STEP_EXPERTS = 4


def _moe_kernel(x_ref, rwh_ref, rwl_ref, rb_ref, wg_ref, wu_ref, wd_ref, lnw_ref, lnb_ref, o_ref,
                xb_s, wt_s, acc_s):
    e = pl.program_id(1)
    tm = x_ref.shape[0]
    lane = lax.broadcasted_iota(jnp.int32, (tm, LANES), 1)
    lane_f = lane.astype(f32)

    @pl.when(e == 0)
    def _():
        x = x_ref[...]
        xh, xl = _split2(x)
        xb_s[...] = xh
        rwh = rwh_ref[...]
        logits = _mm(xh, rwh) + _mm(xh, rwl_ref[...]) + _mm(xl, rwh) + rb_ref[...]
        is_grp = (lane >= N_EXPERTS) & (lane < N_EXPERTS + N_EXPERT_GROUPS)
        gl = jnp.where(is_grp, logits, -jnp.inf)
        gmax = jnp.max(gl, axis=-1, keepdims=True)
        gidx = jnp.min(jnp.where(gl == gmax, lane_f, 1e9), axis=-1, keepdims=True) - float(N_EXPERTS)
        p_group = 1.0 / jnp.sum(jnp.where(is_grp, jnp.exp(gl - gmax), 0.0), axis=-1, keepdims=True)
        grp_of_lane = jnp.floor(lane_f * (1.0 / EXPERTS_PER_GROUP))
        in_grp = (lane < N_EXPERTS) & (grp_of_lane == gidx)
        el = jnp.where(in_grp, logits, -jnp.inf)
        m1 = jnp.max(el, axis=-1, keepdims=True)
        i1 = jnp.min(jnp.where(el == m1, lane_f, 1e9), axis=-1, keepdims=True)
        el2 = jnp.where(lane_f == i1, -jnp.inf, el)
        m2 = jnp.max(el2, axis=-1, keepdims=True)
        i2 = jnp.min(jnp.where(el2 == m2, lane_f, 1e9), axis=-1, keepdims=True)
        e2 = jnp.exp(m2 - m1)
        w1 = p_group / (1.0 + e2)
        w2 = p_group * e2 / (1.0 + e2)
        wt_s[...] = jnp.where(lane_f == i1, w1, 0.0) + jnp.where(lane_f == i2, w2, 0.0)
        acc_s[...] = jnp.zeros_like(acc_s)

    xb = xb_s[...]
    wt = wt_s[...]
    gact = [_mm(xb, wg_ref[j]) for j in range(MOE_STEP_EXPERTS)]
    up = [_mm(xb, wu_ref[j]) for j in range(MOE_STEP_EXPERTS)]
    part = None
    for j in range(MOE_STEP_EXPERTS):
        wcol = jnp.sum(jnp.where(lane == e * MOE_STEP_EXPERTS + j, wt, 0.0), axis=-1, keepdims=True)
        hid = gact[j] * jax.nn.sigmoid(gact[j]) * up[j] * wcol
        out = _mm(hid.astype(bf16), wd_ref[j])
        part = out if part is None else part + out
    acc_s[...] += part

    @pl.when(e == N_EXPERTS // MOE_STEP_EXPERTS - 1)
    def _():
        o_ref[...] = _layer_norm(DEEPNORM_ALPHA * x_ref[...] + acc_s[...], lnw_ref[...], lnb_ref[...])


def _moe(x2, rw, rb, wg, wu, wd, lnw, lnb, tm=1024):
    m = x2.shape[0]
    tm = min(tm, m)
    rwh, rwl = _split2(rw)
    full = lambda a: pl.BlockSpec(a.shape, lambda i, e: (0,) * a.ndim)
    return pl.pallas_call(
        _moe_kernel,
        grid=(m // tm, N_EXPERTS // MOE_STEP_EXPERTS),
        in_specs=[pl.BlockSpec((tm, D_MODEL), lambda i, e: (i, 0)),
                  full(rwh), full(rwl), full(rb),
                  pl.BlockSpec((MOE_STEP_EXPERTS, D_MODEL, D_EXPERT), lambda i, e: (e, 0, 0)),
                  pl.BlockSpec((MOE_STEP_EXPERTS, D_MODEL, D_EXPERT), lambda i, e: (e, 0, 0)),
                  pl.BlockSpec((MOE_STEP_EXPERTS, D_EXPERT, D_MODEL), lambda i, e: (e, 0, 0)),
                  full(lnw), full(lnb)],
        out_specs=pl.BlockSpec((tm, D_MODEL), lambda i, e: (i, 0)),
        out_shape=jax.ShapeDtypeStruct((m, D_MODEL), f32),
        scratch_shapes=[pltpu.VMEM((tm, D_MODEL), bf16), pltpu.VMEM((tm, LANES), f32),
                        pltpu.VMEM((tm, D_MODEL), f32)],
        compiler_params=pltpu.CompilerParams(
            dimension_semantics=("parallel", "arbitrary"), vmem_limit_bytes=VMEM_LIMIT),
    )(x2, rwh, rwl, rb, wg, wu, wd, lnw, lnb)


def _router_weights(router_group_w, router_group_b, router_expert_w, router_expert_b):
    pad = LANES - N_EXPERTS - N_EXPERT_GROUPS
    rw = jnp.concatenate([router_expert_w.astype(f32), router_group_w.astype(f32),
                          jnp.zeros((D_MODEL, pad), f32)], axis=1)
    rb = jnp.concatenate([router_expert_b.astype(f32), router_group_b.astype(f32),
                          jnp.zeros((pad,), f32)]).reshape(1, LANES)
    return rw, rb


def kernel(x, w_in, rwkv_mu, rwkv_w0, rwkv_w2, rwkv_a0, rwkv_a2, rwkv_g2, rwkv_k_k, rwkv_k_a, rwkv_r_k, rwkv_ln_w, rwkv_ln_b, ssm_a_re, ssm_a_im, ssm_b_re, ssm_b_im, ssm_c_re, ssm_c_im, ssm_d, ssm_log_dt, ssm_glu_w, ssm_glu_b, w_up_rwkv, w_up_moba, w_up_ssm, gate_b, w_out, ln1_w, ln1_b, router_group_w, router_group_b, router_expert_w, router_expert_b, expert_w_gate, expert_w_up, expert_w_down, ln2_w, ln2_b):
    bsz, seq, _ = x.shape
    x2 = x.reshape(bsz * seq, D_MODEL).astype(f32)
    row = lambda a: a.reshape(1, -1).astype(f32)
    for l in range(DEPTH):
        w = w_in[l].astype(bf16)
        z = _project(x2, w[:, :OFF_MOBA])
        qkv = _project(x2, w[:, OFF_MOBA:OFF_SSM])
        u = _project(x2, w[:, OFF_SSM:OFF_GATE])
        y_a = _rwkv(z.reshape(bsz, seq, RWKV_COLS),
                    dict(mu=rwkv_mu[l], w0=rwkv_w0[l], w2=rwkv_w2[l], a0=rwkv_a0[l], a2=rwkv_a2[l],
                         g2=rwkv_g2[l], k_k=rwkv_k_k[l], k_a=rwkv_k_a[l], r_k=rwkv_r_k[l],
                         ln_w=rwkv_ln_w[l], ln_b=rwkv_ln_b[l]))
        y_b = _moba(qkv.reshape(bsz, seq, MOBA_COLS))
        y_c = _s5(u.reshape(bsz, seq, D_SSM),
                  dict(a_re=ssm_a_re[l], a_im=ssm_a_im[l], b_re=ssm_b_re[l], b_im=ssm_b_im[l],
                       c_re=ssm_c_re[l], c_im=ssm_c_im[l], d=ssm_d[l], log_dt=ssm_log_dt[l],
                       glu_w=ssm_glu_w[l], glu_b=ssm_glu_b[l]))
        x2 = _merge(x2, y_a.reshape(-1, D_RWKV), y_b.reshape(-1, D_MOBA), y_c.reshape(-1, D_SSM),
                    w[:, OFF_GATE:], row(gate_b[l]),
                    w_up_rwkv[l].astype(bf16), w_up_moba[l].astype(bf16), w_up_ssm[l].astype(bf16),
                    w_out[l].astype(bf16), row(ln1_w[l]), row(ln1_b[l]))
        rw, rb = _router_weights(router_group_w[l], router_group_b[l], router_expert_w[l],
                                 router_expert_b[l])
        x2 = _moe(x2, rw, rb, expert_w_gate[l].astype(bf16), expert_w_up[l].astype(bf16),
                  expert_w_down[l].astype(bf16), row(ln2_w[l]), row(ln2_b[l]))
    return x2.reshape(bsz, seq, D_MODEL)
```

```python
import functools

import jax
import jax.numpy as jnp
import numpy as np
from jax import lax
from jax.experimental import pallas as pl
from jax.experimental.pallas import tpu as pltpu

f32 = jnp.float32
bf16 = jnp.bfloat16

D_MODEL = 1024
DEPTH = 4
HEAD_DIM = 64
D_RWKV = 512
RWKV_HEADS = D_RWKV // HEAD_DIM
DECAY_LORA = 64
ICL_LORA = 64
GATE_LORA = 128
RWKV_GN_EPS = 64e-5
D_MOBA = 512
MOBA_HEADS = D_MOBA // HEAD_DIM
MOBA_BLOCK = 256
MOBA_TOPK = 3
D_SSM = 512
SSM_GROUP = 16
SSM_GROUPS = D_SSM // SSM_GROUP
SSM_STATE = 64
N_BRANCHES = 3
N_EXPERT_GROUPS = 4
EXPERTS_PER_GROUP = 8
N_EXPERTS = N_EXPERT_GROUPS * EXPERTS_PER_GROUP
D_EXPERT = D_MODEL // 4
LN_EPS = 1e-5
DEEPNORM_ALPHA = (2 * DEPTH) ** 0.25
NEG_INF = -1e30
RWKV_COLS = 3 * D_RWKV + DECAY_LORA + ICL_LORA + GATE_LORA
MOBA_COLS = 3 * D_MOBA
OFF_MOBA = RWKV_COLS
OFF_SSM = OFF_MOBA + MOBA_COLS
OFF_GATE = OFF_SSM + D_SSM

LANES = 128
WKV_CHUNK = 64
WKV_PAIR = 2
VMEM_LIMIT = 48 * 1024 * 1024


def _nt(a, b):
    return lax.dot_general(a, b, (((1,), (1,)), ((), ())), preferred_element_type=f32)


def _tn(a, b):
    return lax.dot_general(a, b, (((0,), (0,)), ((), ())), preferred_element_type=f32)


def _mm(a, b):
    return jnp.dot(a, b, preferred_element_type=f32)


def _split2(x):
    hi = x.astype(bf16)
    lo = (x - hi.astype(f32)).astype(bf16)
    return hi, lo


def _layer_norm(y, w, b):
    mu = jnp.mean(y, axis=-1, keepdims=True)
    d = y - mu
    var = jnp.mean(d * d, axis=-1, keepdims=True)
    return d * lax.rsqrt(var + LN_EPS) * w + b


def _proj_kernel(x_ref, w_ref, o_ref):
    o_ref[...] = _mm(x_ref[...].astype(bf16), w_ref[...])


def _project(x2, w, tm=512):
    m, k = x2.shape
    n = w.shape[1]
    return pl.pallas_call(
        _proj_kernel,
        grid=(m // tm,),
        in_specs=[pl.BlockSpec((tm, k), lambda i: (i, 0)), pl.BlockSpec((k, n), lambda i: (0, 0))],
        out_specs=pl.BlockSpec((tm, n), lambda i: (i, 0)),
        out_shape=jax.ShapeDtypeStruct((m, n), f32),
        compiler_params=pltpu.CompilerParams(
            dimension_semantics=("parallel",), vmem_limit_bytes=VMEM_LIMIT),
    )(x2, w)


def _rwkv_kernel(x_ref, wz_ref, mu_ref, w0_ref, w2_ref, a0_ref, a2_ref, g2_ref, kk_ref, ka_ref, rk_ref,
                 lnw_ref, lnb_ref, seg_ref, tri_ref, o_ref,
                 zlast_s, state_s, r_s, k_s, v_s, a_s, b_s, lw_s, g_s, bon_s, y_s,
                 rt_s, bh_s, kh_s, tw_s, arb_s, pc_s):
    tb = x_ref.shape[0]
    n_chunks = tb // WKV_CHUNK
    c64 = WKV_CHUNK

    @pl.when(pl.program_id(1) == 0)
    def _():
        zlast_s[...] = jnp.zeros_like(zlast_s)
        state_s[...] = jnp.zeros_like(state_s)

    seg = seg_ref[...]

    def segsum(x):
        hi, lo = _split2(x)
        return _mm(hi, seg) + _mm(lo, seg)

    z = _mm(x_ref[...].astype(bf16), wz_ref[...])
    row = lax.broadcasted_iota(jnp.int32, z.shape, 0)
    z_prev = jnp.where(row == 0, zlast_s[...], pltpu.roll(z, 1, axis=0))
    zlast_s[...] = z[tb - 1:tb, :]
    zz = z + (z_prev - z) * mu_ref[...]
    r = zz[:, 0:D_RWKV]
    k = zz[:, D_RWKV:2 * D_RWKV]
    v = zz[:, 2 * D_RWKV:3 * D_RWKV]
    o1 = 3 * D_RWKV
    xw = zz[:, o1:o1 + DECAY_LORA]
    xa = zz[:, o1 + DECAY_LORA:o1 + DECAY_LORA + ICL_LORA]
    xg = zz[:, o1 + DECAY_LORA + ICL_LORA:]
    wl = w0_ref[...] + _mm(jnp.tanh(xw).astype(bf16), w2_ref[...])
    softplus = jnp.maximum(-wl, 0.0) + jnp.log(1.0 + jnp.exp(-jnp.abs(wl)))
    lw_s[...] = -jnp.exp(-softplus - 0.5)
    a_icl = jax.nn.sigmoid(a0_ref[...] + _mm(xa.astype(bf16), a2_ref[...]))
    g_s[...] = _mm(jax.nn.sigmoid(xg).astype(bf16), g2_ref[...])
    kk = k * kk_ref[...]
    kk = kk * lax.rsqrt(jnp.maximum(segsum(kk * kk), 1e-24))
    k2 = k * (1.0 + (a_icl - 1.0) * ka_ref[...])
    r_s[...] = r
    k_s[...] = k2
    v_s[...] = v
    a_s[...] = -kk
    b_s[...] = kk * a_icl
    bon_s[...] = segsum(r * k2 * rk_ref[...])

    ri = lax.broadcasted_iota(jnp.int32, (c64, c64), 0)
    ci = lax.broadcasted_iota(jnp.int32, (c64, c64), 1)
    strict = ri > ci
    incl = ri >= ci
    eye = jnp.where(ri == ci, 1.0, 0.0).astype(f32)
    tri = tri_ref[...]

    heads = range(RWKV_HEADS)
    sls = [slice(h * HEAD_DIM, (h + 1) * HEAD_DIM) for h in heads]

    def chunk_terms(c):
        rows = pl.ds(pl.multiple_of(c * c64, c64), c64)
        lwc = lw_s[rows, :]
        lhi, llo = _split2(lwc)
        cl = _mm(tri, lhi) + _mm(tri, llo)
        cl_last = cl[c64 - 1:c64, :]
        e_neg = jnp.exp(-cl)
        e_tot = jnp.exp(cl_last - cl)
        ac = a_s[rows, :]
        bc = b_s[rows, :]
        kc = k_s[rows, :]
        r_t = r_s[rows, :] * jnp.exp(cl)
        rt_s[rows, :] = r_t
        bh_s[rows, :] = bc * e_tot
        kh_s[rows, :] = kc * e_tot
        pc_s[pl.ds(c, 1), :] = jnp.exp(cl_last)
        return rows, ac * jnp.exp(cl - lwc), r_t, bc * e_neg, kc * e_neg, v_s[rows, :]

    def intra_body(c2, carry):
        terms = [chunk_terms(c2 * WKV_PAIR + j) for j in range(WKV_PAIR)]
        probs = [(j, h) for j in range(WKV_PAIR) for h in heads]
        cat = lambda j, p, q, h: jnp.concatenate([terms[j][p][:, sls[h]], terms[j][q][:, sls[h]]],
                                                 axis=0).astype(bf16)
        gm = [_nt(cat(j, 1, 2, h), cat(j, 3, 4, h)) for j, h in probs]
        a_ab = [jnp.where(strict, g[:c64, :c64], 0.0) for g in gm]
        a_ak = [jnp.where(strict, g[:c64, c64:], 0.0).astype(bf16) for g in gm]
        a_rb = [jnp.where(incl, g[c64:, :c64], 0.0).astype(bf16) for g in gm]
        a_rk = [jnp.where(incl, g[c64:, c64:], 0.0).astype(bf16) for g in gm]
        vb = [terms[j][5][:, sls[h]].astype(bf16) for j, h in probs]
        akv = [_mm(a, v) for a, v in zip(a_ak, vb)]
        yloc = [_mm(a, v) for a, v in zip(a_rk, vb)]
        xs = a_ab
        t_inv = [eye + x for x in xs]
        for _ in range(5):
            xb = [x.astype(bf16) for x in xs]
            xs = [_mm(x, x) for x in xb]
            t_inv = [t + _mm(t.astype(bf16), x.astype(bf16)) for t, x in zip(t_inv, xs)]
        rhs = [jnp.concatenate([terms[j][1][:, sls[h]], akv[n]], axis=1).astype(bf16)
               for n, (j, h) in enumerate(probs)]
        tw = [_mm(t.astype(bf16), r) for t, r in zip(t_inv, rhs)]
        for j in range(WKV_PAIR):
            mine = slice(j * RWKV_HEADS, (j + 1) * RWKV_HEADS)
            rows = terms[j][0]
            tw_s[rows, :] = jnp.concatenate(tw[mine], axis=1)
            arb_s[rows, :] = jnp.concatenate(a_rb[mine], axis=1)
            y_s[rows, :] = jnp.concatenate(yloc[mine], axis=1)
        return carry

    lax.fori_loop(0, n_chunks // WKV_PAIR, intra_body, 0)

    def state_body(c, carry):
        rows = pl.ds(pl.multiple_of(c * c64, c64), c64)
        twc = tw_s[rows, :]
        arbc = arb_s[rows, :]
        r_t = rt_s[rows, :]
        b_h = bh_s[rows, :]
        k_h = kh_s[rows, :]
        vc = v_s[rows, :]
        yl = y_s[rows, :]
        p_c = pc_s[pl.ds(c, 1), :]
        s_old = [state_s[h] for h in heads]
        wr = [jnp.concatenate([twc[:, h * LANES:h * LANES + c64], r_t[:, sls[h]]], axis=0).astype(bf16)
              for h in heads]
        wrs = [_nt(wr[h], s_old[h].astype(bf16)) for h in heads]
        u = [wrs[h][:c64] + twc[:, h * LANES + c64:(h + 1) * LANES] for h in heads]
        ys = [wrs[h][c64:] + _mm(arbc[:, sls[h]], u[h].astype(bf16)) + yl[:, sls[h]] for h in heads]
        uv = [jnp.concatenate([u[h], vc[:, sls[h]]], axis=0).astype(bf16) for h in heads]
        bkh = [jnp.concatenate([b_h[:, sls[h]], k_h[:, sls[h]]], axis=0).astype(bf16) for h in heads]
        for h in heads:
            state_s[h] = s_old[h] * p_c[:, sls[h]] + _tn(uv[h], bkh[h])
        y_s[rows, :] = jnp.concatenate(ys, axis=1)
        return carry

    lax.fori_loop(0, n_chunks, state_body, 0)

    y = y_s[...]
    mean = segsum(y) * (1.0 / HEAD_DIM)
    d = y - mean
    var = segsum(d * d) * (1.0 / HEAD_DIM)
    yn = d * lax.rsqrt(var + RWKV_GN_EPS) * lnw_ref[...] + lnb_ref[...]
    o_ref[0] = (yn + bon_s[...] * v_s[...]) * g_s[...]


def _rwkv(x2, wz, p, bsz, seq, tb=512):
    tb = min(tb, seq)
    n_l = seq // tb
    seg = (np.arange(D_RWKV)[:, None] // HEAD_DIM == np.arange(D_RWKV)[None, :] // HEAD_DIM)
    seg = jnp.asarray(seg, bf16)
    tri = jnp.asarray(np.tril(np.ones((WKV_CHUNK, WKV_CHUNK))), bf16)
    row = lambda a: a.reshape(1, -1).astype(f32)
    args = [x2, wz, row(p['mu']), row(p['w0']), p['w2'].astype(bf16), row(p['a0']), p['a2'].astype(bf16),
            p['g2'].astype(bf16), row(p['k_k']), row(p['k_a']), row(p['r_k']), row(p['ln_w']),
            row(p['ln_b']), seg, tri]
    full = lambda a: pl.BlockSpec(a.shape, lambda b, l: (0,) * a.ndim)
    in_specs = [pl.BlockSpec((tb, D_MODEL), lambda b, l: (b * n_l + l, 0))] + [full(a) for a in args[1:]]
    big = lambda: pltpu.VMEM((tb, D_RWKV), f32)
    return pl.pallas_call(
        _rwkv_kernel,
        grid=(bsz, n_l),
        in_specs=in_specs,
        out_specs=pl.BlockSpec((1, tb, D_RWKV), lambda b, l: (b, l, 0)),
        out_shape=jax.ShapeDtypeStruct((bsz, seq, D_RWKV), f32),
        scratch_shapes=[pltpu.VMEM((1, RWKV_COLS), f32),
                        pltpu.VMEM((RWKV_HEADS, HEAD_DIM, HEAD_DIM), f32),
                        big(), big(), big(), big(), big(), big(), big(), big(), big(),
                        big(), big(), big(), pltpu.VMEM((tb, RWKV_HEADS * LANES), f32),
                        pltpu.VMEM((tb, D_RWKV), bf16), pltpu.VMEM((tb // WKV_CHUNK, D_RWKV), f32)],
        compiler_params=pltpu.CompilerParams(
            dimension_semantics=("parallel", "arbitrary"), vmem_limit_bytes=VMEM_LIMIT),
    )(*args)


MOBA_AUG_POS = 0
MOBA_AUG_BLK = 6
LOG2E = 1.4426950408889634


MOBA_STEP_HEADS = 4


def _moba_kernel(q_ref, k_ref, v_ref, tmpl_ref, o_ref, kaug_s, vaug_s, kmean_s):
    blk = MOBA_BLOCK
    half = HEAD_DIM
    seq = k_ref.shape[1]
    n_blk = seq // blk
    qi = pl.program_id(2)
    heads = range(MOBA_STEP_HEADS)
    tile = [slice((hh // 2) * LANES, (hh // 2 + 1) * LANES) for hh in heads]
    aug0 = [half if hh % 2 == 0 else 0 for hh in heads]

    @pl.when(qi == 0)
    def _():
        lane_l = lax.broadcasted_iota(jnp.int32, (seq, LANES), 1)
        for hh in heads:
            kb = k_ref[0, :, tile[hh]].astype(bf16)
            vb = v_ref[0, :, tile[hh]].astype(bf16)
            in_head = (lane_l < half) if hh % 2 == 0 else (lane_l >= half)
            kaug_s[hh] = jnp.where(in_head, kb, tmpl_ref[0, hh])
            ones_lane = jnp.where(lane_l == aug0[hh], 1.0, 0.0).astype(bf16)
            vaug_s[hh] = jnp.where(in_head, vb, ones_lane)
        for n in range(n_blk):
            km = jnp.mean(k_ref[0, n * blk:(n + 1) * blk, :], axis=0, keepdims=True)
            for tt in range(MOBA_STEP_HEADS // 2):
                kmean_s[tt, n:n + 1, :] = km[:, tt * LANES:(tt + 1) * LANES]

    ri = lax.broadcasted_iota(jnp.int32, (blk, blk), 0)
    ci = lax.broadcasted_iota(jnp.int32, (blk, blk), 1)
    causal = ri >= ci
    lane = lax.broadcasted_iota(jnp.int32, (blk, LANES), 1)
    lane_f = lane.astype(f32)
    qf = [q_ref[0, :, tile[hh]] * (HEAD_DIM ** -0.5 * LOG2E) for hh in heads]
    in_head = [(lane < half) if hh % 2 == 0 else (lane >= half) for hh in heads]
    blk_lane = [lane - (aug0[hh] + MOBA_AUG_BLK) for hh in heads]

    qs = [_split2(jnp.where(in_head[hh], qf[hh], 0.0)) for hh in heads]
    kms = [_split2(kmean_s[tt]) for tt in range(MOBA_STEP_HEADS // 2)]
    gate = [_nt(kms[hh // 2][0], qs[hh][0]) + _nt(kms[hh // 2][1], qs[hh][0])
            + _nt(kms[hh // 2][0], qs[hh][1]) for hh in heads]
    brow = lax.broadcasted_iota(jnp.int32, (n_blk, blk), 0)
    brow_f = brow.astype(f32)
    g = [jnp.where(brow < qi, x, -jnp.inf) for x in gate]
    picked = [jnp.zeros((n_blk, blk), f32) for _ in heads]
    for _ in range(MOBA_TOPK):
        mx = [jnp.max(x, axis=0, keepdims=True) for x in g]
        idx = [jnp.min(jnp.where(g[hh] == mx[hh], brow_f, 1e9), axis=0, keepdims=True) for hh in heads]
        hit = [brow_f == i for i in idx]
        picked = [jnp.where(hit[hh], 1.0, picked[hh]) for hh in heads]
        g = [jnp.where(hit[hh], -jnp.inf, g[hh]) for hh in heads]
    sel_t = [jnp.where(brow < qi, p, 0.0).astype(bf16) for p in picked]
    lane_b = lax.broadcasted_iota(jnp.int32, (n_blk, LANES), 1)
    brow_b = lax.broadcasted_iota(jnp.int32, (n_blk, LANES), 0)
    q_aug, q_own = [], []
    for hh in heads:
        place = jnp.where(lane_b == brow_b + (aug0[hh] + MOBA_AUG_BLK), 1.0, 0.0).astype(bf16)
        sel = _tn(sel_t[hh], place) > 0.5
        is_pos = (lane >= aug0[hh] + MOBA_AUG_POS) & (lane < aug0[hh] + MOBA_AUG_POS + 3)
        is_blk = (blk_lane[hh] >= 0) & (blk_lane[hh] < n_blk)
        own = jnp.where(is_pos, 1.0, 0.0)
        aug = jnp.where(is_blk, jnp.where(sel, 0.0, NEG_INF), own)
        q_aug.append(jnp.where(in_head[hh], qf[hh], aug).astype(bf16))
        q_own.append(jnp.where(in_head[hh], qf[hh], own).astype(bf16))

    start = pl.multiple_of(qi * blk, blk)
    s0 = [_nt(q_own[hh], kaug_s[hh, pl.ds(start, blk), :]) for hh in heads]
    s0 = [jnp.where(causal, s, NEG_INF) for s in s0]
    m0 = [jnp.max(s, axis=-1, keepdims=True) for s in s0]
    p0 = [jnp.exp2(s - m).astype(bf16) for s, m in zip(s0, m0)]
    a0 = [_mm(p0[hh], vaug_s[hh, pl.ds(start, blk), :]) for hh in heads]

    def make_body(width, base):
        def body(t, carry):
            ms, accs = carry
            rows = pl.ds(pl.multiple_of(base + t * (width * blk), 2 * blk), width * blk)
            s = [_nt(q_aug[hh], kaug_s[hh, rows, :]) for hh in heads]
            m_new = [jnp.maximum(ms[hh], jnp.max(s[hh], axis=-1, keepdims=True)) for hh in heads]
            p = [jnp.exp2(s[hh] - m_new[hh]).astype(bf16) for hh in heads]
            pv = [_mm(p[hh], vaug_s[hh, rows, :]) for hh in heads]
            accs = [accs[hh] * jnp.exp2(ms[hh] - m_new[hh]) + pv[hh] for hh in heads]
            return (tuple(m_new), tuple(accs))
        return body

    n_pairs = (qi + 1) // 2
    n_quads = n_pairs // 2
    carry = lax.fori_loop(0, n_quads, make_body(4, 0), (tuple(m0), tuple(a0)))
    _, accs = lax.fori_loop(0, n_pairs - 2 * n_quads, make_body(2, n_quads * (4 * blk)), carry)
    out = [accs[hh] / accs[hh][:, aug0[hh]:aug0[hh] + 1] for hh in heads]
    for tt in range(MOBA_STEP_HEADS // 2):
        o_ref[0, :, tt * LANES:(tt + 1) * LANES] = jnp.where(lane < half, out[2 * tt], out[2 * tt + 1])


def _moba_template(seq):
    pos = np.arange(seq, dtype=np.float64)
    tmpl = np.zeros((MOBA_HEADS, seq, LANES), np.float32)
    for h in range(MOBA_HEADS):
        a0 = HEAD_DIM if h % 2 == 0 else 0
        slope = 2.0 ** (-8.0 * (h + 1) / MOBA_HEADS)
        rest = slope * LOG2E * pos
        for c in range(3):
            part = rest.astype(np.float32).astype(bf16).astype(np.float64)
            tmpl[h, :, a0 + MOBA_AUG_POS + c] = part
            rest = rest - part
        tmpl[h, np.arange(seq), a0 + MOBA_AUG_BLK + np.arange(seq) // MOBA_BLOCK] = 1.0
    return jnp.asarray(tmpl.reshape(MOBA_HEADS // MOBA_STEP_HEADS, MOBA_STEP_HEADS, seq, LANES), bf16)


def _moba(qkv3):
    bsz, seq, _ = qkv3.shape
    blk = MOBA_BLOCK
    sh = MOBA_STEP_HEADS
    n_grp = MOBA_HEADS // sh
    wid = sh * HEAD_DIM
    assert seq % (2 * blk) == 0 and seq // blk <= HEAD_DIM - MOBA_AUG_BLK
    return pl.pallas_call(
        _moba_kernel,
        grid=(bsz, n_grp, seq // blk),
        in_specs=[pl.BlockSpec((1, blk, wid), lambda b, p, i: (b, i, p)),
                  pl.BlockSpec((1, seq, wid), lambda b, p, i: (b, 0, n_grp + p)),
                  pl.BlockSpec((1, seq, wid), lambda b, p, i: (b, 0, 2 * n_grp + p)),
                  pl.BlockSpec((1, sh, seq, LANES), lambda b, p, i: (p, 0, 0, 0))],
        out_specs=pl.BlockSpec((1, blk, wid), lambda b, p, i: (b, i, p)),
        out_shape=jax.ShapeDtypeStruct((bsz, seq, D_MOBA), f32),
        scratch_shapes=[pltpu.VMEM((sh, seq, LANES), bf16), pltpu.VMEM((sh, seq, LANES), bf16),
                        pltpu.VMEM((sh // 2, seq // blk, LANES), f32)],
        compiler_params=pltpu.CompilerParams(
            dimension_semantics=("parallel", "parallel", "arbitrary"), vmem_limit_bytes=VMEM_LIMIT),
    )(qkv3, qkv3, qkv3, _moba_template(seq))


S5_SUPER = 4
S5_SEGS = 8
S5_COLS = 512


def _s5_kernel(x_ref, wu_ref, perm_ref, bd_ref, cre_ref, cim_ref, lam_ref, d_ref, gw_ref, gb_ref, o_ref,
               hr_s, hi_s, carry_s):
    tb = x_ref.shape[0]
    seg = tb // S5_SEGS
    n_state = lam_ref.shape[1]
    sgw = n_state // S5_SUPER
    uw = D_SSM // S5_SUPER

    @pl.when(pl.program_id(1) == 0)
    def _():
        carry_s[...] = jnp.zeros_like(carry_s)

    u = _mm(x_ref[...].astype(bf16), wu_ref[...])
    perm = perm_ref[...]
    ub = _mm(perm, u.astype(bf16)).astype(bf16)
    for sg in range(S5_SUPER):
        bu = _mm(ub[:, sg * uw:(sg + 1) * uw], bd_ref[sg])
        for j in range(sgw // LANES):
            hr_s[sg * (sgw // LANES) + j] = bu[:, j * LANES:(j + 1) * LANES]
            hi_s[sg * (sgw // LANES) + j] = bu[:, sgw + j * LANES:sgw + (j + 1) * LANES]

    tpc = S5_COLS // LANES

    def load(ref, cc, idx):
        return jnp.concatenate([ref[cc * tpc + j, idx, :] for j in range(tpc)], axis=1)

    def store(ref, cc, idx, val):
        for j in range(tpc):
            ref[cc * tpc + j, idx, :] = val[:, j * LANES:(j + 1) * LANES]

    for cc in range(n_state // S5_COLS):
        cols = slice(cc * S5_COLS, (cc + 1) * S5_COLS)
        lr = lam_ref[0:1, cols]
        li = lam_ref[1:2, cols]

        def local(r, carry, cc=cc, lr=lr, li=li):
            hr, hi = carry
            idx = pl.ds(pl.multiple_of(r * S5_SEGS, S5_SEGS), S5_SEGS)
            nr = lr * hr - li * hi + load(hr_s, cc, idx)
            ni = lr * hi + li * hr + load(hi_s, cc, idx)
            store(hr_s, cc, idx, nr)
            store(hi_s, cc, idx, ni)
            return nr, ni

        zero = jnp.zeros((S5_SEGS, S5_COLS), f32)
        er, ei = lax.fori_loop(0, seg, local, (zero, zero), unroll=4)
        pr, pi = lr, li
        for _ in range(seg.bit_length() - 1):
            pr, pi = pr * pr - pi * pi, 2.0 * pr * pi
        cr = carry_s[0:1, cols]
        ci = carry_s[1:2, cols]
        in_r, in_i = [], []
        for s in range(S5_SEGS):
            in_r.append(cr)
            in_i.append(ci)
            cr, ci = pr * cr - pi * ci + er[s:s + 1, :], pr * ci + pi * cr + ei[s:s + 1, :]
        carry_s[0:1, cols] = cr
        carry_s[1:2, cols] = ci

        def fix(r, carry, cc=cc, lr=lr, li=li):
            qr, qi = carry
            qr, qi = lr * qr - li * qi, lr * qi + li * qr
            idx = pl.ds(pl.multiple_of(r * S5_SEGS, S5_SEGS), S5_SEGS)
            store(hr_s, cc, idx, load(hr_s, cc, idx) + qr)
            store(hi_s, cc, idx, load(hi_s, cc, idx) + qi)
            return qr, qi

        lax.fori_loop(0, seg, fix, (jnp.concatenate(in_r, axis=0), jnp.concatenate(in_i, axis=0)),
                      unroll=4)

    slab = lambda ref, sg: jnp.concatenate(
        [ref[sg * (sgw // LANES) + j] for j in range(sgw // LANES)], axis=1).astype(bf16)
    ys = [_mm(slab(hr_s, sg), cre_ref[sg]) + _mm(slab(hi_s, sg), cim_ref[sg]) for sg in range(S5_SUPER)]
    yh, yl = _split2(jnp.concatenate(ys, axis=1))
    y = _tn(perm, yh) + _tn(perm, yl) + d_ref[...] * u
    y = 0.5 * y * (1.0 + jnp.tanh(0.7978845608028654 * (y + 0.044715 * (y * y * y))))
    zg = _mm(y.astype(bf16), gw_ref[...]) + gb_ref[...]
    o_ref[0] = zg[:, :D_SSM] * jax.nn.sigmoid(zg[:, D_SSM:])


def _s5_params(p):
    a_re = p['a_re'].astype(f32)
    a_im = p['a_im'].astype(f32)
    dt = jnp.exp(p['log_dt'].astype(f32))[:, None]
    mag = jnp.exp(a_re * dt)
    lam_re = mag * jnp.cos(a_im * dt)
    lam_im = mag * jnp.sin(a_im * dt)
    den = a_re * a_re + a_im * a_im
    nr = lam_re - 1.0
    coef_re = (nr * a_re + lam_im * a_im) / den
    coef_im = (lam_im * a_re - nr * a_im) / den
    b_re = p['b_re'].astype(f32)
    b_im = p['b_im'].astype(f32)
    bb_re = coef_re[..., None] * b_re - coef_im[..., None] * b_im
    bb_im = coef_re[..., None] * b_im + coef_im[..., None] * b_re
    gps = SSM_GROUPS // S5_SUPER
    eye = jnp.eye(gps, dtype=f32)
    n_state = SSM_GROUPS * SSM_STATE
    sgw = n_state // S5_SUPER
    uw = D_SSM // S5_SUPER

    def in_map(bb):
        bb = bb.reshape(S5_SUPER, gps, SSM_STATE, SSM_GROUP)
        return jnp.einsum('sgph,gk->sghkp', bb, eye).reshape(S5_SUPER, uw, sgw)

    def out_map(c):
        c = c.reshape(S5_SUPER, gps, SSM_GROUP, SSM_STATE)
        return jnp.einsum('sghp,gk->sgpkh', c, eye).reshape(S5_SUPER, sgw, uw)

    bd = jnp.concatenate([in_map(bb_re), in_map(bb_im)], axis=2)
    cre = out_map(p['c_re'].astype(f32))
    cim = -out_map(p['c_im'].astype(f32))
    lam = jnp.stack([lam_re.reshape(n_state), lam_im.reshape(n_state)], axis=0)
    return bd.astype(bf16), cre.astype(bf16), cim.astype(bf16), lam


def _s5(x2, wu, p, bsz, seq, tb=512):
    tb = min(tb, seq)
    n_l = seq // tb
    seg = tb // S5_SEGS
    assert tb % S5_SEGS == 0 and seg & (seg - 1) == 0
    n_state = SSM_GROUPS * SSM_STATE
    bd, cre, cim, lam = _s5_params(p)
    src = (np.arange(tb) % S5_SEGS) * seg + np.arange(tb) // S5_SEGS
    perm = jnp.asarray(np.arange(tb)[None, :] == src[:, None], bf16)
    args = [x2, wu, perm, bd, cre, cim, lam, p['d'].reshape(1, -1).astype(f32), p['glu_w'].astype(bf16),
            p['glu_b'].reshape(1, -1).astype(f32)]
    full = lambda a: pl.BlockSpec(a.shape, lambda b, l: (0,) * a.ndim)
    return pl.pallas_call(
        _s5_kernel,
        grid=(bsz, n_l),
        in_specs=[pl.BlockSpec((tb, D_MODEL), lambda b, l: (b * n_l + l, 0))] + [full(a) for a in args[1:]],
        out_specs=pl.BlockSpec((1, tb, D_SSM), lambda b, l: (b, l, 0)),
        out_shape=jax.ShapeDtypeStruct((bsz, seq, D_SSM), f32),
        scratch_shapes=[pltpu.VMEM((n_state // LANES, tb, LANES), f32),
                        pltpu.VMEM((n_state // LANES, tb, LANES), f32),
                        pltpu.VMEM((2, n_state), f32)],
        compiler_params=pltpu.CompilerParams(
            dimension_semantics=("parallel", "arbitrary"), vmem_limit_bytes=VMEM_LIMIT),
    )(*args)


def _merge_kernel(x_ref, ya_ref, yb_ref, yc_ref, wg_ref, gb_ref, wa_ref, wb_ref, wc_ref, wo_ref,
                  lnw_ref, lnb_ref, o_ref):
    x = x_ref[...]
    xb = x.astype(bf16)
    merged = None
    for br, (y_ref, w_ref) in enumerate(((ya_ref, wa_ref), (yb_ref, wb_ref), (yc_ref, wc_ref))):
        cols = slice(br * D_MODEL, (br + 1) * D_MODEL)
        gate = jax.nn.sigmoid(_mm(xb, wg_ref[:, cols]) + gb_ref[:, cols])
        term = gate * _mm(y_ref[...].astype(bf16), w_ref[...])
        merged = term if merged is None else merged + term
    h = _mm(merged.astype(bf16), wo_ref[...])
    o_ref[...] = _layer_norm(DEEPNORM_ALPHA * x + h, lnw_ref[...], lnb_ref[...])


def _merge(x2, ya, yb, yc, wg, gb, wa, wb, wc, wo, lnw, lnb, tm=256):
    m = x2.shape[0]
    rowblk = lambda n: pl.BlockSpec((tm, n), lambda i: (i, 0))
    full = lambda a: pl.BlockSpec(a.shape, lambda i: (0,) * a.ndim)
    consts = [wg, gb, wa, wb, wc, wo, lnw, lnb]
    return pl.pallas_call(
        _merge_kernel,
        grid=(m // tm,),
        in_specs=[rowblk(D_MODEL), rowblk(D_RWKV), rowblk(D_MOBA), rowblk(D_SSM)]
        + [full(a) for a in consts],
        out_specs=rowblk(D_MODEL),
        out_shape=jax.ShapeDtypeStruct((m, D_MODEL), f32),
        compiler_params=pltpu.CompilerParams(
            dimension_semantics=("parallel",), vmem_limit_bytes=VMEM_LIMIT),
    )(x2, ya, yb, yc, *consts)


MOE_STEP_EXPERTS = 4


def _moe_kernel(x_ref, rwh_ref, rwl_ref, rb_ref, wg_ref, wu_ref, wd_ref, lnw_ref, lnb_ref, o_ref,
                xb_s, wt_s, acc_s):
    e = pl.program_id(1)
    tm = x_ref.shape[0]
    lane = lax.broadcasted_iota(jnp.int32, (tm, LANES), 1)
    lane_f = lane.astype(f32)

    @pl.when(e == 0)
    def _():
        x = x_ref[...]
        xh, xl = _split2(x)
        xb_s[...] = xh
        rwh = rwh_ref[...]
        logits = _mm(xh, rwh) + _mm(xh, rwl_ref[...]) + _mm(xl, rwh) + rb_ref[...]
        is_grp = (lane >= N_EXPERTS) & (lane < N_EXPERTS + N_EXPERT_GROUPS)
        gl = jnp.where(is_grp, logits, -jnp.inf)
        gmax = jnp.max(gl, axis=-1, keepdims=True)
        gidx = jnp.min(jnp.where(gl == gmax, lane_f, 1e9), axis=-1, keepdims=True) - float(N_EXPERTS)
        p_group = 1.0 / jnp.sum(jnp.where(is_grp, jnp.exp(gl - gmax), 0.0), axis=-1, keepdims=True)
        grp_of_lane = jnp.floor(lane_f * (1.0 / EXPERTS_PER_GROUP))
        in_grp = (lane < N_EXPERTS) & (grp_of_lane == gidx)
        el = jnp.where(in_grp, logits, -jnp.inf)
        m1 = jnp.max(el, axis=-1, keepdims=True)
        i1 = jnp.min(jnp.where(el == m1, lane_f, 1e9), axis=-1, keepdims=True)
        el2 = jnp.where(lane_f == i1, -jnp.inf, el)
        m2 = jnp.max(el2, axis=-1, keepdims=True)
        i2 = jnp.min(jnp.where(el2 == m2, lane_f, 1e9), axis=-1, keepdims=True)
        e2 = jnp.exp(m2 - m1)
        w1 = p_group / (1.0 + e2)
        w2 = p_group * e2 / (1.0 + e2)
        wt_s[...] = jnp.where(lane_f == i1, w1, 0.0) + jnp.where(lane_f == i2, w2, 0.0)
        acc_s[...] = jnp.zeros_like(acc_s)

    xb = xb_s[...]
    wt = wt_s[...]
    gact = [_mm(xb, wg_ref[j]) for j in range(MOE_STEP_EXPERTS)]
    up = [_mm(xb, wu_ref[j]) for j in range(MOE_STEP_EXPERTS)]
    part = None
    for j in range(MOE_STEP_EXPERTS):
        wcol = jnp.sum(jnp.where(lane == e * MOE_STEP_EXPERTS + j, wt, 0.0), axis=-1, keepdims=True)
        hid = gact[j] * jax.nn.sigmoid(gact[j]) * up[j] * wcol
        out = _mm(hid.astype(bf16), wd_ref[j])
        part = out if part is None else part + out
    acc_s[...] += part

    @pl.when(e == N_EXPERTS // MOE_STEP_EXPERTS - 1)
    def _():
        o_ref[...] = _layer_norm(DEEPNORM_ALPHA * x_ref[...] + acc_s[...], lnw_ref[...], lnb_ref[...])


def _moe(x2, rw, rb, wg, wu, wd, lnw, lnb, tm=1024):
    m = x2.shape[0]
    tm = min(tm, m)
    rwh, rwl = _split2(rw)
    full = lambda a: pl.BlockSpec(a.shape, lambda i, e: (0,) * a.ndim)
    return pl.pallas_call(
        _moe_kernel,
        grid=(m // tm, N_EXPERTS // MOE_STEP_EXPERTS),
        in_specs=[pl.BlockSpec((tm, D_MODEL), lambda i, e: (i, 0)),
                  full(rwh), full(rwl), full(rb),
                  pl.BlockSpec((MOE_STEP_EXPERTS, D_MODEL, D_EXPERT), lambda i, e: (e, 0, 0)),
                  pl.BlockSpec((MOE_STEP_EXPERTS, D_MODEL, D_EXPERT), lambda i, e: (e, 0, 0)),
                  pl.BlockSpec((MOE_STEP_EXPERTS, D_EXPERT, D_MODEL), lambda i, e: (e, 0, 0)),
                  full(lnw), full(lnb)],
        out_specs=pl.BlockSpec((tm, D_MODEL), lambda i, e: (i, 0)),
        out_shape=jax.ShapeDtypeStruct((m, D_MODEL), f32),
        scratch_shapes=[pltpu.VMEM((tm, D_MODEL), bf16), pltpu.VMEM((tm, LANES), f32),
                        pltpu.VMEM((tm, D_MODEL), f32)],
        compiler_params=pltpu.CompilerParams(
            dimension_semantics=("parallel", "arbitrary"), vmem_limit_bytes=VMEM_LIMIT),
    )(x2, rwh, rwl, rb, wg, wu, wd, lnw, lnb)


def _router_weights(router_group_w, router_group_b, router_expert_w, router_expert_b):
    pad = LANES - N_EXPERTS - N_EXPERT_GROUPS
    rw = jnp.concatenate([router_expert_w.astype(f32), router_group_w.astype(f32),
                          jnp.zeros((D_MODEL, pad), f32)], axis=1)
    rb = jnp.concatenate([router_expert_b.astype(f32), router_group_b.astype(f32),
                          jnp.zeros((pad,), f32)]).reshape(1, LANES)
    return rw, rb


def kernel(x, w_in, rwkv_mu, rwkv_w0, rwkv_w2, rwkv_a0, rwkv_a2, rwkv_g2, rwkv_k_k, rwkv_k_a, rwkv_r_k, rwkv_ln_w, rwkv_ln_b, ssm_a_re, ssm_a_im, ssm_b_re, ssm_b_im, ssm_c_re, ssm_c_im, ssm_d, ssm_log_dt, ssm_glu_w, ssm_glu_b, w_up_rwkv, w_up_moba, w_up_ssm, gate_b, w_out, ln1_w, ln1_b, router_group_w, router_group_b, router_expert_w, router_expert_b, expert_w_gate, expert_w_up, expert_w_down, ln2_w, ln2_b):
    bsz, seq, _ = x.shape
    x2 = x.reshape(bsz * seq, D_MODEL).astype(f32)
    row = lambda a: a.reshape(1, -1).astype(f32)
    for l in range(DEPTH):
        w = w_in[l].astype(bf16)
        qkv = _project(x2, w[:, OFF_MOBA:OFF_SSM])
        y_a = _rwkv(x2, w[:, :OFF_MOBA],
                    dict(mu=rwkv_mu[l], w0=rwkv_w0[l], w2=rwkv_w2[l], a0=rwkv_a0[l], a2=rwkv_a2[l],
                         g2=rwkv_g2[l], k_k=rwkv_k_k[l], k_a=rwkv_k_a[l], r_k=rwkv_r_k[l],
                         ln_w=rwkv_ln_w[l], ln_b=rwkv_ln_b[l]), bsz, seq)
        y_b = _moba(qkv.reshape(bsz, seq, MOBA_COLS))
        y_c = _s5(x2, w[:, OFF_SSM:OFF_GATE],
                  dict(a_re=ssm_a_re[l], a_im=ssm_a_im[l], b_re=ssm_b_re[l], b_im=ssm_b_im[l],
                       c_re=ssm_c_re[l], c_im=ssm_c_im[l], d=ssm_d[l], log_dt=ssm_log_dt[l],
                       glu_w=ssm_glu_w[l], glu_b=ssm_glu_b[l]), bsz, seq)
        x2 = _merge(x2, y_a.reshape(-1, D_RWKV), y_b.reshape(-1, D_MOBA), y_c.reshape(-1, D_SSM),
                    w[:, OFF_GATE:], row(gate_b[l]),
                    w_up_rwkv[l].astype(bf16), w_up_moba[l].astype(bf16), w_up_ssm[l].astype(bf16),
                    w_out[l].astype(bf16), row(ln1_w[l]), row(ln1_b[l]))
        rw, rb = _router_weights(router_group_w[l], router_group_b[l], router_expert_w[l],
                                 router_expert_b[l])
        x2 = _moe(x2, rw, rb, expert_w_gate[l].astype(bf16), expert_w_up[l].astype(bf16),
                  expert_w_down[l].astype(bf16), row(ln2_w[l]), row(ln2_b[l]))
    return x2.reshape(bsz, seq, D_MODEL)
```

```python
import functools

import jax
import jax.numpy as jnp
import numpy as np
from jax import lax
from jax.experimental import pallas as pl
from jax.experimental.pallas import tpu as pltpu

f32 = jnp.float32
bf16 = jnp.bfloat16

D_MODEL = 1024
DEPTH = 4
HEAD_DIM = 64
D_RWKV = 512
RWKV_HEADS = D_RWKV // HEAD_DIM
DECAY_LORA = 64
ICL_LORA = 64
GATE_LORA = 128
RWKV_GN_EPS = 64e-5
D_MOBA = 512
MOBA_HEADS = D_MOBA // HEAD_DIM
MOBA_BLOCK = 256
MOBA_TOPK = 3
D_SSM = 512
SSM_GROUP = 16
SSM_GROUPS = D_SSM // SSM_GROUP
SSM_STATE = 64
N_BRANCHES = 3
N_EXPERT_GROUPS = 4
EXPERTS_PER_GROUP = 8
N_EXPERTS = N_EXPERT_GROUPS * EXPERTS_PER_GROUP
D_EXPERT = D_MODEL // 4
LN_EPS = 1e-5
DEEPNORM_ALPHA = (2 * DEPTH) ** 0.25
NEG_INF = -1e30
RWKV_COLS = 3 * D_RWKV + DECAY_LORA + ICL_LORA + GATE_LORA
MOBA_COLS = 3 * D_MOBA
OFF_MOBA = RWKV_COLS
OFF_SSM = OFF_MOBA + MOBA_COLS
OFF_GATE = OFF_SSM + D_SSM

LANES = 128
WKV_CHUNK = 64
WKV_PAIR = 2
VMEM_LIMIT = 48 * 1024 * 1024


def _nt(a, b):
    return lax.dot_general(a, b, (((1,), (1,)), ((), ())), preferred_element_type=f32)


def _tn(a, b):
    return lax.dot_general(a, b, (((0,), (0,)), ((), ())), preferred_element_type=f32)


def _mm(a, b):
    return jnp.dot(a, b, preferred_element_type=f32)


def _split2(x):
    hi = x.astype(bf16)
    lo = (x - hi.astype(f32)).astype(bf16)
    return hi, lo


def _layer_norm(y, w, b):
    mu = jnp.mean(y, axis=-1, keepdims=True)
    d = y - mu
    var = jnp.mean(d * d, axis=-1, keepdims=True)
    return d * lax.rsqrt(var + LN_EPS) * w + b


def _proj_kernel(x_ref, w_ref, o_ref):
    o_ref[...] = _mm(x_ref[...].astype(bf16), w_ref[...])


def _project(x2, w, tm=512):
    m, k = x2.shape
    n = w.shape[1]
    return pl.pallas_call(
        _proj_kernel,
        grid=(m // tm,),
        in_specs=[pl.BlockSpec((tm, k), lambda i: (i, 0)), pl.BlockSpec((k, n), lambda i: (0, 0))],
        out_specs=pl.BlockSpec((tm, n), lambda i: (i, 0)),
        out_shape=jax.ShapeDtypeStruct((m, n), f32),
        compiler_params=pltpu.CompilerParams(
            dimension_semantics=("parallel",), vmem_limit_bytes=VMEM_LIMIT),
    )(x2, w)


def _rwkv_kernel(x_ref, wz_ref, mu_ref, w0_ref, w2_ref, a0_ref, a2_ref, g2_ref, kk_ref, ka_ref, rk_ref,
                 lnw_ref, lnb_ref, seg_ref, tri_ref, o_ref,
                 zlast_s, state_s, r_s, k_s, v_s, a_s, b_s, lw_s, g_s, bon_s, y_s,
                 rt_s, bh_s, kh_s, tw_s, arb_s, pc_s):
    tb = x_ref.shape[0]
    n_chunks = tb // WKV_CHUNK
    c64 = WKV_CHUNK

    @pl.when(pl.program_id(1) == 0)
    def _():
        zlast_s[...] = jnp.zeros_like(zlast_s)
        state_s[...] = jnp.zeros_like(state_s)

    seg = seg_ref[...]

    def segsum(x):
        hi, lo = _split2(x)
        return _mm(hi, seg) + _mm(lo, seg)

    z = _mm(x_ref[...].astype(bf16), wz_ref[...])
    row = lax.broadcasted_iota(jnp.int32, z.shape, 0)
    z_prev = jnp.where(row == 0, zlast_s[...], pltpu.roll(z, 1, axis=0))
    zlast_s[...] = z[tb - 1:tb, :]
    zz = z + (z_prev - z) * mu_ref[...]
    r = zz[:, 0:D_RWKV]
    k = zz[:, D_RWKV:2 * D_RWKV]
    v = zz[:, 2 * D_RWKV:3 * D_RWKV]
    o1 = 3 * D_RWKV
    xw = zz[:, o1:o1 + DECAY_LORA]
    xa = zz[:, o1 + DECAY_LORA:o1 + DECAY_LORA + ICL_LORA]
    xg = zz[:, o1 + DECAY_LORA + ICL_LORA:]
    wl = w0_ref[...] + _mm(jnp.tanh(xw).astype(bf16), w2_ref[...])
    softplus = jnp.maximum(-wl, 0.0) + jnp.log(1.0 + jnp.exp(-jnp.abs(wl)))
    lw_s[...] = -jnp.exp(-softplus - 0.5)
    a_icl = jax.nn.sigmoid(a0_ref[...] + _mm(xa.astype(bf16), a2_ref[...]))
    g_s[...] = _mm(jax.nn.sigmoid(xg).astype(bf16), g2_ref[...])
    kk = k * kk_ref[...]
    kk = kk * lax.rsqrt(jnp.maximum(segsum(kk * kk), 1e-24))
    k2 = k * (1.0 + (a_icl - 1.0) * ka_ref[...])
    r_s[...] = r
    k_s[...] = k2
    v_s[...] = v
    a_s[...] = -kk
    b_s[...] = kk * a_icl
    bon_s[...] = segsum(r * k2 * rk_ref[...])

    ri = lax.broadcasted_iota(jnp.int32, (c64, c64), 0)
    ci = lax.broadcasted_iota(jnp.int32, (c64, c64), 1)
    strict = ri > ci
    incl = ri >= ci
    eye = jnp.where(ri == ci, 1.0, 0.0).astype(f32)
    tri = tri_ref[...]

    heads = range(RWKV_HEADS)
    sls = [slice(h * HEAD_DIM, (h + 1) * HEAD_DIM) for h in heads]

    def chunk_terms(c):
        rows = pl.ds(pl.multiple_of(c * c64, c64), c64)
        lwc = lw_s[rows, :]
        lhi, llo = _split2(lwc)
        cl = _mm(tri, lhi) + _mm(tri, llo)
        cl_last = cl[c64 - 1:c64, :]
        e_neg = jnp.exp(-cl)
        e_tot = jnp.exp(cl_last - cl)
        ac = a_s[rows, :]
        bc = b_s[rows, :]
        kc = k_s[rows, :]
        r_t = r_s[rows, :] * jnp.exp(cl)
        rt_s[rows, :] = r_t
        bh_s[rows, :] = bc * e_tot
        kh_s[rows, :] = kc * e_tot
        pc_s[pl.ds(c, 1), :] = jnp.exp(cl_last)
        return rows, ac * jnp.exp(cl - lwc), r_t, bc * e_neg, kc * e_neg, v_s[rows, :]

    def intra_body(c2, carry):
        terms = [chunk_terms(c2 * WKV_PAIR + j) for j in range(WKV_PAIR)]
        probs = [(j, h) for j in range(WKV_PAIR) for h in heads]
        cat = lambda j, p, q, h: jnp.concatenate([terms[j][p][:, sls[h]], terms[j][q][:, sls[h]]],
                                                 axis=0).astype(bf16)
        gm = [_nt(cat(j, 1, 2, h), cat(j, 3, 4, h)) for j, h in probs]
        a_ab = [jnp.where(strict, g[:c64, :c64], 0.0) for g in gm]
        a_ak = [jnp.where(strict, g[:c64, c64:], 0.0).astype(bf16) for g in gm]
        a_rb = [jnp.where(incl, g[c64:, :c64], 0.0).astype(bf16) for g in gm]
        a_rk = [jnp.where(incl, g[c64:, c64:], 0.0).astype(bf16) for g in gm]
        vb = [terms[j][5][:, sls[h]].astype(bf16) for j, h in probs]
        akv = [_mm(a, v) for a, v in zip(a_ak, vb)]
        yloc = [_mm(a, v) for a, v in zip(a_rk, vb)]
        xs = a_ab
        t_inv = [eye + x for x in xs]
        for _ in range(5):
            xb = [x.astype(bf16) for x in xs]
            xs = [_mm(x, x) for x in xb]
            t_inv = [t + _mm(t.astype(bf16), x.astype(bf16)) for t, x in zip(t_inv, xs)]
        rhs = [jnp.concatenate([terms[j][1][:, sls[h]], akv[n]], axis=1).astype(bf16)
               for n, (j, h) in enumerate(probs)]
        tw = [_mm(t.astype(bf16), r) for t, r in zip(t_inv, rhs)]
        for j in range(WKV_PAIR):
            mine = slice(j * RWKV_HEADS, (j + 1) * RWKV_HEADS)
            rows = terms[j][0]
            tw_s[rows, :] = jnp.concatenate(tw[mine], axis=1)
            arb_s[rows, :] = jnp.concatenate(a_rb[mine], axis=1)
            y_s[rows, :] = jnp.concatenate(yloc[mine], axis=1)
        return carry

    lax.fori_loop(0, n_chunks // WKV_PAIR, intra_body, 0)

    def state_body(c, carry):
        rows = pl.ds(pl.multiple_of(c * c64, c64), c64)
        twc = tw_s[rows, :]
        arbc = arb_s[rows, :]
        r_t = rt_s[rows, :]
        b_h = bh_s[rows, :]
        k_h = kh_s[rows, :]
        vc = v_s[rows, :]
        yl = y_s[rows, :]
        p_c = pc_s[pl.ds(c, 1), :]
        s_old = [state_s[h] for h in heads]
        wr = [jnp.concatenate([twc[:, h * LANES:h * LANES + c64], r_t[:, sls[h]]], axis=0).astype(bf16)
              for h in heads]
        wrs = [_nt(wr[h], s_old[h].astype(bf16)) for h in heads]
        u = [wrs[h][:c64] + twc[:, h * LANES + c64:(h + 1) * LANES] for h in heads]
        ys = [wrs[h][c64:] + _mm(arbc[:, sls[h]], u[h].astype(bf16)) + yl[:, sls[h]] for h in heads]
        uv = [jnp.concatenate([u[h], vc[:, sls[h]]], axis=0).astype(bf16) for h in heads]
        bkh = [jnp.concatenate([b_h[:, sls[h]], k_h[:, sls[h]]], axis=0).astype(bf16) for h in heads]
        for h in heads:
            state_s[h] = s_old[h] * p_c[:, sls[h]] + _tn(uv[h], bkh[h])
        y_s[rows, :] = jnp.concatenate(ys, axis=1)
        return carry

    lax.fori_loop(0, n_chunks, state_body, 0)

    y = y_s[...]
    mean = segsum(y) * (1.0 / HEAD_DIM)
    d = y - mean
    var = segsum(d * d) * (1.0 / HEAD_DIM)
    yn = d * lax.rsqrt(var + RWKV_GN_EPS) * lnw_ref[...] + lnb_ref[...]
    o_ref[0] = (yn + bon_s[...] * v_s[...]) * g_s[...]


def _rwkv(x2, wz, p, bsz, seq, tb=512):
    tb = min(tb, seq)
    n_l = seq // tb
    seg = (np.arange(D_RWKV)[:, None] // HEAD_DIM == np.arange(D_RWKV)[None, :] // HEAD_DIM)
    seg = jnp.asarray(seg, bf16)
    tri = jnp.asarray(np.tril(np.ones((WKV_CHUNK, WKV_CHUNK))), bf16)
    row = lambda a: a.reshape(1, -1).astype(f32)
    args = [x2, wz, row(p['mu']), row(p['w0']), p['w2'].astype(bf16), row(p['a0']), p['a2'].astype(bf16),
            p['g2'].astype(bf16), row(p['k_k']), row(p['k_a']), row(p['r_k']), row(p['ln_w']),
            row(p['ln_b']), seg, tri]
    full = lambda a: pl.BlockSpec(a.shape, lambda b, l: (0,) * a.ndim)
    in_specs = [pl.BlockSpec((tb, D_MODEL), lambda b, l: (b * n_l + l, 0))] + [full(a) for a in args[1:]]
    big = lambda: pltpu.VMEM((tb, D_RWKV), f32)
    return pl.pallas_call(
        _rwkv_kernel,
        grid=(bsz, n_l),
        in_specs=in_specs,
        out_specs=pl.BlockSpec((1, tb, D_RWKV), lambda b, l: (b, l, 0)),
        out_shape=jax.ShapeDtypeStruct((bsz, seq, D_RWKV), f32),
        scratch_shapes=[pltpu.VMEM((1, RWKV_COLS), f32),
                        pltpu.VMEM((RWKV_HEADS, HEAD_DIM, HEAD_DIM), f32),
                        big(), big(), big(), big(), big(), big(), big(), big(), big(),
                        big(), big(), big(), pltpu.VMEM((tb, RWKV_HEADS * LANES), f32),
                        pltpu.VMEM((tb, D_RWKV), bf16), pltpu.VMEM((tb // WKV_CHUNK, D_RWKV), f32)],
        compiler_params=pltpu.CompilerParams(
            dimension_semantics=("parallel", "arbitrary"), vmem_limit_bytes=VMEM_LIMIT),
    )(*args)


MOBA_AUG_POS = 0
MOBA_AUG_BLK = 6
LOG2E = 1.4426950408889634


MOBA_STEP_HEADS = 4


def _moba_kernel(q_ref, k_ref, v_ref, tmpl_ref, o_ref, kaug_s, vaug_s, kmean_s):
    blk = MOBA_BLOCK
    half = HEAD_DIM
    seq = k_ref.shape[1]
    n_blk = seq // blk
    qi = pl.program_id(2)
    heads = range(MOBA_STEP_HEADS)
    tile = [slice((hh // 2) * LANES, (hh // 2 + 1) * LANES) for hh in heads]
    aug0 = [half if hh % 2 == 0 else 0 for hh in heads]

    @pl.when(qi == 0)
    def _():
        lane_l = lax.broadcasted_iota(jnp.int32, (seq, LANES), 1)
        for hh in heads:
            kb = k_ref[0, :, tile[hh]].astype(bf16)
            vb = v_ref[0, :, tile[hh]].astype(bf16)
            in_head = (lane_l < half) if hh % 2 == 0 else (lane_l >= half)
            kaug_s[hh] = jnp.where(in_head, kb, tmpl_ref[0, hh])
            ones_lane = jnp.where(lane_l == aug0[hh], 1.0, 0.0).astype(bf16)
            vaug_s[hh] = jnp.where(in_head, vb, ones_lane)
        for n in range(n_blk):
            km = jnp.mean(k_ref[0, n * blk:(n + 1) * blk, :], axis=0, keepdims=True)
            for tt in range(MOBA_STEP_HEADS // 2):
                kmean_s[tt, n:n + 1, :] = km[:, tt * LANES:(tt + 1) * LANES]

    ri = lax.broadcasted_iota(jnp.int32, (blk, blk), 0)
    ci = lax.broadcasted_iota(jnp.int32, (blk, blk), 1)
    causal = ri >= ci
    lane = lax.broadcasted_iota(jnp.int32, (blk, LANES), 1)
    lane_f = lane.astype(f32)
    qf = [q_ref[0, :, tile[hh]] * (HEAD_DIM ** -0.5 * LOG2E) for hh in heads]
    in_head = [(lane < half) if hh % 2 == 0 else (lane >= half) for hh in heads]
    blk_lane = [lane - (aug0[hh] + MOBA_AUG_BLK) for hh in heads]

    qs = [_split2(jnp.where(in_head[hh], qf[hh], 0.0)) for hh in heads]
    kms = [_split2(kmean_s[tt]) for tt in range(MOBA_STEP_HEADS // 2)]
    gate = [_nt(kms[hh // 2][0], qs[hh][0]) + _nt(kms[hh // 2][1], qs[hh][0])
            + _nt(kms[hh // 2][0], qs[hh][1]) for hh in heads]
    brow = lax.broadcasted_iota(jnp.int32, (n_blk, blk), 0)
    brow_f = brow.astype(f32)
    g = [jnp.where(brow < qi, x, -jnp.inf) for x in gate]
    picked = [jnp.zeros((n_blk, blk), f32) for _ in heads]
    for _ in range(MOBA_TOPK):
        mx = [jnp.max(x, axis=0, keepdims=True) for x in g]
        idx = [jnp.min(jnp.where(g[hh] == mx[hh], brow_f, 1e9), axis=0, keepdims=True) for hh in heads]
        hit = [brow_f == i for i in idx]
        picked = [jnp.where(hit[hh], 1.0, picked[hh]) for hh in heads]
        g = [jnp.where(hit[hh], -jnp.inf, g[hh]) for hh in heads]
    sel_t = [jnp.where(brow < qi, p, 0.0).astype(bf16) for p in picked]
    lane_b = lax.broadcasted_iota(jnp.int32, (n_blk, LANES), 1)
    brow_b = lax.broadcasted_iota(jnp.int32, (n_blk, LANES), 0)
    q_aug, q_own = [], []
    for hh in heads:
        place = jnp.where(lane_b == brow_b + (aug0[hh] + MOBA_AUG_BLK), 1.0, 0.0).astype(bf16)
        sel = _tn(sel_t[hh], place) > 0.5
        is_pos = (lane >= aug0[hh] + MOBA_AUG_POS) & (lane < aug0[hh] + MOBA_AUG_POS + 3)
        is_blk = (blk_lane[hh] >= 0) & (blk_lane[hh] < n_blk)
        own = jnp.where(is_pos, 1.0, 0.0)
        aug = jnp.where(is_blk, jnp.where(sel, 0.0, NEG_INF), own)
        q_aug.append(jnp.where(in_head[hh], qf[hh], aug).astype(bf16))
        q_own.append(jnp.where(in_head[hh], qf[hh], own).astype(bf16))

    start = pl.multiple_of(qi * blk, blk)
    s0 = [_nt(q_own[hh], kaug_s[hh, pl.ds(start, blk), :]) for hh in heads]
    s0 = [jnp.where(causal, s, NEG_INF) for s in s0]
    m0 = [jnp.max(s, axis=-1, keepdims=True) for s in s0]
    p0 = [jnp.exp2(s - m).astype(bf16) for s, m in zip(s0, m0)]
    a0 = [_mm(p0[hh], vaug_s[hh, pl.ds(start, blk), :]) for hh in heads]

    def make_body(width, base):
        def body(t, carry):
            ms, accs = carry
            rows = pl.ds(pl.multiple_of(base + t * (width * blk), 2 * blk), width * blk)
            s = [_nt(q_aug[hh], kaug_s[hh, rows, :]) for hh in heads]
            m_new = [jnp.maximum(ms[hh], jnp.max(s[hh], axis=-1, keepdims=True)) for hh in heads]
            p = [jnp.exp2(s[hh] - m_new[hh]).astype(bf16) for hh in heads]
            pv = [_mm(p[hh], vaug_s[hh, rows, :]) for hh in heads]
            accs = [accs[hh] * jnp.exp2(ms[hh] - m_new[hh]) + pv[hh] for hh in heads]
            return (tuple(m_new), tuple(accs))
        return body

    n_pairs = (qi + 1) // 2
    n_quads = n_pairs // 2
    carry = lax.fori_loop(0, n_quads, make_body(4, 0), (tuple(m0), tuple(a0)))
    _, accs = lax.fori_loop(0, n_pairs - 2 * n_quads, make_body(2, n_quads * (4 * blk)), carry)
    out = [accs[hh] / accs[hh][:, aug0[hh]:aug0[hh] + 1] for hh in heads]
    for tt in range(MOBA_STEP_HEADS // 2):
        o_ref[0, :, tt * LANES:(tt + 1) * LANES] = jnp.where(lane < half, out[2 * tt], out[2 * tt + 1])


def _moba_template(seq):
    pos = np.arange(seq, dtype=np.float64)
    tmpl = np.zeros((MOBA_HEADS, seq, LANES), np.float32)
    for h in range(MOBA_HEADS):
        a0 = HEAD_DIM if h % 2 == 0 else 0
        slope = 2.0 ** (-8.0 * (h + 1) / MOBA_HEADS)
        rest = slope * LOG2E * pos
        for c in range(3):
            part = rest.astype(np.float32).astype(bf16).astype(np.float64)
            tmpl[h, :, a0 + MOBA_AUG_POS + c] = part
            rest = rest - part
        tmpl[h, np.arange(seq), a0 + MOBA_AUG_BLK + np.arange(seq) // MOBA_BLOCK] = 1.0
    return jnp.asarray(tmpl.reshape(MOBA_HEADS // MOBA_STEP_HEADS, MOBA_STEP_HEADS, seq, LANES), bf16)


def _moba(qkv3):
    bsz, seq, _ = qkv3.shape
    blk = MOBA_BLOCK
    sh = MOBA_STEP_HEADS
    n_grp = MOBA_HEADS // sh
    wid = sh * HEAD_DIM
    assert seq % (2 * blk) == 0 and seq // blk <= HEAD_DIM - MOBA_AUG_BLK
    return pl.pallas_call(
        _moba_kernel,
        grid=(bsz, n_grp, seq // blk),
        in_specs=[pl.BlockSpec((1, blk, wid), lambda b, p, i: (b, i, p)),
                  pl.BlockSpec((1, seq, wid), lambda b, p, i: (b, 0, n_grp + p)),
                  pl.BlockSpec((1, seq, wid), lambda b, p, i: (b, 0, 2 * n_grp + p)),
                  pl.BlockSpec((1, sh, seq, LANES), lambda b, p, i: (p, 0, 0, 0))],
        out_specs=pl.BlockSpec((1, blk, wid), lambda b, p, i: (b, i, p)),
        out_shape=jax.ShapeDtypeStruct((bsz, seq, D_MOBA), f32),
        scratch_shapes=[pltpu.VMEM((sh, seq, LANES), bf16), pltpu.VMEM((sh, seq, LANES), bf16),
                        pltpu.VMEM((sh // 2, seq // blk, LANES), f32)],
        compiler_params=pltpu.CompilerParams(
            dimension_semantics=("parallel", "parallel", "arbitrary"), vmem_limit_bytes=VMEM_LIMIT),
    )(qkv3, qkv3, qkv3, _moba_template(seq))


S5_SUPER = 4
S5_SEGS = 8
S5_COLS = 512


def _s5_kernel(x_ref, wu_ref, perm_ref, bd_ref, cre_ref, cim_ref, lam_ref, d_ref, gw_ref, gb_ref, o_ref,
               hr_s, hi_s, carry_s):
    tb = x_ref.shape[0]
    seg = tb // S5_SEGS
    n_state = lam_ref.shape[1]
    sgw = n_state // S5_SUPER
    uw = D_SSM // S5_SUPER

    @pl.when(pl.program_id(1) == 0)
    def _():
        carry_s[...] = jnp.zeros_like(carry_s)

    u = _mm(x_ref[...].astype(bf16), wu_ref[...])
    perm = perm_ref[...]
    ub = _mm(perm, u.astype(bf16)).astype(bf16)
    for sg in range(S5_SUPER):
        bu = _mm(ub[:, sg * uw:(sg + 1) * uw], bd_ref[sg])
        for j in range(sgw // LANES):
            hr_s[sg * (sgw // LANES) + j] = bu[:, j * LANES:(j + 1) * LANES]
            hi_s[sg * (sgw // LANES) + j] = bu[:, sgw + j * LANES:sgw + (j + 1) * LANES]

    tpc = S5_COLS // LANES

    def load(ref, cc, idx):
        return jnp.concatenate([ref[cc * tpc + j, idx, :] for j in range(tpc)], axis=1)

    def store(ref, cc, idx, val):
        for j in range(tpc):
            ref[cc * tpc + j, idx, :] = val[:, j * LANES:(j + 1) * LANES]

    for cc in range(n_state // S5_COLS):
        cols = slice(cc * S5_COLS, (cc + 1) * S5_COLS)
        lr = lam_ref[0:1, cols]
        li = lam_ref[1:2, cols]

        def local(r, carry, cc=cc, lr=lr, li=li):
            hr, hi = carry
            idx = pl.ds(pl.multiple_of(r * S5_SEGS, S5_SEGS), S5_SEGS)
            nr = lr * hr - li * hi + load(hr_s, cc, idx)
            ni = lr * hi + li * hr + load(hi_s, cc, idx)
            store(hr_s, cc, idx, nr)
            store(hi_s, cc, idx, ni)
            return nr, ni

        zero = jnp.zeros((S5_SEGS, S5_COLS), f32)
        er, ei = lax.fori_loop(0, seg, local, (zero, zero), unroll=4)
        pr, pi = lr, li
        for _ in range(seg.bit_length() - 1):
            pr, pi = pr * pr - pi * pi, 2.0 * pr * pi
        cr = carry_s[0:1, cols]
        ci = carry_s[1:2, cols]
        in_r, in_i = [], []
        for s in range(S5_SEGS):
            in_r.append(cr)
            in_i.append(ci)
            cr, ci = pr * cr - pi * ci + er[s:s + 1, :], pr * ci + pi * cr + ei[s:s + 1, :]
        carry_s[0:1, cols] = cr
        carry_s[1:2, cols] = ci

        def fix(r, carry, cc=cc, lr=lr, li=li):
            qr, qi = carry
            qr, qi = lr * qr - li * qi, lr * qi + li * qr
            idx = pl.ds(pl.multiple_of(r * S5_SEGS, S5_SEGS), S5_SEGS)
            store(hr_s, cc, idx, load(hr_s, cc, idx) + qr)
            store(hi_s, cc, idx, load(hi_s, cc, idx) + qi)
            return qr, qi

        lax.fori_loop(0, seg, fix, (jnp.concatenate(in_r, axis=0), jnp.concatenate(in_i, axis=0)),
                      unroll=4)

    slab = lambda ref, sg: jnp.concatenate(
        [ref[sg * (sgw // LANES) + j] for j in range(sgw // LANES)], axis=1).astype(bf16)
    ys = [_mm(slab(hr_s, sg), cre_ref[sg]) + _mm(slab(hi_s, sg), cim_ref[sg]) for sg in range(S5_SUPER)]
    yh, yl = _split2(jnp.concatenate(ys, axis=1))
    y = _tn(perm, yh) + _tn(perm, yl) + d_ref[...] * u
    y = 0.5 * y * (1.0 + jnp.tanh(0.7978845608028654 * (y + 0.044715 * (y * y * y))))
    zg = _mm(y.astype(bf16), gw_ref[...]) + gb_ref[...]
    o_ref[0] = zg[:, :D_SSM] * jax.nn.sigmoid(zg[:, D_SSM:])


def _s5_params(p):
    a_re = p['a_re'].astype(f32)
    a_im = p['a_im'].astype(f32)
    dt = jnp.exp(p['log_dt'].astype(f32))[:, None]
    mag = jnp.exp(a_re * dt)
    lam_re = mag * jnp.cos(a_im * dt)
    lam_im = mag * jnp.sin(a_im * dt)
    den = a_re * a_re + a_im * a_im
    nr = lam_re - 1.0
    coef_re = (nr * a_re + lam_im * a_im) / den
    coef_im = (lam_im * a_re - nr * a_im) / den
    b_re = p['b_re'].astype(f32)
    b_im = p['b_im'].astype(f32)
    bb_re = coef_re[..., None] * b_re - coef_im[..., None] * b_im
    bb_im = coef_re[..., None] * b_im + coef_im[..., None] * b_re
    gps = SSM_GROUPS // S5_SUPER
    eye = jnp.eye(gps, dtype=f32)
    n_state = SSM_GROUPS * SSM_STATE
    sgw = n_state // S5_SUPER
    uw = D_SSM // S5_SUPER

    def in_map(bb):
        bb = bb.reshape(S5_SUPER, gps, SSM_STATE, SSM_GROUP)
        return jnp.einsum('sgph,gk->sghkp', bb, eye).reshape(S5_SUPER, uw, sgw)

    def out_map(c):
        c = c.reshape(S5_SUPER, gps, SSM_GROUP, SSM_STATE)
        return jnp.einsum('sghp,gk->sgpkh', c, eye).reshape(S5_SUPER, sgw, uw)

    bd = jnp.concatenate([in_map(bb_re), in_map(bb_im)], axis=2)
    cre = out_map(p['c_re'].astype(f32))
    cim = -out_map(p['c_im'].astype(f32))
    lam = jnp.stack([lam_re.reshape(n_state), lam_im.reshape(n_state)], axis=0)
    return bd.astype(bf16), cre.astype(bf16), cim.astype(bf16), lam


def _s5(x2, wu, p, bsz, seq, tb=512):
    tb = min(tb, seq)
    n_l = seq // tb
    seg = tb // S5_SEGS
    assert tb % S5_SEGS == 0 and seg & (seg - 1) == 0
    n_state = SSM_GROUPS * SSM_STATE
    bd, cre, cim, lam = _s5_params(p)
    src = (np.arange(tb) % S5_SEGS) * seg + np.arange(tb) // S5_SEGS
    perm = jnp.asarray(np.arange(tb)[None, :] == src[:, None], bf16)
    args = [x2, wu, perm, bd, cre, cim, lam, p['d'].reshape(1, -1).astype(f32), p['glu_w'].astype(bf16),
            p['glu_b'].reshape(1, -1).astype(f32)]
    full = lambda a: pl.BlockSpec(a.shape, lambda b, l: (0,) * a.ndim)
    return pl.pallas_call(
        _s5_kernel,
        grid=(bsz, n_l),
        in_specs=[pl.BlockSpec((tb, D_MODEL), lambda b, l: (b * n_l + l, 0))] + [full(a) for a in args[1:]],
        out_specs=pl.BlockSpec((1, tb, D_SSM), lambda b, l: (b, l, 0)),
        out_shape=jax.ShapeDtypeStruct((bsz, seq, D_SSM), f32),
        scratch_shapes=[pltpu.VMEM((n_state // LANES, tb, LANES), f32),
                        pltpu.VMEM((n_state // LANES, tb, LANES), f32),
                        pltpu.VMEM((2, n_state), f32)],
        compiler_params=pltpu.CompilerParams(
            dimension_semantics=("parallel", "arbitrary"), vmem_limit_bytes=VMEM_LIMIT),
    )(*args)


def _merge_kernel(x_ref, ya_ref, yb_ref, yc_ref, wg_ref, gb_ref, wa_ref, wb_ref, wc_ref, wo_ref,
                  lnw_ref, lnb_ref, o_ref):
    x = x_ref[...]
    xb = x.astype(bf16)
    merged = None
    for br, (y_ref, w_ref) in enumerate(((ya_ref, wa_ref), (yb_ref, wb_ref), (yc_ref, wc_ref))):
        cols = slice(br * D_MODEL, (br + 1) * D_MODEL)
        gate = jax.nn.sigmoid(_mm(xb, wg_ref[:, cols]) + gb_ref[:, cols])
        term = gate * _mm(y_ref[...].astype(bf16), w_ref[...])
        merged = term if merged is None else merged + term
    h = _mm(merged.astype(bf16), wo_ref[...])
    o_ref[...] = _layer_norm(DEEPNORM_ALPHA * x + h, lnw_ref[...], lnb_ref[...])


def _merge(x2, ya, yb, yc, wg, gb, wa, wb, wc, wo, lnw, lnb, tm=256):
    m = x2.shape[0]
    rowblk = lambda n: pl.BlockSpec((tm, n), lambda i: (i, 0))
    full = lambda a: pl.BlockSpec(a.shape, lambda i: (0,) * a.ndim)
    consts = [wg, gb, wa, wb, wc, wo, lnw, lnb]
    return pl.pallas_call(
        _merge_kernel,
        grid=(m // tm,),
        in_specs=[rowblk(D_MODEL), rowblk(D_RWKV), rowblk(D_MOBA), rowblk(D_SSM)]
        + [full(a) for a in consts],
        out_specs=rowblk(D_MODEL),
        out_shape=jax.ShapeDtypeStruct((m, D_MODEL), f32),
        compiler_params=pltpu.CompilerParams(
            dimension_semantics=("parallel",), vmem_limit_bytes=VMEM_LIMIT),
    )(x2, ya, yb, yc, *consts)


MOE_STEP_EXPERTS = 4
MOE_CHUNK = 256
MOE_STEPS = N_EXPERTS // MOE_STEP_EXPERTS
STEPS_PER_GROUP = EXPERTS_PER_GROUP // MOE_STEP_EXPERTS


def _moe_kernel(x_ref, rwh_ref, rwl_ref, rb_ref, wg_ref, wu_ref, wd_ref, lnw_ref, lnb_ref, o_ref,
                xs_s, outs_s, wts_s, pos_s, posrow_s, meta_s):
    s = pl.program_id(1)
    tm = x_ref.shape[0]
    ck = MOE_CHUNK
    lane = lax.broadcasted_iota(jnp.int32, (tm, LANES), 1)
    lane_f = lane.astype(f32)

    @pl.when(s == 0)
    def _():
        xh, xl = _split2(x_ref[...])
        rwh = rwh_ref[...]
        logits = _mm(xh, rwh) + _mm(xh, rwl_ref[...]) + _mm(xl, rwh) + rb_ref[...]
        is_grp = (lane >= N_EXPERTS) & (lane < N_EXPERTS + N_EXPERT_GROUPS)
        gl = jnp.where(is_grp, logits, -jnp.inf)
        gmax = jnp.max(gl, axis=-1, keepdims=True)
        gidx = jnp.min(jnp.where(gl == gmax, lane_f, 1e9), axis=-1, keepdims=True) - float(N_EXPERTS)
        p_group = 1.0 / jnp.sum(jnp.where(is_grp, jnp.exp(gl - gmax), 0.0), axis=-1, keepdims=True)
        grp_of_lane = jnp.floor(lane_f * (1.0 / EXPERTS_PER_GROUP))
        in_grp = (lane < N_EXPERTS) & (grp_of_lane == gidx)
        el = jnp.where(in_grp, logits, -jnp.inf)
        m1 = jnp.max(el, axis=-1, keepdims=True)
        i1 = jnp.min(jnp.where(el == m1, lane_f, 1e9), axis=-1, keepdims=True)
        el2 = jnp.where(lane_f == i1, -jnp.inf, el)
        m2 = jnp.max(el2, axis=-1, keepdims=True)
        i2 = jnp.min(jnp.where(el2 == m2, lane_f, 1e9), axis=-1, keepdims=True)
        e2 = jnp.exp(m2 - m1)
        w1 = p_group / (1.0 + e2)
        w2 = p_group * e2 / (1.0 + e2)
        wt = jnp.where(lane_f == i1, w1, 0.0) + jnp.where(lane_f == i2, w2, 0.0)

        ohg = lane_f == gidx
        ri = lax.broadcasted_iota(jnp.int32, (tm, tm), 0)
        ci = lax.broadcasted_iota(jnp.int32, (tm, tm), 1)
        tri = jnp.where(ri >= ci, 1.0, 0.0).astype(bf16)
        cg = _mm(tri, jnp.where(ohg, 1.0, 0.0).astype(bf16))
        cnt = cg[tm - 1:tm, :]
        padded = jnp.ceil(cnt * (1.0 / ck)) * ck
        ui = lax.broadcasted_iota(jnp.int32, (LANES, LANES), 0)
        uj = lax.broadcasted_iota(jnp.int32, (LANES, LANES), 1)
        upper = jnp.where(ui < uj, 1.0, 0.0).astype(bf16)
        off = _mm(jnp.broadcast_to(padded, (8, LANES)).astype(bf16), upper)[0:1, :]
        rank = jnp.sum(jnp.where(ohg, cg - 1.0, 0.0), axis=-1, keepdims=True)
        pos = jnp.sum(jnp.where(ohg, off, 0.0), axis=-1, keepdims=True) + rank
        pos_s[...] = jnp.broadcast_to(pos, (tm, LANES))
        pa = jnp.floor(pos * (1.0 / 64.0))
        pb = pos - 64.0 * pa
        cols = jnp.where(lane == 0, pa, jnp.where(lane == 1, pb, 0.0)).astype(bf16)
        r8 = lax.broadcasted_iota(jnp.int32, (8, LANES), 0)
        l8 = lax.broadcasted_iota(jnp.int32, (8, LANES), 1)
        sel8 = jnp.where((r8 == 0) & (l8 == 0), 64.0, jnp.where((r8 == 0) & (l8 == 1), 1.0, 0.0))
        posrow_s[...] = _nt(sel8.astype(bf16), cols)
        off_i = off.astype(jnp.int32)
        nck_i = (padded * (1.0 / ck)).astype(jnp.int32)
        total = 0
        for g in range(N_EXPERT_GROUPS):
            meta_s[g] = off_i[0, g]
            meta_s[N_EXPERT_GROUPS + g] = nck_i[0, g]
            total = total + nck_i[0, g]
        meta_s[2 * N_EXPERT_GROUPS] = total

        wth, wtl = _split2(wt)
        rows_f = lax.broadcasted_iota(jnp.int32, (ck, tm), 0).astype(f32)
        posrow = posrow_s[0:1, :]

        def fill(c, carry):
            r0 = pl.multiple_of(c * ck, ck)
            pc = jnp.where(rows_f + r0.astype(f32) == posrow, 1.0, 0.0).astype(bf16)
            xs_s[pl.ds(r0, ck), :] = _mm(pc, xh).astype(bf16)
            wts_s[pl.ds(r0, ck), :] = _mm(pc, wth) + _mm(pc, wtl)
            outs_s[pl.ds(r0, ck), :] = jnp.zeros((ck, D_MODEL), f32)
            return carry

        lax.fori_loop(0, total, fill, 0)

    grp = s // STEPS_PER_GROUP
    off_g = meta_s[grp]
    lane_c = lax.broadcasted_iota(jnp.int32, (ck, LANES), 1)

    def chunk(c, carry):
        rows = pl.ds(pl.multiple_of(off_g + c * ck, ck), ck)
        xs = xs_s[rows, :]
        wts = wts_s[rows, :]
        gact = [_mm(xs, wg_ref[j]) for j in range(MOE_STEP_EXPERTS)]
        up = [_mm(xs, wu_ref[j]) for j in range(MOE_STEP_EXPERTS)]
        part = None
        for j in range(MOE_STEP_EXPERTS):
            wcol = jnp.sum(jnp.where(lane_c == s * MOE_STEP_EXPERTS + j, wts, 0.0), axis=-1, keepdims=True)
            hid = gact[j] * jax.nn.sigmoid(gact[j]) * up[j] * wcol
            out = _mm(hid.astype(bf16), wd_ref[j])
            part = out if part is None else part + out
        outs_s[rows, :] += part
        return carry

    lax.fori_loop(0, meta_s[N_EXPERT_GROUPS + grp], chunk, 0)

    @pl.when(s == MOE_STEPS - 1)
    def _():
        o_ref[...] = DEEPNORM_ALPHA * x_ref[...]
        pos_col = pos_s[:, 0:1]
        cols_f = lax.broadcasted_iota(jnp.int32, (tm, ck), 1).astype(f32)

        def unsort(c, carry):
            r0 = pl.multiple_of(c * ck, ck)
            pt = jnp.where(cols_f + r0.astype(f32) == pos_col, 1.0, 0.0).astype(bf16)
            oh, ol = _split2(outs_s[pl.ds(r0, ck), :])
            o_ref[...] += _mm(pt, oh) + _mm(pt, ol)
            return carry

        lax.fori_loop(0, meta_s[2 * N_EXPERT_GROUPS], unsort, 0)
        o_ref[...] = _layer_norm(o_ref[...], lnw_ref[...], lnb_ref[...])


def _moe(x2, rw, rb, wg, wu, wd, lnw, lnb, tm=1024):
    m = x2.shape[0]
    tm = min(tm, m)
    rwh, rwl = _split2(rw)
    n_rows = tm + N_EXPERT_GROUPS * MOE_CHUNK
    full = lambda a: pl.BlockSpec(a.shape, lambda i, e: (0,) * a.ndim)
    return pl.pallas_call(
        _moe_kernel,
        grid=(m // tm, MOE_STEPS),
        in_specs=[pl.BlockSpec((tm, D_MODEL), lambda i, e: (i, 0)),
                  full(rwh), full(rwl), full(rb),
                  pl.BlockSpec((MOE_STEP_EXPERTS, D_MODEL, D_EXPERT), lambda i, e: (e, 0, 0)),
                  pl.BlockSpec((MOE_STEP_EXPERTS, D_MODEL, D_EXPERT), lambda i, e: (e, 0, 0)),
                  pl.BlockSpec((MOE_STEP_EXPERTS, D_EXPERT, D_MODEL), lambda i, e: (e, 0, 0)),
                  full(lnw), full(lnb)],
        out_specs=pl.BlockSpec((tm, D_MODEL), lambda i, e: (i, 0)),
        out_shape=jax.ShapeDtypeStruct((m, D_MODEL), f32),
        scratch_shapes=[pltpu.VMEM((n_rows, D_MODEL), bf16), pltpu.VMEM((n_rows, D_MODEL), f32),
                        pltpu.VMEM((n_rows, LANES), f32), pltpu.VMEM((tm, LANES), f32),
                        pltpu.VMEM((8, tm), f32), pltpu.SMEM((2 * N_EXPERT_GROUPS + 1,), jnp.int32)],
        compiler_params=pltpu.CompilerParams(
            dimension_semantics=("parallel", "arbitrary"), vmem_limit_bytes=VMEM_LIMIT),
    )(x2, rwh, rwl, rb, wg, wu, wd, lnw, lnb)


def _router_weights(router_group_w, router_group_b, router_expert_w, router_expert_b):
    pad = LANES - N_EXPERTS - N_EXPERT_GROUPS
    rw = jnp.concatenate([router_expert_w.astype(f32), router_group_w.astype(f32),
                          jnp.zeros((D_MODEL, pad), f32)], axis=1)
    rb = jnp.concatenate([router_expert_b.astype(f32), router_group_b.astype(f32),
                          jnp.zeros((pad,), f32)]).reshape(1, LANES)
    return rw, rb


def kernel(x, w_in, rwkv_mu, rwkv_w0, rwkv_w2, rwkv_a0, rwkv_a2, rwkv_g2, rwkv_k_k, rwkv_k_a, rwkv_r_k, rwkv_ln_w, rwkv_ln_b, ssm_a_re, ssm_a_im, ssm_b_re, ssm_b_im, ssm_c_re, ssm_c_im, ssm_d, ssm_log_dt, ssm_glu_w, ssm_glu_b, w_up_rwkv, w_up_moba, w_up_ssm, gate_b, w_out, ln1_w, ln1_b, router_group_w, router_group_b, router_expert_w, router_expert_b, expert_w_gate, expert_w_up, expert_w_down, ln2_w, ln2_b):
    bsz, seq, _ = x.shape
    x2 = x.reshape(bsz * seq, D_MODEL).astype(f32)
    row = lambda a: a.reshape(1, -1).astype(f32)
    for l in range(DEPTH):
        w = w_in[l].astype(bf16)
        qkv = _project(x2, w[:, OFF_MOBA:OFF_SSM])
        y_a = _rwkv(x2, w[:, :OFF_MOBA],
                    dict(mu=rwkv_mu[l], w0=rwkv_w0[l], w2=rwkv_w2[l], a0=rwkv_a0[l], a2=rwkv_a2[l],
                         g2=rwkv_g2[l], k_k=rwkv_k_k[l], k_a=rwkv_k_a[l], r_k=rwkv_r_k[l],
                         ln_w=rwkv_ln_w[l], ln_b=rwkv_ln_b[l]), bsz, seq)
        y_b = _moba(qkv.reshape(bsz, seq, MOBA_COLS))
        y_c = _s5(x2, w[:, OFF_SSM:OFF_GATE],
                  dict(a_re=ssm_a_re[l], a_im=ssm_a_im[l], b_re=ssm_b_re[l], b_im=ssm_b_im[l],
                       c_re=ssm_c_re[l], c_im=ssm_c_im[l], d=ssm_d[l], log_dt=ssm_log_dt[l],
                       glu_w=ssm_glu_w[l], glu_b=ssm_glu_b[l]), bsz, seq)
        x2 = _merge(x2, y_a.reshape(-1, D_RWKV), y_b.reshape(-1, D_MOBA), y_c.reshape(-1, D_SSM),
                    w[:, OFF_GATE:], row(gate_b[l]),
                    w_up_rwkv[l].astype(bf16), w_up_moba[l].astype(bf16), w_up_ssm[l].astype(bf16),
                    w_out[l].astype(bf16), row(ln1_w[l]), row(ln1_b[l]))
        rw, rb = _router_weights(router_group_w[l], router_group_b[l], router_expert_w[l],
                                 router_expert_b[l])
        x2 = _moe(x2, rw, rb, expert_w_gate[l].astype(bf16), expert_w_up[l].astype(bf16),
                  expert_w_down[l].astype(bf16), row(ln2_w[l]), row(ln2_b[l]))
    return x2.reshape(bsz, seq, D_MODEL)
```

```python
import functools

import jax
import jax.numpy as jnp
import numpy as np
from jax import lax
from jax.experimental import pallas as pl
from jax.experimental.pallas import tpu as pltpu

f32 = jnp.float32
bf16 = jnp.bfloat16

D_MODEL = 1024
DEPTH = 4
HEAD_DIM = 64
D_RWKV = 512
RWKV_HEADS = D_RWKV // HEAD_DIM
DECAY_LORA = 64
ICL_LORA = 64
GATE_LORA = 128
RWKV_GN_EPS = 64e-5
D_MOBA = 512
MOBA_HEADS = D_MOBA // HEAD_DIM
MOBA_BLOCK = 256
MOBA_TOPK = 3
D_SSM = 512
SSM_GROUP = 16
SSM_GROUPS = D_SSM // SSM_GROUP
SSM_STATE = 64
N_BRANCHES = 3
N_EXPERT_GROUPS = 4
EXPERTS_PER_GROUP = 8
N_EXPERTS = N_EXPERT_GROUPS * EXPERTS_PER_GROUP
D_EXPERT = D_MODEL // 4
LN_EPS = 1e-5
DEEPNORM_ALPHA = (2 * DEPTH) ** 0.25
NEG_INF = -1e30
RWKV_COLS = 3 * D_RWKV + DECAY_LORA + ICL_LORA + GATE_LORA
MOBA_COLS = 3 * D_MOBA
OFF_MOBA = RWKV_COLS
OFF_SSM = OFF_MOBA + MOBA_COLS
OFF_GATE = OFF_SSM + D_SSM

LANES = 128
WKV_CHUNK = 64
WKV_PAIR = 4
VMEM_LIMIT = 48 * 1024 * 1024


def _nt(a, b):
    return lax.dot_general(a, b, (((1,), (1,)), ((), ())), preferred_element_type=f32)


def _tn(a, b):
    return lax.dot_general(a, b, (((0,), (0,)), ((), ())), preferred_element_type=f32)


def _mm(a, b):
    return jnp.dot(a, b, preferred_element_type=f32)


def _split2(x):
    hi = x.astype(bf16)
    lo = (x - hi.astype(f32)).astype(bf16)
    return hi, lo


def _layer_norm(y, w, b):
    mu = jnp.mean(y, axis=-1, keepdims=True)
    d = y - mu
    var = jnp.mean(d * d, axis=-1, keepdims=True)
    return d * lax.rsqrt(var + LN_EPS) * w + b


def _proj_kernel(x_ref, w_ref, o_ref):
    o_ref[...] = _mm(x_ref[...].astype(bf16), w_ref[...])


def _project(x2, w, tm=512):
    m, k = x2.shape
    n = w.shape[1]
    return pl.pallas_call(
        _proj_kernel,
        grid=(m // tm,),
        in_specs=[pl.BlockSpec((tm, k), lambda i: (i, 0)), pl.BlockSpec((k, n), lambda i: (0, 0))],
        out_specs=pl.BlockSpec((tm, n), lambda i: (i, 0)),
        out_shape=jax.ShapeDtypeStruct((m, n), f32),
        compiler_params=pltpu.CompilerParams(
            dimension_semantics=("parallel",), vmem_limit_bytes=VMEM_LIMIT),
    )(x2, w)


def _rwkv_kernel(x_ref, wz_ref, mu_ref, w0_ref, w2_ref, a0_ref, a2_ref, g2_ref, kk_ref, ka_ref, rk_ref,
                 lnw_ref, lnb_ref, seg_ref, tri_ref, o_ref,
                 zlast_s, state_s, r_s, k_s, v_s, a_s, b_s, lw_s, g_s, bon_s, y_s,
                 rt_s, bh_s, kh_s, tw_s, arb_s, pc_s):
    tb = x_ref.shape[0]
    n_chunks = tb // WKV_CHUNK
    c64 = WKV_CHUNK

    @pl.when(pl.program_id(1) == 0)
    def _():
        zlast_s[...] = jnp.zeros_like(zlast_s)
        state_s[...] = jnp.zeros_like(state_s)

    seg = seg_ref[...]

    def segsum(x):
        hi, lo = _split2(x)
        return _mm(hi, seg) + _mm(lo, seg)

    z = _mm(x_ref[...].astype(bf16), wz_ref[...])
    row = lax.broadcasted_iota(jnp.int32, z.shape, 0)
    z_prev = jnp.where(row == 0, zlast_s[...], pltpu.roll(z, 1, axis=0))
    zlast_s[...] = z[tb - 1:tb, :]
    zz = z + (z_prev - z) * mu_ref[...]
    r = zz[:, 0:D_RWKV]
    k = zz[:, D_RWKV:2 * D_RWKV]
    v = zz[:, 2 * D_RWKV:3 * D_RWKV]
    o1 = 3 * D_RWKV
    xw = zz[:, o1:o1 + DECAY_LORA]
    xa = zz[:, o1 + DECAY_LORA:o1 + DECAY_LORA + ICL_LORA]
    xg = zz[:, o1 + DECAY_LORA + ICL_LORA:]
    wl = w0_ref[...] + _mm(jnp.tanh(xw).astype(bf16), w2_ref[...])
    lw_s[...] = -float(np.exp(-0.5)) * jax.nn.sigmoid(wl)
    a_icl = jax.nn.sigmoid(a0_ref[...] + _mm(xa.astype(bf16), a2_ref[...]))
    g_s[...] = _mm(jax.nn.sigmoid(xg).astype(bf16), g2_ref[...])
    kk = k * kk_ref[...]
    kk = kk * lax.rsqrt(jnp.maximum(segsum(kk * kk), 1e-24))
    k2 = k * (1.0 + (a_icl - 1.0) * ka_ref[...])
    r_s[...] = r
    k_s[...] = k2
    v_s[...] = v
    a_s[...] = -kk
    b_s[...] = kk * a_icl
    bon_s[...] = segsum(r * k2 * rk_ref[...])

    ri = lax.broadcasted_iota(jnp.int32, (c64, c64), 0)
    ci = lax.broadcasted_iota(jnp.int32, (c64, c64), 1)
    strict = ri > ci
    incl = ri >= ci
    eye = jnp.where(ri == ci, 1.0, 0.0).astype(f32)
    tri = tri_ref[...]

    heads = range(RWKV_HEADS)
    sls = [slice(h * HEAD_DIM, (h + 1) * HEAD_DIM) for h in heads]

    def chunk_terms(c):
        rows = pl.ds(pl.multiple_of(c * c64, c64), c64)
        lwc = lw_s[rows, :]
        lhi, llo = _split2(lwc)
        cl = _mm(tri, lhi) + _mm(tri, llo)
        cl_last = cl[c64 - 1:c64, :]
        e_neg = jnp.exp(-cl)
        e_tot = jnp.exp(cl_last - cl)
        ac = a_s[rows, :]
        bc = b_s[rows, :]
        kc = k_s[rows, :]
        r_t = r_s[rows, :] * jnp.exp(cl)
        rt_s[rows, :] = r_t
        bh_s[rows, :] = bc * e_tot
        kh_s[rows, :] = kc * e_tot
        pc_s[pl.ds(c, 1), :] = jnp.exp(cl_last)
        return rows, ac * jnp.exp(cl - lwc), r_t, bc * e_neg, kc * e_neg, v_s[rows, :]

    def intra_body(c2, carry):
        terms = [chunk_terms(c2 * WKV_PAIR + j) for j in range(WKV_PAIR)]
        probs = [(j, h) for j in range(WKV_PAIR) for h in heads]
        cat = lambda j, p, q, h: jnp.concatenate([terms[j][p][:, sls[h]], terms[j][q][:, sls[h]]],
                                                 axis=0).astype(bf16)
        gm = [_nt(cat(j, 1, 2, h), cat(j, 3, 4, h)) for j, h in probs]
        a_ab = [jnp.where(strict, g[:c64, :c64], 0.0) for g in gm]
        a_ak = [jnp.where(strict, g[:c64, c64:], 0.0).astype(bf16) for g in gm]
        a_rb = [jnp.where(incl, g[c64:, :c64], 0.0).astype(bf16) for g in gm]
        a_rk = [jnp.where(incl, g[c64:, c64:], 0.0).astype(bf16) for g in gm]
        vb = [terms[j][5][:, sls[h]].astype(bf16) for j, h in probs]
        akv = [_mm(a, v) for a, v in zip(a_ak, vb)]
        yloc = [_mm(a, v) for a, v in zip(a_rk, vb)]
        xs = a_ab
        t_inv = [eye + x for x in xs]
        for _ in range(5):
            xb = [x.astype(bf16) for x in xs]
            xs = [_mm(x, x) for x in xb]
            t_inv = [t + _mm(t.astype(bf16), x.astype(bf16)) for t, x in zip(t_inv, xs)]
        rhs = [jnp.concatenate([terms[j][1][:, sls[h]], akv[n]], axis=1).astype(bf16)
               for n, (j, h) in enumerate(probs)]
        tw = [_mm(t.astype(bf16), r) for t, r in zip(t_inv, rhs)]
        for j in range(WKV_PAIR):
            mine = slice(j * RWKV_HEADS, (j + 1) * RWKV_HEADS)
            rows = terms[j][0]
            tw_s[rows, :] = jnp.concatenate(tw[mine], axis=1)
            arb_s[rows, :] = jnp.concatenate(a_rb[mine], axis=1)
            y_s[rows, :] = jnp.concatenate(yloc[mine], axis=1)
        return carry

    lax.fori_loop(0, n_chunks // WKV_PAIR, intra_body, 0)

    def state_body(c, carry):
        rows = pl.ds(pl.multiple_of(c * c64, c64), c64)
        twc = tw_s[rows, :]
        arbc = arb_s[rows, :]
        r_t = rt_s[rows, :]
        b_h = bh_s[rows, :]
        k_h = kh_s[rows, :]
        vc = v_s[rows, :]
        yl = y_s[rows, :]
        p_c = pc_s[pl.ds(c, 1), :]
        s_old = [state_s[h] for h in heads]
        wr = [jnp.concatenate([twc[:, h * LANES:h * LANES + c64], r_t[:, sls[h]]], axis=0).astype(bf16)
              for h in heads]
        wrs = [_nt(wr[h], s_old[h].astype(bf16)) for h in heads]
        u = [wrs[h][:c64] + twc[:, h * LANES + c64:(h + 1) * LANES] for h in heads]
        ys = [wrs[h][c64:] + _mm(arbc[:, sls[h]], u[h].astype(bf16)) + yl[:, sls[h]] for h in heads]
        uv = [jnp.concatenate([u[h], vc[:, sls[h]]], axis=0).astype(bf16) for h in heads]
        bkh = [jnp.concatenate([b_h[:, sls[h]], k_h[:, sls[h]]], axis=0).astype(bf16) for h in heads]
        for h in heads:
            state_s[h] = s_old[h] * p_c[:, sls[h]] + _tn(uv[h], bkh[h])
        y_s[rows, :] = jnp.concatenate(ys, axis=1)
        return carry

    lax.fori_loop(0, n_chunks, state_body, 0)

    y = y_s[...]
    mean = segsum(y) * (1.0 / HEAD_DIM)
    d = y - mean
    var = segsum(d * d) * (1.0 / HEAD_DIM)
    yn = d * lax.rsqrt(var + RWKV_GN_EPS) * lnw_ref[...] + lnb_ref[...]
    o_ref[0] = (yn + bon_s[...] * v_s[...]) * g_s[...]


def _rwkv(x2, wz, p, bsz, seq, tb=512):
    tb = min(tb, seq)
    n_l = seq // tb
    seg = (np.arange(D_RWKV)[:, None] // HEAD_DIM == np.arange(D_RWKV)[None, :] // HEAD_DIM)
    seg = jnp.asarray(seg, bf16)
    tri = jnp.asarray(np.tril(np.ones((WKV_CHUNK, WKV_CHUNK))), bf16)
    row = lambda a: a.reshape(1, -1).astype(f32)
    args = [x2, wz, row(p['mu']), row(p['w0']), p['w2'].astype(bf16), row(p['a0']), p['a2'].astype(bf16),
            p['g2'].astype(bf16), row(p['k_k']), row(p['k_a']), row(p['r_k']), row(p['ln_w']),
            row(p['ln_b']), seg, tri]
    full = lambda a: pl.BlockSpec(a.shape, lambda b, l: (0,) * a.ndim)
    in_specs = [pl.BlockSpec((tb, D_MODEL), lambda b, l: (b * n_l + l, 0))] + [full(a) for a in args[1:]]
    big = lambda: pltpu.VMEM((tb, D_RWKV), f32)
    return pl.pallas_call(
        _rwkv_kernel,
        grid=(bsz, n_l),
        in_specs=in_specs,
        out_specs=pl.BlockSpec((1, tb, D_RWKV), lambda b, l: (b, l, 0)),
        out_shape=jax.ShapeDtypeStruct((bsz, seq, D_RWKV), f32),
        scratch_shapes=[pltpu.VMEM((1, RWKV_COLS), f32),
                        pltpu.VMEM((RWKV_HEADS, HEAD_DIM, HEAD_DIM), f32),
                        big(), big(), big(), big(), big(), big(), big(), big(), big(),
                        big(), big(), big(), pltpu.VMEM((tb, RWKV_HEADS * LANES), f32),
                        pltpu.VMEM((tb, D_RWKV), bf16), pltpu.VMEM((tb // WKV_CHUNK, D_RWKV), f32)],
        compiler_params=pltpu.CompilerParams(
            dimension_semantics=("parallel", "arbitrary"), vmem_limit_bytes=VMEM_LIMIT),
    )(*args)


MOBA_AUG_POS = 0
MOBA_AUG_BLK = 6
LOG2E = 1.4426950408889634


MOBA_STEP_HEADS = 4


def _moba_kernel(q_ref, k_ref, v_ref, tmpl_ref, o_ref, kaug_s, vaug_s, kmean_s):
    blk = MOBA_BLOCK
    half = HEAD_DIM
    seq = k_ref.shape[1]
    n_blk = seq // blk
    qi = pl.program_id(2)
    heads = range(MOBA_STEP_HEADS)
    tile = [slice((hh // 2) * LANES, (hh // 2 + 1) * LANES) for hh in heads]
    aug0 = [half if hh % 2 == 0 else 0 for hh in heads]

    @pl.when(qi == 0)
    def _():
        lane_l = lax.broadcasted_iota(jnp.int32, (seq, LANES), 1)
        for hh in heads:
            kb = k_ref[0, :, tile[hh]].astype(bf16)
            vb = v_ref[0, :, tile[hh]].astype(bf16)
            in_head = (lane_l < half) if hh % 2 == 0 else (lane_l >= half)
            kaug_s[hh] = jnp.where(in_head, kb, tmpl_ref[0, hh])
            ones_lane = jnp.where(lane_l == aug0[hh], 1.0, 0.0).astype(bf16)
            vaug_s[hh] = jnp.where(in_head, vb, ones_lane)
        for n in range(n_blk):
            km = jnp.mean(k_ref[0, n * blk:(n + 1) * blk, :], axis=0, keepdims=True)
            for tt in range(MOBA_STEP_HEADS // 2):
                kmean_s[tt, n:n + 1, :] = km[:, tt * LANES:(tt + 1) * LANES]

    ri = lax.broadcasted_iota(jnp.int32, (blk, blk), 0)
    ci = lax.broadcasted_iota(jnp.int32, (blk, blk), 1)
    causal = ri >= ci
    lane = lax.broadcasted_iota(jnp.int32, (blk, LANES), 1)
    lane_f = lane.astype(f32)
    qf = [q_ref[0, :, tile[hh]] * (HEAD_DIM ** -0.5 * LOG2E) for hh in heads]
    in_head = [(lane < half) if hh % 2 == 0 else (lane >= half) for hh in heads]
    blk_lane = [lane - (aug0[hh] + MOBA_AUG_BLK) for hh in heads]

    qs = [_split2(jnp.where(in_head[hh], qf[hh], 0.0)) for hh in heads]
    kms = [_split2(kmean_s[tt]) for tt in range(MOBA_STEP_HEADS // 2)]
    gate = [_nt(kms[hh // 2][0], qs[hh][0]) + _nt(kms[hh // 2][1], qs[hh][0])
            + _nt(kms[hh // 2][0], qs[hh][1]) for hh in heads]
    brow = lax.broadcasted_iota(jnp.int32, (n_blk, blk), 0)
    brow_f = brow.astype(f32)
    g = [jnp.where(brow < qi, x, -jnp.inf) for x in gate]
    picked = [jnp.zeros((n_blk, blk), f32) for _ in heads]
    for _ in range(MOBA_TOPK):
        mx = [jnp.max(x, axis=0, keepdims=True) for x in g]
        idx = [jnp.min(jnp.where(g[hh] == mx[hh], brow_f, 1e9), axis=0, keepdims=True) for hh in heads]
        hit = [brow_f == i for i in idx]
        picked = [jnp.where(hit[hh], 1.0, picked[hh]) for hh in heads]
        g = [jnp.where(hit[hh], -jnp.inf, g[hh]) for hh in heads]
    sel_t = [jnp.where(brow < qi, p, 0.0).astype(bf16) for p in picked]
    lane_b = lax.broadcasted_iota(jnp.int32, (n_blk, LANES), 1)
    brow_b = lax.broadcasted_iota(jnp.int32, (n_blk, LANES), 0)
    q_aug, q_own = [], []
    for hh in heads:
        place = jnp.where(lane_b == brow_b + (aug0[hh] + MOBA_AUG_BLK), 1.0, 0.0).astype(bf16)
        sel = _tn(sel_t[hh], place) > 0.5
        is_pos = (lane >= aug0[hh] + MOBA_AUG_POS) & (lane < aug0[hh] + MOBA_AUG_POS + 3)
        is_blk = (blk_lane[hh] >= 0) & (blk_lane[hh] < n_blk)
        own = jnp.where(is_pos, 1.0, 0.0)
        aug = jnp.where(is_blk, jnp.where(sel, 0.0, NEG_INF), own)
        q_aug.append(jnp.where(in_head[hh], qf[hh], aug).astype(bf16))
        q_own.append(jnp.where(in_head[hh], qf[hh], own).astype(bf16))

    start = pl.multiple_of(qi * blk, blk)
    s0 = [_nt(q_own[hh], kaug_s[hh, pl.ds(start, blk), :]) for hh in heads]
    s0 = [jnp.where(causal, s, NEG_INF) for s in s0]
    m0 = [jnp.max(s, axis=-1, keepdims=True) for s in s0]
    p0 = [jnp.exp2(s - m).astype(bf16) for s, m in zip(s0, m0)]
    a0 = [_mm(p0[hh], vaug_s[hh, pl.ds(start, blk), :]) for hh in heads]

    def make_body(width, base):
        def body(t, carry):
            ms, accs = carry
            rows = pl.ds(pl.multiple_of(base + t * (width * blk), 2 * blk), width * blk)
            s = [_nt(q_aug[hh], kaug_s[hh, rows, :]) for hh in heads]
            m_new = [jnp.maximum(ms[hh], jnp.max(s[hh], axis=-1, keepdims=True)) for hh in heads]
            p = [jnp.exp2(s[hh] - m_new[hh]).astype(bf16) for hh in heads]
            pv = [_mm(p[hh], vaug_s[hh, rows, :]) for hh in heads]
            accs = [accs[hh] * jnp.exp2(ms[hh] - m_new[hh]) + pv[hh] for hh in heads]
            return (tuple(m_new), tuple(accs))
        return body

    n_pairs = (qi + 1) // 2
    n_quads = n_pairs // 2
    carry = lax.fori_loop(0, n_quads, make_body(4, 0), (tuple(m0), tuple(a0)))
    _, accs = lax.fori_loop(0, n_pairs - 2 * n_quads, make_body(2, n_quads * (4 * blk)), carry)
    out = [accs[hh] / accs[hh][:, aug0[hh]:aug0[hh] + 1] for hh in heads]
    for tt in range(MOBA_STEP_HEADS // 2):
        o_ref[0, :, tt * LANES:(tt + 1) * LANES] = jnp.where(lane < half, out[2 * tt], out[2 * tt + 1])


def _moba_template(seq):
    pos = np.arange(seq, dtype=np.float64)
    tmpl = np.zeros((MOBA_HEADS, seq, LANES), np.float32)
    for h in range(MOBA_HEADS):
        a0 = HEAD_DIM if h % 2 == 0 else 0
        slope = 2.0 ** (-8.0 * (h + 1) / MOBA_HEADS)
        rest = slope * LOG2E * pos
        for c in range(3):
            part = rest.astype(np.float32).astype(bf16).astype(np.float64)
            tmpl[h, :, a0 + MOBA_AUG_POS + c] = part
            rest = rest - part
        tmpl[h, np.arange(seq), a0 + MOBA_AUG_BLK + np.arange(seq) // MOBA_BLOCK] = 1.0
    return jnp.asarray(tmpl.reshape(MOBA_HEADS // MOBA_STEP_HEADS, MOBA_STEP_HEADS, seq, LANES), bf16)


def _moba(qkv3):
    bsz, seq, _ = qkv3.shape
    blk = MOBA_BLOCK
    sh = MOBA_STEP_HEADS
    n_grp = MOBA_HEADS // sh
    wid = sh * HEAD_DIM
    assert seq % (2 * blk) == 0 and seq // blk <= HEAD_DIM - MOBA_AUG_BLK
    return pl.pallas_call(
        _moba_kernel,
        grid=(bsz, n_grp, seq // blk),
        in_specs=[pl.BlockSpec((1, blk, wid), lambda b, p, i: (b, i, p)),
                  pl.BlockSpec((1, seq, wid), lambda b, p, i: (b, 0, n_grp + p)),
                  pl.BlockSpec((1, seq, wid), lambda b, p, i: (b, 0, 2 * n_grp + p)),
                  pl.BlockSpec((1, sh, seq, LANES), lambda b, p, i: (p, 0, 0, 0))],
        out_specs=pl.BlockSpec((1, blk, wid), lambda b, p, i: (b, i, p)),
        out_shape=jax.ShapeDtypeStruct((bsz, seq, D_MOBA), f32),
        scratch_shapes=[pltpu.VMEM((sh, seq, LANES), bf16), pltpu.VMEM((sh, seq, LANES), bf16),
                        pltpu.VMEM((sh // 2, seq // blk, LANES), f32)],
        compiler_params=pltpu.CompilerParams(
            dimension_semantics=("parallel", "parallel", "arbitrary"), vmem_limit_bytes=VMEM_LIMIT),
    )(qkv3, qkv3, qkv3, _moba_template(seq))


S5_SUPER = 4
S5_SEGS = 8
S5_COLS = 512


def _s5_kernel(x_ref, wu_ref, perm_ref, bd_ref, cre_ref, cim_ref, lam_ref, d_ref, gw_ref, gb_ref, o_ref,
               hr_s, hi_s, carry_s):
    tb = x_ref.shape[0]
    seg = tb // S5_SEGS
    n_state = lam_ref.shape[1]
    sgw = n_state // S5_SUPER
    uw = D_SSM // S5_SUPER

    @pl.when(pl.program_id(1) == 0)
    def _():
        carry_s[...] = jnp.zeros_like(carry_s)

    u = _mm(x_ref[...].astype(bf16), wu_ref[...])
    perm = perm_ref[...]
    ub = _mm(perm, u.astype(bf16)).astype(bf16)
    for sg in range(S5_SUPER):
        bu = _mm(ub[:, sg * uw:(sg + 1) * uw], bd_ref[sg])
        for j in range(sgw // LANES):
            hr_s[sg * (sgw // LANES) + j] = bu[:, j * LANES:(j + 1) * LANES]
            hi_s[sg * (sgw // LANES) + j] = bu[:, sgw + j * LANES:sgw + (j + 1) * LANES]

    tpc = S5_COLS // LANES

    def load(ref, cc, idx):
        return jnp.concatenate([ref[cc * tpc + j, idx, :] for j in range(tpc)], axis=1)

    def store(ref, cc, idx, val):
        for j in range(tpc):
            ref[cc * tpc + j, idx, :] = val[:, j * LANES:(j + 1) * LANES]

    for cc in range(n_state // S5_COLS):
        cols = slice(cc * S5_COLS, (cc + 1) * S5_COLS)
        lr = lam_ref[0:1, cols]
        li = lam_ref[1:2, cols]

        def local(r, carry, cc=cc, lr=lr, li=li):
            hr, hi = carry
            idx = pl.ds(pl.multiple_of(r * S5_SEGS, S5_SEGS), S5_SEGS)
            nr = lr * hr - li * hi + load(hr_s, cc, idx)
            ni = lr * hi + li * hr + load(hi_s, cc, idx)
            store(hr_s, cc, idx, nr)
            store(hi_s, cc, idx, ni)
            return nr, ni

        zero = jnp.zeros((S5_SEGS, S5_COLS), f32)
        er, ei = lax.fori_loop(0, seg, local, (zero, zero), unroll=4)
        pr, pi = lr, li
        for _ in range(seg.bit_length() - 1):
            pr, pi = pr * pr - pi * pi, 2.0 * pr * pi
        cr = carry_s[0:1, cols]
        ci = carry_s[1:2, cols]
        in_r, in_i = [], []
        for s in range(S5_SEGS):
            in_r.append(cr)
            in_i.append(ci)
            cr, ci = pr * cr - pi * ci + er[s:s + 1, :], pr * ci + pi * cr + ei[s:s + 1, :]
        carry_s[0:1, cols] = cr
        carry_s[1:2, cols] = ci

        def fix(r, carry, cc=cc, lr=lr, li=li):
            qr, qi = carry
            qr, qi = lr * qr - li * qi, lr * qi + li * qr
            idx = pl.ds(pl.multiple_of(r * S5_SEGS, S5_SEGS), S5_SEGS)
            store(hr_s, cc, idx, load(hr_s, cc, idx) + qr)
            store(hi_s, cc, idx, load(hi_s, cc, idx) + qi)
            return qr, qi

        lax.fori_loop(0, seg, fix, (jnp.concatenate(in_r, axis=0), jnp.concatenate(in_i, axis=0)),
                      unroll=4)

    slab = lambda ref, sg: jnp.concatenate(
        [ref[sg * (sgw // LANES) + j] for j in range(sgw // LANES)], axis=1).astype(bf16)
    ys = [_mm(slab(hr_s, sg), cre_ref[sg]) + _mm(slab(hi_s, sg), cim_ref[sg]) for sg in range(S5_SUPER)]
    yh, yl = _split2(jnp.concatenate(ys, axis=1))
    y = _tn(perm, yh) + _tn(perm, yl) + d_ref[...] * u
    y = 0.5 * y * (1.0 + jnp.tanh(0.7978845608028654 * (y + 0.044715 * (y * y * y))))
    zg = _mm(y.astype(bf16), gw_ref[...]) + gb_ref[...]
    o_ref[0] = zg[:, :D_SSM] * jax.nn.sigmoid(zg[:, D_SSM:])


def _s5_params(p):
    a_re = p['a_re'].astype(f32)
    a_im = p['a_im'].astype(f32)
    dt = jnp.exp(p['log_dt'].astype(f32))[:, None]
    mag = jnp.exp(a_re * dt)
    lam_re = mag * jnp.cos(a_im * dt)
    lam_im = mag * jnp.sin(a_im * dt)
    den = a_re * a_re + a_im * a_im
    nr = lam_re - 1.0
    coef_re = (nr * a_re + lam_im * a_im) / den
    coef_im = (lam_im * a_re - nr * a_im) / den
    b_re = p['b_re'].astype(f32)
    b_im = p['b_im'].astype(f32)
    bb_re = coef_re[..., None] * b_re - coef_im[..., None] * b_im
    bb_im = coef_re[..., None] * b_im + coef_im[..., None] * b_re
    gps = SSM_GROUPS // S5_SUPER
    eye = jnp.eye(gps, dtype=f32)
    n_state = SSM_GROUPS * SSM_STATE
    sgw = n_state // S5_SUPER
    uw = D_SSM // S5_SUPER

    def in_map(bb):
        bb = bb.reshape(S5_SUPER, gps, SSM_STATE, SSM_GROUP)
        return jnp.einsum('sgph,gk->sghkp', bb, eye).reshape(S5_SUPER, uw, sgw)

    def out_map(c):
        c = c.reshape(S5_SUPER, gps, SSM_GROUP, SSM_STATE)
        return jnp.einsum('sghp,gk->sgpkh', c, eye).reshape(S5_SUPER, sgw, uw)

    bd = jnp.concatenate([in_map(bb_re), in_map(bb_im)], axis=2)
    cre = out_map(p['c_re'].astype(f32))
    cim = -out_map(p['c_im'].astype(f32))
    lam = jnp.stack([lam_re.reshape(n_state), lam_im.reshape(n_state)], axis=0)
    return bd.astype(bf16), cre.astype(bf16), cim.astype(bf16), lam


def _s5(x2, wu, p, bsz, seq, tb=512):
    tb = min(tb, seq)
    n_l = seq // tb
    seg = tb // S5_SEGS
    assert tb % S5_SEGS == 0 and seg & (seg - 1) == 0
    n_state = SSM_GROUPS * SSM_STATE
    bd, cre, cim, lam = _s5_params(p)
    src = (np.arange(tb) % S5_SEGS) * seg + np.arange(tb) // S5_SEGS
    perm = jnp.asarray(np.arange(tb)[None, :] == src[:, None], bf16)
    args = [x2, wu, perm, bd, cre, cim, lam, p['d'].reshape(1, -1).astype(f32), p['glu_w'].astype(bf16),
            p['glu_b'].reshape(1, -1).astype(f32)]
    full = lambda a: pl.BlockSpec(a.shape, lambda b, l: (0,) * a.ndim)
    return pl.pallas_call(
        _s5_kernel,
        grid=(bsz, n_l),
        in_specs=[pl.BlockSpec((tb, D_MODEL), lambda b, l: (b * n_l + l, 0))] + [full(a) for a in args[1:]],
        out_specs=pl.BlockSpec((1, tb, D_SSM), lambda b, l: (b, l, 0)),
        out_shape=jax.ShapeDtypeStruct((bsz, seq, D_SSM), f32),
        scratch_shapes=[pltpu.VMEM((n_state // LANES, tb, LANES), f32),
                        pltpu.VMEM((n_state // LANES, tb, LANES), f32),
                        pltpu.VMEM((2, n_state), f32)],
        compiler_params=pltpu.CompilerParams(
            dimension_semantics=("parallel", "arbitrary"), vmem_limit_bytes=VMEM_LIMIT),
    )(*args)


def _merge_kernel(x_ref, ya_ref, yb_ref, yc_ref, wg_ref, gb_ref, wa_ref, wb_ref, wc_ref, wo_ref,
                  lnw_ref, lnb_ref, o_ref):
    x = x_ref[...]
    xb = x.astype(bf16)
    merged = None
    for br, (y_ref, w_ref) in enumerate(((ya_ref, wa_ref), (yb_ref, wb_ref), (yc_ref, wc_ref))):
        cols = slice(br * D_MODEL, (br + 1) * D_MODEL)
        gate = jax.nn.sigmoid(_mm(xb, wg_ref[:, cols]) + gb_ref[:, cols])
        term = gate * _mm(y_ref[...].astype(bf16), w_ref[...])
        merged = term if merged is None else merged + term
    h = _mm(merged.astype(bf16), wo_ref[...])
    o_ref[...] = _layer_norm(DEEPNORM_ALPHA * x + h, lnw_ref[...], lnb_ref[...])


def _merge(x2, ya, yb, yc, wg, gb, wa, wb, wc, wo, lnw, lnb, tm=256):
    m = x2.shape[0]
    rowblk = lambda n: pl.BlockSpec((tm, n), lambda i: (i, 0))
    full = lambda a: pl.BlockSpec(a.shape, lambda i: (0,) * a.ndim)
    consts = [wg, gb, wa, wb, wc, wo, lnw, lnb]
    return pl.pallas_call(
        _merge_kernel,
        grid=(m // tm,),
        in_specs=[rowblk(D_MODEL), rowblk(D_RWKV), rowblk(D_MOBA), rowblk(D_SSM)]
        + [full(a) for a in consts],
        out_specs=rowblk(D_MODEL),
        out_shape=jax.ShapeDtypeStruct((m, D_MODEL), f32),
        compiler_params=pltpu.CompilerParams(
            dimension_semantics=("parallel",), vmem_limit_bytes=VMEM_LIMIT),
    )(x2, ya, yb, yc, *consts)


MOE_STEP_EXPERTS = 4
MOE_CHUNK = 256
MOE_STEPS = N_EXPERTS // MOE_STEP_EXPERTS
STEPS_PER_GROUP = EXPERTS_PER_GROUP // MOE_STEP_EXPERTS


def _moe_kernel(x_ref, rwh_ref, rwl_ref, rb_ref, wg_ref, wu_ref, wd_ref, lnw_ref, lnb_ref, o_ref,
                xs_s, outs_s, wts_s, pos_s, posrow_s, meta_s):
    s = pl.program_id(1)
    tm = x_ref.shape[0]
    ck = MOE_CHUNK
    lane = lax.broadcasted_iota(jnp.int32, (tm, LANES), 1)
    lane_f = lane.astype(f32)

    @pl.when(s == 0)
    def _():
        xh, xl = _split2(x_ref[...])
        rwh = rwh_ref[...]
        logits = _mm(xh, rwh) + _mm(xh, rwl_ref[...]) + _mm(xl, rwh) + rb_ref[...]
        is_grp = (lane >= N_EXPERTS) & (lane < N_EXPERTS + N_EXPERT_GROUPS)
        gl = jnp.where(is_grp, logits, -jnp.inf)
        gmax = jnp.max(gl, axis=-1, keepdims=True)
        gidx = jnp.min(jnp.where(gl == gmax, lane_f, 1e9), axis=-1, keepdims=True) - float(N_EXPERTS)
        p_group = 1.0 / jnp.sum(jnp.where(is_grp, jnp.exp(gl - gmax), 0.0), axis=-1, keepdims=True)
        grp_of_lane = jnp.floor(lane_f * (1.0 / EXPERTS_PER_GROUP))
        in_grp = (lane < N_EXPERTS) & (grp_of_lane == gidx)
        el = jnp.where(in_grp, logits, -jnp.inf)
        m1 = jnp.max(el, axis=-1, keepdims=True)
        i1 = jnp.min(jnp.where(el == m1, lane_f, 1e9), axis=-1, keepdims=True)
        el2 = jnp.where(lane_f == i1, -jnp.inf, el)
        m2 = jnp.max(el2, axis=-1, keepdims=True)
        i2 = jnp.min(jnp.where(el2 == m2, lane_f, 1e9), axis=-1, keepdims=True)
        e2 = jnp.exp(m2 - m1)
        w1 = p_group / (1.0 + e2)
        w2 = p_group * e2 / (1.0 + e2)
        wt = jnp.where(lane_f == i1, w1, 0.0) + jnp.where(lane_f == i2, w2, 0.0)

        ohg = lane_f == gidx
        ri = lax.broadcasted_iota(jnp.int32, (tm, tm), 0)
        ci = lax.broadcasted_iota(jnp.int32, (tm, tm), 1)
        tri = jnp.where(ri >= ci, 1.0, 0.0).astype(bf16)
        cg = _mm(tri, jnp.where(ohg, 1.0, 0.0).astype(bf16))
        cnt = cg[tm - 1:tm, :]
        padded = jnp.ceil(cnt * (1.0 / ck)) * ck
        ui = lax.broadcasted_iota(jnp.int32, (LANES, LANES), 0)
        uj = lax.broadcasted_iota(jnp.int32, (LANES, LANES), 1)
        upper = jnp.where(ui < uj, 1.0, 0.0).astype(bf16)
        off = _mm(jnp.broadcast_to(padded, (8, LANES)).astype(bf16), upper)[0:1, :]
        rank = jnp.sum(jnp.where(ohg, cg - 1.0, 0.0), axis=-1, keepdims=True)
        pos = jnp.sum(jnp.where(ohg, off, 0.0), axis=-1, keepdims=True) + rank
        pos_s[...] = jnp.broadcast_to(pos, (tm, LANES))
        pa = jnp.floor(pos * (1.0 / 64.0))
        pb = pos - 64.0 * pa
        cols = jnp.where(lane == 0, pa, jnp.where(lane == 1, pb, 0.0)).astype(bf16)
        r8 = lax.broadcasted_iota(jnp.int32, (8, LANES), 0)
        l8 = lax.broadcasted_iota(jnp.int32, (8, LANES), 1)
        sel8 = jnp.where((r8 == 0) & (l8 == 0), 64.0, jnp.where((r8 == 0) & (l8 == 1), 1.0, 0.0))
        posrow_s[...] = _nt(sel8.astype(bf16), cols)
        off_i = off.astype(jnp.int32)
        nck_i = (padded * (1.0 / ck)).astype(jnp.int32)
        total = 0
        for g in range(N_EXPERT_GROUPS):
            meta_s[g] = off_i[0, g]
            meta_s[N_EXPERT_GROUPS + g] = nck_i[0, g]
            total = total + nck_i[0, g]
        meta_s[2 * N_EXPERT_GROUPS] = total

        wth, wtl = _split2(wt)
        rows_f = lax.broadcasted_iota(jnp.int32, (ck, tm), 0).astype(f32)
        posrow = posrow_s[0:1, :]

        def fill(c, carry):
            r0 = pl.multiple_of(c * ck, ck)
            pc = jnp.where(rows_f + r0.astype(f32) == posrow, 1.0, 0.0).astype(bf16)
            xs_s[pl.ds(r0, ck), :] = _mm(pc, xh).astype(bf16)
            wts_s[pl.ds(r0, ck), :] = _mm(pc, wth) + _mm(pc, wtl)
            outs_s[pl.ds(r0, ck), :] = jnp.zeros((ck, D_MODEL), f32)
            return carry

        lax.fori_loop(0, total, fill, 0)

    grp = s // STEPS_PER_GROUP
    off_g = meta_s[grp]
    lane_c = lax.broadcasted_iota(jnp.int32, (ck, LANES), 1)

    def chunk(c, carry):
        rows = pl.ds(pl.multiple_of(off_g + c * ck, ck), ck)
        xs = xs_s[rows, :]
        wts = wts_s[rows, :]
        gact = [_mm(xs, wg_ref[j]) for j in range(MOE_STEP_EXPERTS)]
        up = [_mm(xs, wu_ref[j]) for j in range(MOE_STEP_EXPERTS)]
        part = None
        for j in range(MOE_STEP_EXPERTS):
            wcol = jnp.sum(jnp.where(lane_c == s * MOE_STEP_EXPERTS + j, wts, 0.0), axis=-1, keepdims=True)
            hid = gact[j] * jax.nn.sigmoid(gact[j]) * up[j] * wcol
            out = _mm(hid.astype(bf16), wd_ref[j])
            part = out if part is None else part + out
        outs_s[rows, :] += part
        return carry

    lax.fori_loop(0, meta_s[N_EXPERT_GROUPS + grp], chunk, 0)

    @pl.when(s == MOE_STEPS - 1)
    def _():
        o_ref[...] = DEEPNORM_ALPHA * x_ref[...]
        pos_col = pos_s[:, 0:1]
        cols_f = lax.broadcasted_iota(jnp.int32, (tm, ck), 1).astype(f32)

        def unsort(c, carry):
            r0 = pl.multiple_of(c * ck, ck)
            pt = jnp.where(cols_f + r0.astype(f32) == pos_col, 1.0, 0.0).astype(bf16)
            o_ref[...] += _mm(pt, outs_s[pl.ds(r0, ck), :].astype(bf16))
            return carry

        lax.fori_loop(0, meta_s[2 * N_EXPERT_GROUPS], unsort, 0)
        o_ref[...] = _layer_norm(o_ref[...], lnw_ref[...], lnb_ref[...])


def _moe(x2, rw, rb, wg, wu, wd, lnw, lnb, tm=1024):
    m = x2.shape[0]
    tm = min(tm, m)
    rwh, rwl = _split2(rw)
    n_rows = tm + N_EXPERT_GROUPS * MOE_CHUNK
    full = lambda a: pl.BlockSpec(a.shape, lambda i, e: (0,) * a.ndim)
    return pl.pallas_call(
        _moe_kernel,
        grid=(m // tm, MOE_STEPS),
        in_specs=[pl.BlockSpec((tm, D_MODEL), lambda i, e: (i, 0)),
                  full(rwh), full(rwl), full(rb),
                  pl.BlockSpec((MOE_STEP_EXPERTS, D_MODEL, D_EXPERT), lambda i, e: (e, 0, 0)),
                  pl.BlockSpec((MOE_STEP_EXPERTS, D_MODEL, D_EXPERT), lambda i, e: (e, 0, 0)),
                  pl.BlockSpec((MOE_STEP_EXPERTS, D_EXPERT, D_MODEL), lambda i, e: (e, 0, 0)),
                  full(lnw), full(lnb)],
        out_specs=pl.BlockSpec((tm, D_MODEL), lambda i, e: (i, 0)),
        out_shape=jax.ShapeDtypeStruct((m, D_MODEL), f32),
        scratch_shapes=[pltpu.VMEM((n_rows, D_MODEL), bf16), pltpu.VMEM((n_rows, D_MODEL), f32),
                        pltpu.VMEM((n_rows, LANES), f32), pltpu.VMEM((tm, LANES), f32),
                        pltpu.VMEM((8, tm), f32), pltpu.SMEM((2 * N_EXPERT_GROUPS + 1,), jnp.int32)],
        compiler_params=pltpu.CompilerParams(
            dimension_semantics=("parallel", "arbitrary"), vmem_limit_bytes=VMEM_LIMIT),
    )(x2, rwh, rwl, rb, wg, wu, wd, lnw, lnb)


def _router_weights(router_group_w, router_group_b, router_expert_w, router_expert_b):
    pad = LANES - N_EXPERTS - N_EXPERT_GROUPS
    rw = jnp.concatenate([router_expert_w.astype(f32), router_group_w.astype(f32),
                          jnp.zeros((D_MODEL, pad), f32)], axis=1)
    rb = jnp.concatenate([router_expert_b.astype(f32), router_group_b.astype(f32),
                          jnp.zeros((pad,), f32)]).reshape(1, LANES)
    return rw, rb


def kernel(x, w_in, rwkv_mu, rwkv_w0, rwkv_w2, rwkv_a0, rwkv_a2, rwkv_g2, rwkv_k_k, rwkv_k_a, rwkv_r_k, rwkv_ln_w, rwkv_ln_b, ssm_a_re, ssm_a_im, ssm_b_re, ssm_b_im, ssm_c_re, ssm_c_im, ssm_d, ssm_log_dt, ssm_glu_w, ssm_glu_b, w_up_rwkv, w_up_moba, w_up_ssm, gate_b, w_out, ln1_w, ln1_b, router_group_w, router_group_b, router_expert_w, router_expert_b, expert_w_gate, expert_w_up, expert_w_down, ln2_w, ln2_b):
    bsz, seq, _ = x.shape
    x2 = x.reshape(bsz * seq, D_MODEL).astype(f32)
    row = lambda a: a.reshape(1, -1).astype(f32)
    for l in range(DEPTH):
        w = w_in[l].astype(bf16)
        qkv = _project(x2, w[:, OFF_MOBA:OFF_SSM])
        y_a = _rwkv(x2, w[:, :OFF_MOBA],
                    dict(mu=rwkv_mu[l], w0=rwkv_w0[l], w2=rwkv_w2[l], a0=rwkv_a0[l], a2=rwkv_a2[l],
                         g2=rwkv_g2[l], k_k=rwkv_k_k[l], k_a=rwkv_k_a[l], r_k=rwkv_r_k[l],
                         ln_w=rwkv_ln_w[l], ln_b=rwkv_ln_b[l]), bsz, seq)
        y_b = _moba(qkv.reshape(bsz, seq, MOBA_COLS))
        y_c = _s5(x2, w[:, OFF_SSM:OFF_GATE],
                  dict(a_re=ssm_a_re[l], a_im=ssm_a_im[l], b_re=ssm_b_re[l], b_im=ssm_b_im[l],
                       c_re=ssm_c_re[l], c_im=ssm_c_im[l], d=ssm_d[l], log_dt=ssm_log_dt[l],
                       glu_w=ssm_glu_w[l], glu_b=ssm_glu_b[l]), bsz, seq)
        x2 = _merge(x2, y_a.reshape(-1, D_RWKV), y_b.reshape(-1, D_MOBA), y_c.reshape(-1, D_SSM),
                    w[:, OFF_GATE:], row(gate_b[l]),
                    w_up_rwkv[l].astype(bf16), w_up_moba[l].astype(bf16), w_up_ssm[l].astype(bf16),
                    w_out[l].astype(bf16), row(ln1_w[l]), row(ln1_b[l]))
        rw, rb = _router_weights(router_group_w[l], router_group_b[l], router_expert_w[l],
                                 router_expert_b[l])
        x2 = _moe(x2, rw, rb, expert_w_gate[l].astype(bf16), expert_w_up[l].astype(bf16),
                  expert_w_down[l].astype(bf16), row(ln2_w[l]), row(ln2_b[l]))
    return x2.reshape(bsz, seq, D_MODEL)
```

```python
import functools

import jax
import jax.numpy as jnp
import numpy as np
from jax import lax
from jax.experimental import pallas as pl
from jax.experimental.pallas import tpu as pltpu

f32 = jnp.float32
bf16 = jnp.bfloat16

D_MODEL = 1024
DEPTH = 4
HEAD_DIM = 64
D_RWKV = 512
RWKV_HEADS = D_RWKV // HEAD_DIM
DECAY_LORA = 64
ICL_LORA = 64
GATE_LORA = 128
RWKV_GN_EPS = 64e-5
D_MOBA = 512
MOBA_HEADS = D_MOBA // HEAD_DIM
MOBA_BLOCK = 256
MOBA_TOPK = 3
D_SSM = 512
SSM_GROUP = 16
SSM_GROUPS = D_SSM // SSM_GROUP
SSM_STATE = 64
N_BRANCHES = 3
N_EXPERT_GROUPS = 4
EXPERTS_PER_GROUP = 8
N_EXPERTS = N_EXPERT_GROUPS * EXPERTS_PER_GROUP
D_EXPERT = D_MODEL // 4
LN_EPS = 1e-5
DEEPNORM_ALPHA = (2 * DEPTH) ** 0.25
NEG_INF = -1e30
RWKV_COLS = 3 * D_RWKV + DECAY_LORA + ICL_LORA + GATE_LORA
MOBA_COLS = 3 * D_MOBA
OFF_MOBA = RWKV_COLS
OFF_SSM = OFF_MOBA + MOBA_COLS
OFF_GATE = OFF_SSM + D_SSM

LANES = 128
MXU_TILE = 256
WKV_CHUNK = 64
WKV_PAIR = 4
VMEM_LIMIT = 48 * 1024 * 1024


def _nt(a, b):
    return lax.dot_general(a, b, (((1,), (1,)), ((), ())), preferred_element_type=f32)


def _tn(a, b):
    return lax.dot_general(a, b, (((0,), (0,)), ((), ())), preferred_element_type=f32)


def _mm(a, b):
    return jnp.dot(a, b, preferred_element_type=f32)


def _split2(x):
    hi = x.astype(bf16)
    lo = (x - hi.astype(f32)).astype(bf16)
    return hi, lo


def _layer_norm(y, w, b):
    mu = jnp.mean(y, axis=-1, keepdims=True)
    d = y - mu
    var = jnp.mean(d * d, axis=-1, keepdims=True)
    return d * lax.rsqrt(var + LN_EPS) * w + b


def _proj_kernel(x_ref, w_ref, o_ref):
    o_ref[...] = _mm(x_ref[...].astype(bf16), w_ref[...])


def _project(x2, w, tm=512):
    m, k = x2.shape
    n = w.shape[1]
    return pl.pallas_call(
        _proj_kernel,
        grid=(m // tm,),
        in_specs=[pl.BlockSpec((tm, k), lambda i: (i, 0)), pl.BlockSpec((k, n), lambda i: (0, 0))],
        out_specs=pl.BlockSpec((tm, n), lambda i: (i, 0)),
        out_shape=jax.ShapeDtypeStruct((m, n), f32),
        compiler_params=pltpu.CompilerParams(
            dimension_semantics=("parallel",), vmem_limit_bytes=VMEM_LIMIT),
    )(x2, w)


def _rwkv_kernel(x_ref, wz_ref, mu_ref, w0_ref, w2_ref, a0_ref, a2_ref, g2_ref, kk_ref, ka_ref, rk_ref,
                 lnw_ref, lnb_ref, seg_ref, tri_ref, o_ref,
                 zlast_s, state_s, r_s, k_s, v_s, a_s, b_s, lw_s, g_s, bon_s, y_s,
                 rt_s, bh_s, kh_s, tw_s, arb_s, pc_s):
    tb = x_ref.shape[0]
    n_chunks = tb // WKV_CHUNK
    c64 = WKV_CHUNK

    @pl.when(pl.program_id(1) == 0)
    def _():
        zlast_s[...] = jnp.zeros_like(zlast_s)
        state_s[...] = jnp.zeros_like(state_s)

    seg = seg_ref[...]

    def segsum(x):
        hi, lo = _split2(x)
        w = seg.shape[0]
        return jnp.concatenate([_mm(hi[:, c:c + w], seg) + _mm(lo[:, c:c + w], seg)
                                for c in range(0, D_RWKV, w)], axis=1)

    z = _mm(x_ref[...].astype(bf16), wz_ref[...])
    row = lax.broadcasted_iota(jnp.int32, z.shape, 0)
    z_prev = jnp.where(row == 0, zlast_s[...], pltpu.roll(z, 1, axis=0))
    zlast_s[...] = z[tb - 1:tb, :]
    zz = z + (z_prev - z) * mu_ref[...]
    r = zz[:, 0:D_RWKV]
    k = zz[:, D_RWKV:2 * D_RWKV]
    v = zz[:, 2 * D_RWKV:3 * D_RWKV]
    o1 = 3 * D_RWKV
    xw = zz[:, o1:o1 + DECAY_LORA]
    xa = zz[:, o1 + DECAY_LORA:o1 + DECAY_LORA + ICL_LORA]
    xg = zz[:, o1 + DECAY_LORA + ICL_LORA:]
    wl = w0_ref[...] + _mm(jnp.tanh(xw).astype(bf16), w2_ref[...])
    lw_s[...] = -float(np.exp(-0.5)) * jax.nn.sigmoid(wl)
    a_icl = jax.nn.sigmoid(a0_ref[...] + _mm(xa.astype(bf16), a2_ref[...]))
    g_s[...] = _mm(jax.nn.sigmoid(xg).astype(bf16), g2_ref[...])
    kk = k * kk_ref[...]
    kk = kk * lax.rsqrt(jnp.maximum(segsum(kk * kk), 1e-24))
    k2 = k * (1.0 + (a_icl - 1.0) * ka_ref[...])
    r_s[...] = r
    k_s[...] = k2
    v_s[...] = v
    a_s[...] = -kk
    b_s[...] = kk * a_icl
    bon_s[...] = segsum(r * k2 * rk_ref[...])

    ri = lax.broadcasted_iota(jnp.int32, (c64, c64), 0)
    ci = lax.broadcasted_iota(jnp.int32, (c64, c64), 1)
    strict = ri > ci
    incl = ri >= ci
    eye = jnp.where(ri == ci, 1.0, 0.0).astype(f32)
    tri = tri_ref[...]

    heads = range(RWKV_HEADS)
    sls = [slice(h * HEAD_DIM, (h + 1) * HEAD_DIM) for h in heads]

    def chunk_terms(c):
        rows = pl.ds(pl.multiple_of(c * c64, c64), c64)
        lwc = lw_s[rows, :]
        lhi, llo = _split2(lwc)
        cl = _mm(tri, lhi) + _mm(tri, llo)
        cl_last = cl[c64 - 1:c64, :]
        e_neg = jnp.exp(-cl)
        e_tot = jnp.exp(cl_last - cl)
        ac = a_s[rows, :]
        bc = b_s[rows, :]
        kc = k_s[rows, :]
        r_t = r_s[rows, :] * jnp.exp(cl)
        rt_s[rows, :] = r_t
        bh_s[rows, :] = bc * e_tot
        kh_s[rows, :] = kc * e_tot
        pc_s[pl.ds(c, 1), :] = jnp.exp(cl_last)
        return rows, ac * jnp.exp(cl - lwc), r_t, bc * e_neg, kc * e_neg, v_s[rows, :]

    def intra_body(c2, carry):
        terms = [chunk_terms(c2 * WKV_PAIR + j) for j in range(WKV_PAIR)]
        probs = [(j, h) for j in range(WKV_PAIR) for h in heads]
        cat = lambda j, p, q, h: jnp.concatenate([terms[j][p][:, sls[h]], terms[j][q][:, sls[h]]],
                                                 axis=0).astype(bf16)
        gm = [_nt(cat(j, 1, 2, h), cat(j, 3, 4, h)) for j, h in probs]
        a_ab = [jnp.where(strict, g[:c64, :c64], 0.0) for g in gm]
        a_ak = [jnp.where(strict, g[:c64, c64:], 0.0).astype(bf16) for g in gm]
        a_rb = [jnp.where(incl, g[c64:, :c64], 0.0).astype(bf16) for g in gm]
        a_rk = [jnp.where(incl, g[c64:, c64:], 0.0).astype(bf16) for g in gm]
        vb = [terms[j][5][:, sls[h]].astype(bf16) for j, h in probs]
        akv = [_mm(a, v) for a, v in zip(a_ak, vb)]
        yloc = [_mm(a, v) for a, v in zip(a_rk, vb)]
        xs = a_ab
        t_inv = [eye + x for x in xs]
        for _ in range(5):
            xb = [x.astype(bf16) for x in xs]
            xs = [_mm(x, x) for x in xb]
            t_inv = [t + _mm(t.astype(bf16), x.astype(bf16)) for t, x in zip(t_inv, xs)]
        rhs = [jnp.concatenate([terms[j][1][:, sls[h]], akv[n]], axis=1).astype(bf16)
               for n, (j, h) in enumerate(probs)]
        tw = [_mm(t.astype(bf16), r) for t, r in zip(t_inv, rhs)]
        for j in range(WKV_PAIR):
            mine = slice(j * RWKV_HEADS, (j + 1) * RWKV_HEADS)
            rows = terms[j][0]
            tw_s[rows, :] = jnp.concatenate(tw[mine], axis=1)
            arb_s[rows, :] = jnp.concatenate(a_rb[mine], axis=1)
            y_s[rows, :] = jnp.concatenate(yloc[mine], axis=1)
        return carry

    lax.fori_loop(0, n_chunks // WKV_PAIR, intra_body, 0)

    def state_body(c, carry):
        rows = pl.ds(pl.multiple_of(c * c64, c64), c64)
        twc = tw_s[rows, :]
        arbc = arb_s[rows, :]
        r_t = rt_s[rows, :]
        b_h = bh_s[rows, :]
        k_h = kh_s[rows, :]
        vc = v_s[rows, :]
        yl = y_s[rows, :]
        p_c = pc_s[pl.ds(c, 1), :]
        s_old = [state_s[h] for h in heads]
        wr = [jnp.concatenate([twc[:, h * LANES:h * LANES + c64], r_t[:, sls[h]]], axis=0).astype(bf16)
              for h in heads]
        wrs = [_nt(wr[h], s_old[h].astype(bf16)) for h in heads]
        u = [wrs[h][:c64] + twc[:, h * LANES + c64:(h + 1) * LANES] for h in heads]
        ys = [wrs[h][c64:] + _mm(arbc[:, sls[h]], u[h].astype(bf16)) + yl[:, sls[h]] for h in heads]
        uv = [jnp.concatenate([u[h], vc[:, sls[h]]], axis=0).astype(bf16) for h in heads]
        bkh = [jnp.concatenate([b_h[:, sls[h]], k_h[:, sls[h]]], axis=0).astype(bf16) for h in heads]
        for h in heads:
            state_s[h] = s_old[h] * p_c[:, sls[h]] + _tn(uv[h], bkh[h])
        y_s[rows, :] = jnp.concatenate(ys, axis=1)
        return carry

    lax.fori_loop(0, n_chunks, state_body, 0)

    y = y_s[...]
    mean = segsum(y) * (1.0 / HEAD_DIM)
    d = y - mean
    var = segsum(d * d) * (1.0 / HEAD_DIM)
    yn = d * lax.rsqrt(var + RWKV_GN_EPS) * lnw_ref[...] + lnb_ref[...]
    o_ref[0] = (yn + bon_s[...] * v_s[...]) * g_s[...]


def _rwkv(x2, wz, p, bsz, seq, tb=512):
    tb = min(tb, seq)
    n_l = seq // tb
    seg = (np.arange(MXU_TILE)[:, None] // HEAD_DIM == np.arange(MXU_TILE)[None, :] // HEAD_DIM)
    seg = jnp.asarray(seg, bf16)
    tri = jnp.asarray(np.tril(np.ones((WKV_CHUNK, WKV_CHUNK))), bf16)
    row = lambda a: a.reshape(1, -1).astype(f32)
    args = [x2, wz, row(p['mu']), row(p['w0']), p['w2'].astype(bf16), row(p['a0']), p['a2'].astype(bf16),
            p['g2'].astype(bf16), row(p['k_k']), row(p['k_a']), row(p['r_k']), row(p['ln_w']),
            row(p['ln_b']), seg, tri]
    full = lambda a: pl.BlockSpec(a.shape, lambda b, l: (0,) * a.ndim)
    in_specs = [pl.BlockSpec((tb, D_MODEL), lambda b, l: (b * n_l + l, 0))] + [full(a) for a in args[1:]]
    big = lambda: pltpu.VMEM((tb, D_RWKV), f32)
    return pl.pallas_call(
        _rwkv_kernel,
        grid=(bsz, n_l),
        in_specs=in_specs,
        out_specs=pl.BlockSpec((1, tb, D_RWKV), lambda b, l: (b, l, 0)),
        out_shape=jax.ShapeDtypeStruct((bsz, seq, D_RWKV), f32),
        scratch_shapes=[pltpu.VMEM((1, RWKV_COLS), f32),
                        pltpu.VMEM((RWKV_HEADS, HEAD_DIM, HEAD_DIM), f32),
                        big(), big(), big(), big(), big(), big(), big(), big(), big(),
                        big(), big(), big(), pltpu.VMEM((tb, RWKV_HEADS * LANES), f32),
                        pltpu.VMEM((tb, D_RWKV), bf16), pltpu.VMEM((tb // WKV_CHUNK, D_RWKV), f32)],
        compiler_params=pltpu.CompilerParams(
            dimension_semantics=("parallel", "arbitrary"), vmem_limit_bytes=VMEM_LIMIT),
    )(*args)


MOBA_AUG_POS = 0
MOBA_AUG_BLK = 6
LOG2E = 1.4426950408889634


MOBA_STEP_HEADS = 4


def _moba_kernel(q_ref, k_ref, v_ref, tmpl_ref, o_ref, kaug_s, vaug_s, kmean_s):
    blk = MOBA_BLOCK
    half = HEAD_DIM
    seq = k_ref.shape[1]
    n_blk = seq // blk
    qi = pl.program_id(2)
    heads = range(MOBA_STEP_HEADS)
    tile = [slice((hh // 2) * LANES, (hh // 2 + 1) * LANES) for hh in heads]
    aug0 = [half if hh % 2 == 0 else 0 for hh in heads]

    @pl.when(qi == 0)
    def _():
        lane_l = lax.broadcasted_iota(jnp.int32, (seq, LANES), 1)
        for hh in heads:
            kb = k_ref[0, :, tile[hh]].astype(bf16)
            vb = v_ref[0, :, tile[hh]].astype(bf16)
            in_head = (lane_l < half) if hh % 2 == 0 else (lane_l >= half)
            kaug_s[hh] = jnp.where(in_head, kb, tmpl_ref[0, hh])
            ones_lane = jnp.where(lane_l == aug0[hh], 1.0, 0.0).astype(bf16)
            vaug_s[hh] = jnp.where(in_head, vb, ones_lane)
        for n in range(n_blk):
            km = jnp.mean(k_ref[0, n * blk:(n + 1) * blk, :], axis=0, keepdims=True)
            for tt in range(MOBA_STEP_HEADS // 2):
                kmean_s[tt, n:n + 1, :] = km[:, tt * LANES:(tt + 1) * LANES]

    ri = lax.broadcasted_iota(jnp.int32, (blk, blk), 0)
    ci = lax.broadcasted_iota(jnp.int32, (blk, blk), 1)
    causal = ri >= ci
    lane = lax.broadcasted_iota(jnp.int32, (blk, LANES), 1)
    lane_f = lane.astype(f32)
    qf = [q_ref[0, :, tile[hh]] * (HEAD_DIM ** -0.5 * LOG2E) for hh in heads]
    in_head = [(lane < half) if hh % 2 == 0 else (lane >= half) for hh in heads]
    blk_lane = [lane - (aug0[hh] + MOBA_AUG_BLK) for hh in heads]

    qs = [_split2(jnp.where(in_head[hh], qf[hh], 0.0)) for hh in heads]
    kms = [_split2(kmean_s[tt]) for tt in range(MOBA_STEP_HEADS // 2)]
    gate = [_nt(kms[hh // 2][0], qs[hh][0]) + _nt(kms[hh // 2][1], qs[hh][0])
            + _nt(kms[hh // 2][0], qs[hh][1]) for hh in heads]
    is_pos = [(lane >= aug0[hh] + MOBA_AUG_POS) & (lane < aug0[hh] + MOBA_AUG_POS + 3) for hh in heads]
    own = [jnp.where(is_pos[hh], 1.0, 0.0) for hh in heads]
    q_own = [jnp.where(in_head[hh], qf[hh], own[hh]).astype(bf16) for hh in heads]

    start = pl.multiple_of(qi * blk, blk)
    s0 = [_nt(q_own[hh], kaug_s[hh, pl.ds(start, blk), :]) for hh in heads]
    s0 = [jnp.where(causal, s, NEG_INF) for s in s0]
    m0 = [jnp.max(s, axis=-1, keepdims=True) for s in s0]
    p0 = [jnp.exp2(s - m).astype(bf16) for s, m in zip(s0, m0)]
    a0 = [_mm(p0[hh], vaug_s[hh, pl.ds(start, blk), :]) for hh in heads]

    brow = lax.broadcasted_iota(jnp.int32, (n_blk, blk), 0)
    brow_f = brow.astype(f32)
    g = [jnp.where(brow < qi, x, -jnp.inf) for x in gate]
    picked = [jnp.zeros((n_blk, blk), f32) for _ in heads]
    for _ in range(MOBA_TOPK):
        mx = [jnp.max(x, axis=0, keepdims=True) for x in g]
        idx = [jnp.min(jnp.where(g[hh] == mx[hh], brow_f, 1e9), axis=0, keepdims=True) for hh in heads]
        hit = [brow_f == i for i in idx]
        picked = [jnp.where(hit[hh], 1.0, picked[hh]) for hh in heads]
        g = [jnp.where(hit[hh], -jnp.inf, g[hh]) for hh in heads]
    sel_t = [jnp.where(brow < qi, p, 0.0).astype(bf16) for p in picked]
    lane_b = lax.broadcasted_iota(jnp.int32, (n_blk, LANES), 1)
    brow_b = lax.broadcasted_iota(jnp.int32, (n_blk, LANES), 0)
    q_aug = []
    for hh in heads:
        place = jnp.where(lane_b == brow_b + (aug0[hh] + MOBA_AUG_BLK), 1.0, 0.0).astype(bf16)
        sel = _tn(sel_t[hh], place) > 0.5
        is_blk = (blk_lane[hh] >= 0) & (blk_lane[hh] < n_blk)
        aug = jnp.where(is_blk, jnp.where(sel, 0.0, NEG_INF), own[hh])
        q_aug.append(jnp.where(in_head[hh], qf[hh], aug).astype(bf16))

    def make_body(width, base):
        def body(t, carry):
            ms, accs = carry
            rows = pl.ds(pl.multiple_of(base + t * (width * blk), 2 * blk), width * blk)
            s = [_nt(q_aug[hh], kaug_s[hh, rows, :]) for hh in heads]
            m_new = [jnp.maximum(ms[hh], jnp.max(s[hh], axis=-1, keepdims=True)) for hh in heads]
            p = [jnp.exp2(s[hh] - m_new[hh]).astype(bf16) for hh in heads]
            pv = [_mm(p[hh], vaug_s[hh, rows, :]) for hh in heads]
            accs = [accs[hh] * jnp.exp2(ms[hh] - m_new[hh]) + pv[hh] for hh in heads]
            return (tuple(m_new), tuple(accs))
        return body

    n_pairs = (qi + 1) // 2
    n_quads = n_pairs // 2
    carry = lax.fori_loop(0, n_quads, make_body(4, 0), (tuple(m0), tuple(a0)))
    _, accs = lax.fori_loop(0, n_pairs - 2 * n_quads, make_body(2, n_quads * (4 * blk)), carry)
    out = [accs[hh] / accs[hh][:, aug0[hh]:aug0[hh] + 1] for hh in heads]
    for tt in range(MOBA_STEP_HEADS // 2):
        o_ref[0, :, tt * LANES:(tt + 1) * LANES] = jnp.where(lane < half, out[2 * tt], out[2 * tt + 1])


def _moba_template(seq):
    pos = np.arange(seq, dtype=np.float64)
    tmpl = np.zeros((MOBA_HEADS, seq, LANES), np.float32)
    for h in range(MOBA_HEADS):
        a0 = HEAD_DIM if h % 2 == 0 else 0
        slope = 2.0 ** (-8.0 * (h + 1) / MOBA_HEADS)
        rest = slope * LOG2E * pos
        for c in range(3):
            part = rest.astype(np.float32).astype(bf16).astype(np.float64)
            tmpl[h, :, a0 + MOBA_AUG_POS + c] = part
            rest = rest - part
        tmpl[h, np.arange(seq), a0 + MOBA_AUG_BLK + np.arange(seq) // MOBA_BLOCK] = 1.0
    return jnp.asarray(tmpl.reshape(MOBA_HEADS // MOBA_STEP_HEADS, MOBA_STEP_HEADS, seq, LANES), bf16)


def _moba(qkv3):
    bsz, seq, _ = qkv3.shape
    blk = MOBA_BLOCK
    sh = MOBA_STEP_HEADS
    n_grp = MOBA_HEADS // sh
    wid = sh * HEAD_DIM
    assert seq % (2 * blk) == 0 and seq // blk <= HEAD_DIM - MOBA_AUG_BLK
    return pl.pallas_call(
        _moba_kernel,
        grid=(bsz, n_grp, seq // blk),
        in_specs=[pl.BlockSpec((1, blk, wid), lambda b, p, i: (b, i, p)),
                  pl.BlockSpec((1, seq, wid), lambda b, p, i: (b, 0, n_grp + p)),
                  pl.BlockSpec((1, seq, wid), lambda b, p, i: (b, 0, 2 * n_grp + p)),
                  pl.BlockSpec((1, sh, seq, LANES), lambda b, p, i: (p, 0, 0, 0))],
        out_specs=pl.BlockSpec((1, blk, wid), lambda b, p, i: (b, i, p)),
        out_shape=jax.ShapeDtypeStruct((bsz, seq, D_MOBA), f32),
        scratch_shapes=[pltpu.VMEM((sh, seq, LANES), bf16), pltpu.VMEM((sh, seq, LANES), bf16),
                        pltpu.VMEM((sh // 2, seq // blk, LANES), f32)],
        compiler_params=pltpu.CompilerParams(
            dimension_semantics=("parallel", "parallel", "arbitrary"), vmem_limit_bytes=VMEM_LIMIT),
    )(qkv3, qkv3, qkv3, _moba_template(seq))


S5_SUPER = 4
S5_SEGS = 8
S5_COLS = 512


def _s5_kernel(x_ref, wu_ref, perm_ref, bd_ref, cre_ref, cim_ref, lam_ref, d_ref, gw_ref, gb_ref, o_ref,
               hr_s, hi_s, carry_s):
    tb = x_ref.shape[0]
    seg = tb // S5_SEGS
    n_state = lam_ref.shape[1]
    sgw = n_state // S5_SUPER
    uw = D_SSM // S5_SUPER

    @pl.when(pl.program_id(1) == 0)
    def _():
        carry_s[...] = jnp.zeros_like(carry_s)

    u = _mm(x_ref[...].astype(bf16), wu_ref[...])
    perm = perm_ref[...]
    ub = _mm(perm, u.astype(bf16)).astype(bf16)
    for sg in range(S5_SUPER):
        bu = _mm(ub[:, sg * uw:(sg + 1) * uw], bd_ref[sg])
        for j in range(sgw // LANES):
            hr_s[sg * (sgw // LANES) + j] = bu[:, j * LANES:(j + 1) * LANES]
            hi_s[sg * (sgw // LANES) + j] = bu[:, sgw + j * LANES:sgw + (j + 1) * LANES]

    tpc = S5_COLS // LANES

    def load(ref, cc, idx):
        return jnp.concatenate([ref[cc * tpc + j, idx, :] for j in range(tpc)], axis=1)

    def store(ref, cc, idx, val):
        for j in range(tpc):
            ref[cc * tpc + j, idx, :] = val[:, j * LANES:(j + 1) * LANES]

    for cc in range(n_state // S5_COLS):
        cols = slice(cc * S5_COLS, (cc + 1) * S5_COLS)
        lr = lam_ref[0:1, cols]
        li = lam_ref[1:2, cols]

        def local(r, carry, cc=cc, lr=lr, li=li):
            hr, hi = carry
            idx = pl.ds(pl.multiple_of(r * S5_SEGS, S5_SEGS), S5_SEGS)
            nr = lr * hr - li * hi + load(hr_s, cc, idx)
            ni = lr * hi + li * hr + load(hi_s, cc, idx)
            store(hr_s, cc, idx, nr)
            store(hi_s, cc, idx, ni)
            return nr, ni

        zero = jnp.zeros((S5_SEGS, S5_COLS), f32)
        er, ei = lax.fori_loop(0, seg, local, (zero, zero), unroll=4)
        pr, pi = lr, li
        for _ in range(seg.bit_length() - 1):
            pr, pi = pr * pr - pi * pi, 2.0 * pr * pi
        cr = carry_s[0:1, cols]
        ci = carry_s[1:2, cols]
        in_r, in_i = [], []
        for s in range(S5_SEGS):
            in_r.append(cr)
            in_i.append(ci)
            cr, ci = pr * cr - pi * ci + er[s:s + 1, :], pr * ci + pi * cr + ei[s:s + 1, :]
        carry_s[0:1, cols] = cr
        carry_s[1:2, cols] = ci

        def fix(r, carry, cc=cc, lr=lr, li=li):
            qr, qi = carry
            qr, qi = lr * qr - li * qi, lr * qi + li * qr
            idx = pl.ds(pl.multiple_of(r * S5_SEGS, S5_SEGS), S5_SEGS)
            store(hr_s, cc, idx, load(hr_s, cc, idx) + qr)
            store(hi_s, cc, idx, load(hi_s, cc, idx) + qi)
            return qr, qi

        lax.fori_loop(0, seg, fix, (jnp.concatenate(in_r, axis=0), jnp.concatenate(in_i, axis=0)),
                      unroll=4)

    slab = lambda ref, sg: jnp.concatenate(
        [ref[sg * (sgw // LANES) + j] for j in range(sgw // LANES)], axis=1).astype(bf16)
    ys = [_mm(slab(hr_s, sg), cre_ref[sg]) + _mm(slab(hi_s, sg), cim_ref[sg]) for sg in range(S5_SUPER)]
    yh, yl = _split2(jnp.concatenate(ys, axis=1))
    y = _tn(perm, yh) + _tn(perm, yl) + d_ref[...] * u
    y = 0.5 * y * (1.0 + jnp.tanh(0.7978845608028654 * (y + 0.044715 * (y * y * y))))
    zg = _mm(y.astype(bf16), gw_ref[...]) + gb_ref[...]
    o_ref[0] = zg[:, :D_SSM] * jax.nn.sigmoid(zg[:, D_SSM:])


def _s5_params(p):
    a_re = p['a_re'].astype(f32)
    a_im = p['a_im'].astype(f32)
    dt = jnp.exp(p['log_dt'].astype(f32))[:, None]
    mag = jnp.exp(a_re * dt)
    lam_re = mag * jnp.cos(a_im * dt)
    lam_im = mag * jnp.sin(a_im * dt)
    den = a_re * a_re + a_im * a_im
    nr = lam_re - 1.0
    coef_re = (nr * a_re + lam_im * a_im) / den
    coef_im = (lam_im * a_re - nr * a_im) / den
    b_re = p['b_re'].astype(f32)
    b_im = p['b_im'].astype(f32)
    bb_re = coef_re[..., None] * b_re - coef_im[..., None] * b_im
    bb_im = coef_re[..., None] * b_im + coef_im[..., None] * b_re
    gps = SSM_GROUPS // S5_SUPER
    eye = jnp.eye(gps, dtype=f32)
    n_state = SSM_GROUPS * SSM_STATE
    sgw = n_state // S5_SUPER
    uw = D_SSM // S5_SUPER

    def in_map(bb):
        bb = bb.reshape(S5_SUPER, gps, SSM_STATE, SSM_GROUP)
        return jnp.einsum('sgph,gk->sghkp', bb, eye).reshape(S5_SUPER, uw, sgw)

    def out_map(c):
        c = c.reshape(S5_SUPER, gps, SSM_GROUP, SSM_STATE)
        return jnp.einsum('sghp,gk->sgpkh', c, eye).reshape(S5_SUPER, sgw, uw)

    bd = jnp.concatenate([in_map(bb_re), in_map(bb_im)], axis=2)
    cre = out_map(p['c_re'].astype(f32))
    cim = -out_map(p['c_im'].astype(f32))
    lam = jnp.stack([lam_re.reshape(n_state), lam_im.reshape(n_state)], axis=0)
    return bd.astype(bf16), cre.astype(bf16), cim.astype(bf16), lam


def _s5(x2, wu, p, bsz, seq, tb=512):
    tb = min(tb, seq)
    n_l = seq // tb
    seg = tb // S5_SEGS
    assert tb % S5_SEGS == 0 and seg & (seg - 1) == 0
    n_state = SSM_GROUPS * SSM_STATE
    bd, cre, cim, lam = _s5_params(p)
    src = (np.arange(tb) % S5_SEGS) * seg + np.arange(tb) // S5_SEGS
    perm = jnp.asarray(np.arange(tb)[None, :] == src[:, None], bf16)
    args = [x2, wu, perm, bd, cre, cim, lam, p['d'].reshape(1, -1).astype(f32), p['glu_w'].astype(bf16),
            p['glu_b'].reshape(1, -1).astype(f32)]
    full = lambda a: pl.BlockSpec(a.shape, lambda b, l: (0,) * a.ndim)
    return pl.pallas_call(
        _s5_kernel,
        grid=(bsz, n_l),
        in_specs=[pl.BlockSpec((tb, D_MODEL), lambda b, l: (b * n_l + l, 0))] + [full(a) for a in args[1:]],
        out_specs=pl.BlockSpec((1, tb, D_SSM), lambda b, l: (b, l, 0)),
        out_shape=jax.ShapeDtypeStruct((bsz, seq, D_SSM), f32),
        scratch_shapes=[pltpu.VMEM((n_state // LANES, tb, LANES), f32),
                        pltpu.VMEM((n_state // LANES, tb, LANES), f32),
                        pltpu.VMEM((2, n_state), f32)],
        compiler_params=pltpu.CompilerParams(
            dimension_semantics=("parallel", "arbitrary"), vmem_limit_bytes=VMEM_LIMIT),
    )(*args)


def _merge_kernel(x_ref, ya_ref, yb_ref, yc_ref, wg_ref, gb_ref, wa_ref, wb_ref, wc_ref, wo_ref,
                  lnw_ref, lnb_ref, o_ref):
    x = x_ref[...]
    xb = x.astype(bf16)
    merged = None
    for br, (y_ref, w_ref) in enumerate(((ya_ref, wa_ref), (yb_ref, wb_ref), (yc_ref, wc_ref))):
        cols = slice(br * D_MODEL, (br + 1) * D_MODEL)
        gate = jax.nn.sigmoid(_mm(xb, wg_ref[:, cols]) + gb_ref[:, cols])
        term = gate * _mm(y_ref[...].astype(bf16), w_ref[...])
        merged = term if merged is None else merged + term
    h = _mm(merged.astype(bf16), wo_ref[...])
    o_ref[...] = _layer_norm(DEEPNORM_ALPHA * x + h, lnw_ref[...], lnb_ref[...])


def _merge(x2, ya, yb, yc, wg, gb, wa, wb, wc, wo, lnw, lnb, tm=256):
    m = x2.shape[0]
    rowblk = lambda n: pl.BlockSpec((tm, n), lambda i: (i, 0))
    full = lambda a: pl.BlockSpec(a.shape, lambda i: (0,) * a.ndim)
    consts = [wg, gb, wa, wb, wc, wo, lnw, lnb]
    return pl.pallas_call(
        _merge_kernel,
        grid=(m // tm,),
        in_specs=[rowblk(D_MODEL), rowblk(D_RWKV), rowblk(D_MOBA), rowblk(D_SSM)]
        + [full(a) for a in consts],
        out_specs=rowblk(D_MODEL),
        out_shape=jax.ShapeDtypeStruct((m, D_MODEL), f32),
        compiler_params=pltpu.CompilerParams(
            dimension_semantics=("parallel",), vmem_limit_bytes=VMEM_LIMIT),
    )(x2, ya, yb, yc, *consts)


MOE_STEP_EXPERTS = 4
MOE_CHUNK = 256
MOE_STEPS = N_EXPERTS // MOE_STEP_EXPERTS
STEPS_PER_GROUP = EXPERTS_PER_GROUP // MOE_STEP_EXPERTS


def _moe_kernel(x_ref, rwh_ref, rwl_ref, rb_ref, wg_ref, wu_ref, wd_ref, lnw_ref, lnb_ref, o_ref,
                xs_s, outs_s, wts_s, pos_s, posrow_s, meta_s):
    s = pl.program_id(1)
    tm = x_ref.shape[0]
    ck = MOE_CHUNK
    lane = lax.broadcasted_iota(jnp.int32, (tm, LANES), 1)
    lane_f = lane.astype(f32)

    @pl.when(s == 0)
    def _():
        xh, xl = _split2(x_ref[...])
        rwh = rwh_ref[...]
        logits = _mm(xh, rwh) + _mm(xh, rwl_ref[...]) + _mm(xl, rwh) + rb_ref[...]
        is_grp = (lane >= N_EXPERTS) & (lane < N_EXPERTS + N_EXPERT_GROUPS)
        gl = jnp.where(is_grp, logits, -jnp.inf)
        gmax = jnp.max(gl, axis=-1, keepdims=True)
        gidx = jnp.min(jnp.where(gl == gmax, lane_f, 1e9), axis=-1, keepdims=True) - float(N_EXPERTS)
        p_group = 1.0 / jnp.sum(jnp.where(is_grp, jnp.exp(gl - gmax), 0.0), axis=-1, keepdims=True)
        grp_of_lane = jnp.floor(lane_f * (1.0 / EXPERTS_PER_GROUP))
        in_grp = (lane < N_EXPERTS) & (grp_of_lane == gidx)
        el = jnp.where(in_grp, logits, -jnp.inf)
        m1 = jnp.max(el, axis=-1, keepdims=True)
        i1 = jnp.min(jnp.where(el == m1, lane_f, 1e9), axis=-1, keepdims=True)
        el2 = jnp.where(lane_f == i1, -jnp.inf, el)
        m2 = jnp.max(el2, axis=-1, keepdims=True)
        i2 = jnp.min(jnp.where(el2 == m2, lane_f, 1e9), axis=-1, keepdims=True)
        e2 = jnp.exp(m2 - m1)
        w1 = p_group / (1.0 + e2)
        w2 = p_group * e2 / (1.0 + e2)
        wt = jnp.where(lane_f == i1, w1, 0.0) + jnp.where(lane_f == i2, w2, 0.0)

        ohg = lane_f == gidx
        ri = lax.broadcasted_iota(jnp.int32, (tm, tm), 0)
        ci = lax.broadcasted_iota(jnp.int32, (tm, tm), 1)
        tri = jnp.where(ri >= ci, 1.0, 0.0).astype(bf16)
        cg = _mm(tri, jnp.where(ohg, 1.0, 0.0).astype(bf16))
        cnt = cg[tm - 1:tm, :]
        padded = jnp.ceil(cnt * (1.0 / ck)) * ck
        ui = lax.broadcasted_iota(jnp.int32, (LANES, LANES), 0)
        uj = lax.broadcasted_iota(jnp.int32, (LANES, LANES), 1)
        upper = jnp.where(ui < uj, 1.0, 0.0).astype(bf16)
        off = _mm(jnp.broadcast_to(padded, (8, LANES)).astype(bf16), upper)[0:1, :]
        rank = jnp.sum(jnp.where(ohg, cg - 1.0, 0.0), axis=-1, keepdims=True)
        pos = jnp.sum(jnp.where(ohg, off, 0.0), axis=-1, keepdims=True) + rank
        pos_s[...] = jnp.broadcast_to(pos, (tm, LANES))
        pa = jnp.floor(pos * (1.0 / 64.0))
        pb = pos - 64.0 * pa
        cols = jnp.where(lane == 0, pa, jnp.where(lane == 1, pb, 0.0)).astype(bf16)
        r8 = lax.broadcasted_iota(jnp.int32, (8, LANES), 0)
        l8 = lax.broadcasted_iota(jnp.int32, (8, LANES), 1)
        sel8 = jnp.where((r8 == 0) & (l8 == 0), 64.0, jnp.where((r8 == 0) & (l8 == 1), 1.0, 0.0))
        posrow_s[...] = _nt(sel8.astype(bf16), cols)
        off_i = off.astype(jnp.int32)
        nck_i = (padded * (1.0 / ck)).astype(jnp.int32)
        total = 0
        for g in range(N_EXPERT_GROUPS):
            meta_s[g] = off_i[0, g]
            meta_s[N_EXPERT_GROUPS + g] = nck_i[0, g]
            total = total + nck_i[0, g]
        meta_s[2 * N_EXPERT_GROUPS] = total

        wth, wtl = _split2(wt)
        rows_f = lax.broadcasted_iota(jnp.int32, (ck, tm), 0).astype(f32)
        posrow = posrow_s[0:1, :]

        def fill(c, carry):
            r0 = pl.multiple_of(c * ck, ck)
            pc = jnp.where(rows_f + r0.astype(f32) == posrow, 1.0, 0.0).astype(bf16)
            xs_s[pl.ds(r0, ck), :] = _mm(pc, xh).astype(bf16)
            wts_s[pl.ds(r0, ck), :] = _mm(pc, wth) + _mm(pc, wtl)
            outs_s[pl.ds(r0, ck), :] = jnp.zeros((ck, D_MODEL), f32)
            return carry

        lax.fori_loop(0, total, fill, 0)

    grp = s // STEPS_PER_GROUP
    off_g = meta_s[grp]
    lane_c = lax.broadcasted_iota(jnp.int32, (ck, LANES), 1)

    def chunk(c, carry):
        rows = pl.ds(pl.multiple_of(off_g + c * ck, ck), ck)
        xs = xs_s[rows, :]
        wts = wts_s[rows, :]
        gact = [_mm(xs, wg_ref[j]) for j in range(MOE_STEP_EXPERTS)]
        up = [_mm(xs, wu_ref[j]) for j in range(MOE_STEP_EXPERTS)]
        part = None
        for j in range(MOE_STEP_EXPERTS):
            wcol = jnp.sum(jnp.where(lane_c == s * MOE_STEP_EXPERTS + j, wts, 0.0), axis=-1, keepdims=True)
            hid = gact[j] * jax.nn.sigmoid(gact[j]) * up[j] * wcol
            out = _mm(hid.astype(bf16), wd_ref[j])
            part = out if part is None else part + out
        outs_s[rows, :] += part
        return carry

    lax.fori_loop(0, meta_s[N_EXPERT_GROUPS + grp], chunk, 0)

    @pl.when(s == MOE_STEPS - 1)
    def _():
        o_ref[...] = DEEPNORM_ALPHA * x_ref[...]
        pos_col = pos_s[:, 0:1]
        cols_f = lax.broadcasted_iota(jnp.int32, (tm, ck), 1).astype(f32)

        def unsort(c, carry):
            r0 = pl.multiple_of(c * ck, ck)
            pt = jnp.where(cols_f + r0.astype(f32) == pos_col, 1.0, 0.0).astype(bf16)
            o_ref[...] += _mm(pt, outs_s[pl.ds(r0, ck), :].astype(bf16))
            return carry

        lax.fori_loop(0, meta_s[2 * N_EXPERT_GROUPS], unsort, 0)
        o_ref[...] = _layer_norm(o_ref[...], lnw_ref[...], lnb_ref[...])


def _moe(x2, rw, rb, wg, wu, wd, lnw, lnb, tm=1024):
    m = x2.shape[0]
    tm = min(tm, m)
    rwh, rwl = _split2(rw)
    n_rows = tm + N_EXPERT_GROUPS * MOE_CHUNK
    full = lambda a: pl.BlockSpec(a.shape, lambda i, e: (0,) * a.ndim)
    return pl.pallas_call(
        _moe_kernel,
        grid=(m // tm, MOE_STEPS),
        in_specs=[pl.BlockSpec((tm, D_MODEL), lambda i, e: (i, 0)),
                  full(rwh), full(rwl), full(rb),
                  pl.BlockSpec((MOE_STEP_EXPERTS, D_MODEL, D_EXPERT), lambda i, e: (e, 0, 0)),
                  pl.BlockSpec((MOE_STEP_EXPERTS, D_MODEL, D_EXPERT), lambda i, e: (e, 0, 0)),
                  pl.BlockSpec((MOE_STEP_EXPERTS, D_EXPERT, D_MODEL), lambda i, e: (e, 0, 0)),
                  full(lnw), full(lnb)],
        out_specs=pl.BlockSpec((tm, D_MODEL), lambda i, e: (i, 0)),
        out_shape=jax.ShapeDtypeStruct((m, D_MODEL), f32),
        scratch_shapes=[pltpu.VMEM((n_rows, D_MODEL), bf16), pltpu.VMEM((n_rows, D_MODEL), f32),
                        pltpu.VMEM((n_rows, LANES), f32), pltpu.VMEM((tm, LANES), f32),
                        pltpu.VMEM((8, tm), f32), pltpu.SMEM((2 * N_EXPERT_GROUPS + 1,), jnp.int32)],
        compiler_params=pltpu.CompilerParams(
            dimension_semantics=("parallel", "arbitrary"), vmem_limit_bytes=VMEM_LIMIT),
    )(x2, rwh, rwl, rb, wg, wu, wd, lnw, lnb)


def _router_weights(router_group_w, router_group_b, router_expert_w, router_expert_b):
    pad = LANES - N_EXPERTS - N_EXPERT_GROUPS
    rw = jnp.concatenate([router_expert_w.astype(f32), router_group_w.astype(f32),
                          jnp.zeros((D_MODEL, pad), f32)], axis=1)
    rb = jnp.concatenate([router_expert_b.astype(f32), router_group_b.astype(f32),
                          jnp.zeros((pad,), f32)]).reshape(1, LANES)
    return rw, rb


def kernel(x, w_in, rwkv_mu, rwkv_w0, rwkv_w2, rwkv_a0, rwkv_a2, rwkv_g2, rwkv_k_k, rwkv_k_a, rwkv_r_k, rwkv_ln_w, rwkv_ln_b, ssm_a_re, ssm_a_im, ssm_b_re, ssm_b_im, ssm_c_re, ssm_c_im, ssm_d, ssm_log_dt, ssm_glu_w, ssm_glu_b, w_up_rwkv, w_up_moba, w_up_ssm, gate_b, w_out, ln1_w, ln1_b, router_group_w, router_group_b, router_expert_w, router_expert_b, expert_w_gate, expert_w_up, expert_w_down, ln2_w, ln2_b):
    bsz, seq, _ = x.shape
    x2 = x.reshape(bsz * seq, D_MODEL).astype(f32)
    row = lambda a: a.reshape(1, -1).astype(f32)
    for l in range(DEPTH):
        w = w_in[l].astype(bf16)
        qkv = _project(x2, w[:, OFF_MOBA:OFF_SSM])
        y_a = _rwkv(x2, w[:, :OFF_MOBA],
                    dict(mu=rwkv_mu[l], w0=rwkv_w0[l], w2=rwkv_w2[l], a0=rwkv_a0[l], a2=rwkv_a2[l],
                         g2=rwkv_g2[l], k_k=rwkv_k_k[l], k_a=rwkv_k_a[l], r_k=rwkv_r_k[l],
                         ln_w=rwkv_ln_w[l], ln_b=rwkv_ln_b[l]), bsz, seq)
        y_b = _moba(qkv.reshape(bsz, seq, MOBA_COLS))
        y_c = _s5(x2, w[:, OFF_SSM:OFF_GATE],
                  dict(a_re=ssm_a_re[l], a_im=ssm_a_im[l], b_re=ssm_b_re[l], b_im=ssm_b_im[l],
                       c_re=ssm_c_re[l], c_im=ssm_c_im[l], d=ssm_d[l], log_dt=ssm_log_dt[l],
                       glu_w=ssm_glu_w[l], glu_b=ssm_glu_b[l]), bsz, seq)
        x2 = _merge(x2, y_a.reshape(-1, D_RWKV), y_b.reshape(-1, D_MOBA), y_c.reshape(-1, D_SSM),
                    w[:, OFF_GATE:], row(gate_b[l]),
                    w_up_rwkv[l].astype(bf16), w_up_moba[l].astype(bf16), w_up_ssm[l].astype(bf16),
                    w_out[l].astype(bf16), row(ln1_w[l]), row(ln1_b[l]))
        rw, rb = _router_weights(router_group_w[l], router_group_b[l], router_expert_w[l],
                                 router_expert_b[l])
        x2 = _moe(x2, rw, rb, expert_w_gate[l].astype(bf16), expert_w_up[l].astype(bf16),
                  expert_w_down[l].astype(bf16), row(ln2_w[l]), row(ln2_b[l]))
    return x2.reshape(bsz, seq, D_MODEL)
```

```python
import functools

import jax
import jax.numpy as jnp
import numpy as np
from jax import lax
from jax.experimental import pallas as pl
from jax.experimental.pallas import tpu as pltpu

f32 = jnp.float32
bf16 = jnp.bfloat16

D_MODEL = 1024
DEPTH = 4
HEAD_DIM = 64
D_RWKV = 512
RWKV_HEADS = D_RWKV // HEAD_DIM
DECAY_LORA = 64
ICL_LORA = 64
GATE_LORA = 128
RWKV_GN_EPS = 64e-5
D_MOBA = 512
MOBA_HEADS = D_MOBA // HEAD_DIM
MOBA_BLOCK = 256
MOBA_TOPK = 3
D_SSM = 512
SSM_GROUP = 16
SSM_GROUPS = D_SSM // SSM_GROUP
SSM_STATE = 64
N_BRANCHES = 3
N_EXPERT_GROUPS = 4
EXPERTS_PER_GROUP = 8
N_EXPERTS = N_EXPERT_GROUPS * EXPERTS_PER_GROUP
D_EXPERT = D_MODEL // 4
LN_EPS = 1e-5
DEEPNORM_ALPHA = (2 * DEPTH) ** 0.25
NEG_INF = -1e30
RWKV_COLS = 3 * D_RWKV + DECAY_LORA + ICL_LORA + GATE_LORA
MOBA_COLS = 3 * D_MOBA
OFF_MOBA = RWKV_COLS
OFF_SSM = OFF_MOBA + MOBA_COLS
OFF_GATE = OFF_SSM + D_SSM

LANES = 128
MXU_TILE = 256
WKV_CHUNK = 64
WKV_PAIR = 4
VMEM_LIMIT = 48 * 1024 * 1024


def _nt(a, b):
    return lax.dot_general(a, b, (((1,), (1,)), ((), ())), preferred_element_type=f32)


def _tn(a, b):
    return lax.dot_general(a, b, (((0,), (0,)), ((), ())), preferred_element_type=f32)


def _mm(a, b):
    return jnp.dot(a, b, preferred_element_type=f32)


def _split2(x):
    hi = x.astype(bf16)
    lo = (x - hi.astype(f32)).astype(bf16)
    return hi, lo


def _layer_norm(y, w, b):
    mu = jnp.mean(y, axis=-1, keepdims=True)
    d = y - mu
    var = jnp.mean(d * d, axis=-1, keepdims=True)
    return d * lax.rsqrt(var + LN_EPS) * w + b


def _proj_kernel(x_ref, w_ref, o_ref):
    o_ref[...] = _mm(x_ref[...].astype(bf16), w_ref[0])


def _w_window(layer, off, width):
    return pl.BlockSpec((pl.Element(1), pl.Element(D_MODEL), pl.Element(width)),
                        lambda *_: (layer, 0, off))


def _project(x2, w_all, layer, off, n, tm=1024):
    m, k = x2.shape
    return pl.pallas_call(
        _proj_kernel,
        grid=(m // tm,),
        in_specs=[pl.BlockSpec((tm, k), lambda i: (i, 0)), _w_window(layer, off, n)],
        out_specs=pl.BlockSpec((tm, n), lambda i: (i, 0)),
        out_shape=jax.ShapeDtypeStruct((m, n), f32),
        compiler_params=pltpu.CompilerParams(
            dimension_semantics=("parallel",), vmem_limit_bytes=VMEM_LIMIT),
    )(x2, w_all)


def _rwkv_kernel(x_ref, wz_ref, mu_ref, w0_ref, w2_ref, a0_ref, a2_ref, g2_ref, kk_ref, ka_ref, rk_ref,
                 lnw_ref, lnb_ref, seg_ref, tri_ref, o_ref,
                 zlast_s, state_s, r_s, k_s, v_s, a_s, b_s, lw_s, g_s, bon_s, y_s,
                 rt_s, bh_s, kh_s, tw_s, arb_s, pc_s):
    tb = x_ref.shape[0]
    n_chunks = tb // WKV_CHUNK
    c64 = WKV_CHUNK

    @pl.when(pl.program_id(1) == 0)
    def _():
        zlast_s[...] = jnp.zeros_like(zlast_s)
        state_s[...] = jnp.zeros_like(state_s)

    seg = seg_ref[...]

    def segsum(x):
        hi, lo = _split2(x)
        w = seg.shape[0]
        return jnp.concatenate([_mm(hi[:, c:c + w], seg) + _mm(lo[:, c:c + w], seg)
                                for c in range(0, D_RWKV, w)], axis=1)

    z = _mm(x_ref[...].astype(bf16), wz_ref[0])
    row = lax.broadcasted_iota(jnp.int32, z.shape, 0)
    z_prev = jnp.where(row == 0, zlast_s[...], pltpu.roll(z, 1, axis=0))
    zlast_s[...] = z[tb - 1:tb, :]
    zz = z + (z_prev - z) * mu_ref[...]
    r = zz[:, 0:D_RWKV]
    k = zz[:, D_RWKV:2 * D_RWKV]
    v = zz[:, 2 * D_RWKV:3 * D_RWKV]
    o1 = 3 * D_RWKV
    xw = zz[:, o1:o1 + DECAY_LORA]
    xa = zz[:, o1 + DECAY_LORA:o1 + DECAY_LORA + ICL_LORA]
    xg = zz[:, o1 + DECAY_LORA + ICL_LORA:]
    wl = w0_ref[...] + _mm(jnp.tanh(xw).astype(bf16), w2_ref[...])
    lw_s[...] = -float(np.exp(-0.5)) * jax.nn.sigmoid(wl)
    a_icl = jax.nn.sigmoid(a0_ref[...] + _mm(xa.astype(bf16), a2_ref[...]))
    g_s[...] = _mm(jax.nn.sigmoid(xg).astype(bf16), g2_ref[...])
    kk = k * kk_ref[...]
    kk = kk * lax.rsqrt(jnp.maximum(segsum(kk * kk), 1e-24))
    k2 = k * (1.0 + (a_icl - 1.0) * ka_ref[...])
    r_s[...] = r
    k_s[...] = k2
    v_s[...] = v
    a_s[...] = -kk
    b_s[...] = kk * a_icl
    bon_s[...] = segsum(r * k2 * rk_ref[...])

    ri = lax.broadcasted_iota(jnp.int32, (c64, c64), 0)
    ci = lax.broadcasted_iota(jnp.int32, (c64, c64), 1)
    strict = ri > ci
    incl = ri >= ci
    eye = jnp.where(ri == ci, 1.0, 0.0).astype(f32)
    tri = tri_ref[...]

    heads = range(RWKV_HEADS)
    sls = [slice(h * HEAD_DIM, (h + 1) * HEAD_DIM) for h in heads]

    def chunk_terms(c):
        rows = pl.ds(pl.multiple_of(c * c64, c64), c64)
        lwc = lw_s[rows, :]
        lhi, llo = _split2(lwc)
        cl = _mm(tri, lhi) + _mm(tri, llo)
        cl_last = cl[c64 - 1:c64, :]
        e_neg = jnp.exp(-cl)
        e_tot = jnp.exp(cl_last - cl)
        ac = a_s[rows, :]
        bc = b_s[rows, :]
        kc = k_s[rows, :]
        r_t = r_s[rows, :] * jnp.exp(cl)
        rt_s[rows, :] = r_t
        bh_s[rows, :] = bc * e_tot
        kh_s[rows, :] = kc * e_tot
        pc_s[pl.ds(c, 1), :] = jnp.exp(cl_last)
        return rows, ac * jnp.exp(cl - lwc), r_t, bc * e_neg, kc * e_neg, v_s[rows, :]

    def intra_body(c2, carry):
        terms = [chunk_terms(c2 * WKV_PAIR + j) for j in range(WKV_PAIR)]
        probs = [(j, h) for j in range(WKV_PAIR) for h in heads]
        cat = lambda j, p, q, h: jnp.concatenate([terms[j][p][:, sls[h]], terms[j][q][:, sls[h]]],
                                                 axis=0).astype(bf16)
        gm = [_nt(cat(j, 1, 2, h), cat(j, 3, 4, h)) for j, h in probs]
        a_ab = [jnp.where(strict, g[:c64, :c64], 0.0) for g in gm]
        a_ak = [jnp.where(strict, g[:c64, c64:], 0.0).astype(bf16) for g in gm]
        a_rb = [jnp.where(incl, g[c64:, :c64], 0.0).astype(bf16) for g in gm]
        a_rk = [jnp.where(incl, g[c64:, c64:], 0.0).astype(bf16) for g in gm]
        vb = [terms[j][5][:, sls[h]].astype(bf16) for j, h in probs]
        akv = [_mm(a, v) for a, v in zip(a_ak, vb)]
        yloc = [_mm(a, v) for a, v in zip(a_rk, vb)]
        xs = a_ab
        t_inv = [eye + x for x in xs]
        for _ in range(5):
            xb = [x.astype(bf16) for x in xs]
            xs = [_mm(x, x) for x in xb]
            t_inv = [t + _mm(t.astype(bf16), x.astype(bf16)) for t, x in zip(t_inv, xs)]
        rhs = [jnp.concatenate([terms[j][1][:, sls[h]], akv[n]], axis=1).astype(bf16)
               for n, (j, h) in enumerate(probs)]
        tw = [_mm(t.astype(bf16), r) for t, r in zip(t_inv, rhs)]
        for j in range(WKV_PAIR):
            mine = slice(j * RWKV_HEADS, (j + 1) * RWKV_HEADS)
            rows = terms[j][0]
            tw_s[rows, :] = jnp.concatenate(tw[mine], axis=1)
            arb_s[rows, :] = jnp.concatenate(a_rb[mine], axis=1)
            y_s[rows, :] = jnp.concatenate(yloc[mine], axis=1)
        return carry

    lax.fori_loop(0, n_chunks // WKV_PAIR, intra_body, 0)

    def state_body(c, carry):
        rows = pl.ds(pl.multiple_of(c * c64, c64), c64)
        twc = tw_s[rows, :]
        arbc = arb_s[rows, :]
        r_t = rt_s[rows, :]
        b_h = bh_s[rows, :]
        k_h = kh_s[rows, :]
        vc = v_s[rows, :]
        yl = y_s[rows, :]
        p_c = pc_s[pl.ds(c, 1), :]
        s_old = [state_s[h] for h in heads]
        wr = [jnp.concatenate([twc[:, h * LANES:h * LANES + c64], r_t[:, sls[h]]], axis=0).astype(bf16)
              for h in heads]
        wrs = [_nt(wr[h], s_old[h].astype(bf16)) for h in heads]
        u = [wrs[h][:c64] + twc[:, h * LANES + c64:(h + 1) * LANES] for h in heads]
        ys = [wrs[h][c64:] + _mm(arbc[:, sls[h]], u[h].astype(bf16)) + yl[:, sls[h]] for h in heads]
        uv = [jnp.concatenate([u[h], vc[:, sls[h]]], axis=0).astype(bf16) for h in heads]
        bkh = [jnp.concatenate([b_h[:, sls[h]], k_h[:, sls[h]]], axis=0).astype(bf16) for h in heads]
        for h in heads:
            state_s[h] = s_old[h] * p_c[:, sls[h]] + _tn(uv[h], bkh[h])
        y_s[rows, :] = jnp.concatenate(ys, axis=1)
        return carry

    lax.fori_loop(0, n_chunks, state_body, 0)

    y = y_s[...]
    mean = segsum(y) * (1.0 / HEAD_DIM)
    d = y - mean
    var = segsum(d * d) * (1.0 / HEAD_DIM)
    yn = d * lax.rsqrt(var + RWKV_GN_EPS) * lnw_ref[...] + lnb_ref[...]
    o_ref[0] = (yn + bon_s[...] * v_s[...]) * g_s[...]


def _rwkv(x2, w_all, layer, p, bsz, seq, tb=512):
    tb = min(tb, seq)
    n_l = seq // tb
    seg = (np.arange(MXU_TILE)[:, None] // HEAD_DIM == np.arange(MXU_TILE)[None, :] // HEAD_DIM)
    seg = jnp.asarray(seg, bf16)
    tri = jnp.asarray(np.tril(np.ones((WKV_CHUNK, WKV_CHUNK))), bf16)
    row = lambda a: a.reshape(1, -1).astype(f32)
    args = [x2, w_all, row(p['mu']), row(p['w0']), p['w2'].astype(bf16), row(p['a0']), p['a2'].astype(bf16),
            p['g2'].astype(bf16), row(p['k_k']), row(p['k_a']), row(p['r_k']), row(p['ln_w']),
            row(p['ln_b']), seg, tri]
    full = lambda a: pl.BlockSpec(a.shape, lambda b, l: (0,) * a.ndim)
    in_specs = ([pl.BlockSpec((tb, D_MODEL), lambda b, l: (b * n_l + l, 0)), _w_window(layer, 0, RWKV_COLS)]
                + [full(a) for a in args[2:]])
    big = lambda: pltpu.VMEM((tb, D_RWKV), f32)
    return pl.pallas_call(
        _rwkv_kernel,
        grid=(bsz, n_l),
        in_specs=in_specs,
        out_specs=pl.BlockSpec((1, tb, D_RWKV), lambda b, l: (b, l, 0)),
        out_shape=jax.ShapeDtypeStruct((bsz, seq, D_RWKV), f32),
        scratch_shapes=[pltpu.VMEM((1, RWKV_COLS), f32),
                        pltpu.VMEM((RWKV_HEADS, HEAD_DIM, HEAD_DIM), f32),
                        big(), big(), big(), big(), big(), big(), big(), big(), big(),
                        big(), big(), big(), pltpu.VMEM((tb, RWKV_HEADS * LANES), f32),
                        pltpu.VMEM((tb, D_RWKV), bf16), pltpu.VMEM((tb // WKV_CHUNK, D_RWKV), f32)],
        compiler_params=pltpu.CompilerParams(
            dimension_semantics=("parallel", "arbitrary"), vmem_limit_bytes=VMEM_LIMIT),
    )(*args)


MOBA_AUG_POS = 0
MOBA_AUG_BLK = 6
LOG2E = 1.4426950408889634


MOBA_STEP_HEADS = 4


def _moba_kernel(q_ref, k_ref, v_ref, tmpl_ref, o_ref, kaug_s, vaug_s, kmean_s):
    blk = MOBA_BLOCK
    half = HEAD_DIM
    seq = k_ref.shape[1]
    n_blk = seq // blk
    qi = pl.program_id(2)
    heads = range(MOBA_STEP_HEADS)
    tile = [slice((hh // 2) * LANES, (hh // 2 + 1) * LANES) for hh in heads]
    aug0 = [half if hh % 2 == 0 else 0 for hh in heads]

    @pl.when(qi == 0)
    def _():
        lane_l = lax.broadcasted_iota(jnp.int32, (seq, LANES), 1)
        for hh in heads:
            kb = k_ref[0, :, tile[hh]].astype(bf16)
            vb = v_ref[0, :, tile[hh]].astype(bf16)
            in_head = (lane_l < half) if hh % 2 == 0 else (lane_l >= half)
            kaug_s[hh] = jnp.where(in_head, kb, tmpl_ref[0, hh])
            ones_lane = jnp.where(lane_l == aug0[hh], 1.0, 0.0).astype(bf16)
            vaug_s[hh] = jnp.where(in_head, vb, ones_lane)
        for n in range(n_blk):
            km = jnp.mean(k_ref[0, n * blk:(n + 1) * blk, :], axis=0, keepdims=True)
            for tt in range(MOBA_STEP_HEADS // 2):
                kmean_s[tt, n:n + 1, :] = km[:, tt * LANES:(tt + 1) * LANES]

    ri = lax.broadcasted_iota(jnp.int32, (blk, blk), 0)
    ci = lax.broadcasted_iota(jnp.int32, (blk, blk), 1)
    causal = ri >= ci
    lane = lax.broadcasted_iota(jnp.int32, (blk, LANES), 1)
    lane_f = lane.astype(f32)
    qf = [q_ref[0, :, tile[hh]] * (HEAD_DIM ** -0.5 * LOG2E) for hh in heads]
    in_head = [(lane < half) if hh % 2 == 0 else (lane >= half) for hh in heads]
    blk_lane = [lane - (aug0[hh] + MOBA_AUG_BLK) for hh in heads]

    qs = [_split2(jnp.where(in_head[hh], qf[hh], 0.0)) for hh in heads]
    kms = [_split2(kmean_s[tt]) for tt in range(MOBA_STEP_HEADS // 2)]
    gate = [_nt(kms[hh // 2][0], qs[hh][0]) + _nt(kms[hh // 2][1], qs[hh][0])
            + _nt(kms[hh // 2][0], qs[hh][1]) for hh in heads]
    is_pos = [(lane >= aug0[hh] + MOBA_AUG_POS) & (lane < aug0[hh] + MOBA_AUG_POS + 3) for hh in heads]
    own = [jnp.where(is_pos[hh], 1.0, 0.0) for hh in heads]
    q_own = [jnp.where(in_head[hh], qf[hh], own[hh]).astype(bf16) for hh in heads]

    start = pl.multiple_of(qi * blk, blk)
    s0 = [_nt(q_own[hh], kaug_s[hh, pl.ds(start, blk), :]) for hh in heads]
    s0 = [jnp.where(causal, s, NEG_INF) for s in s0]
    m0 = [jnp.max(s, axis=-1, keepdims=True) for s in s0]
    p0 = [jnp.exp2(s - m).astype(bf16) for s, m in zip(s0, m0)]
    a0 = [_mm(p0[hh], vaug_s[hh, pl.ds(start, blk), :]) for hh in heads]

    brow = lax.broadcasted_iota(jnp.int32, (n_blk, blk), 0)
    brow_f = brow.astype(f32)
    g = [jnp.where(brow < qi, x, -jnp.inf) for x in gate]
    picked = [jnp.zeros((n_blk, blk), f32) for _ in heads]
    for _ in range(MOBA_TOPK):
        mx = [jnp.max(x, axis=0, keepdims=True) for x in g]
        idx = [jnp.min(jnp.where(g[hh] == mx[hh], brow_f, 1e9), axis=0, keepdims=True) for hh in heads]
        hit = [brow_f == i for i in idx]
        picked = [jnp.where(hit[hh], 1.0, picked[hh]) for hh in heads]
        g = [jnp.where(hit[hh], -jnp.inf, g[hh]) for hh in heads]
    sel_t = [jnp.where(brow < qi, p, 0.0).astype(bf16) for p in picked]
    lane_b = lax.broadcasted_iota(jnp.int32, (n_blk, LANES), 1)
    brow_b = lax.broadcasted_iota(jnp.int32, (n_blk, LANES), 0)
    q_aug = []
    for hh in heads:
        place = jnp.where(lane_b == brow_b + (aug0[hh] + MOBA_AUG_BLK), 1.0, 0.0).astype(bf16)
        sel = _tn(sel_t[hh], place) > 0.5
        is_blk = (blk_lane[hh] >= 0) & (blk_lane[hh] < n_blk)
        aug = jnp.where(is_blk, jnp.where(sel, 0.0, NEG_INF), own[hh])
        q_aug.append(jnp.where(in_head[hh], qf[hh], aug).astype(bf16))

    def make_body(width, base):
        def body(t, carry):
            ms, accs = carry
            rows = pl.ds(pl.multiple_of(base + t * (width * blk), 2 * blk), width * blk)
            s = [_nt(q_aug[hh], kaug_s[hh, rows, :]) for hh in heads]
            m_new = [jnp.maximum(ms[hh], jnp.max(s[hh], axis=-1, keepdims=True)) for hh in heads]
            p = [jnp.exp2(s[hh] - m_new[hh]).astype(bf16) for hh in heads]
            pv = [_mm(p[hh], vaug_s[hh, rows, :]) for hh in heads]
            accs = [accs[hh] * jnp.exp2(ms[hh] - m_new[hh]) + pv[hh] for hh in heads]
            return (tuple(m_new), tuple(accs))
        return body

    n_pairs = (qi + 1) // 2
    n_quads = n_pairs // 2
    carry = lax.fori_loop(0, n_quads, make_body(4, 0), (tuple(m0), tuple(a0)))
    _, accs = lax.fori_loop(0, n_pairs - 2 * n_quads, make_body(2, n_quads * (4 * blk)), carry)
    out = [accs[hh] / accs[hh][:, aug0[hh]:aug0[hh] + 1] for hh in heads]
    for tt in range(MOBA_STEP_HEADS // 2):
        o_ref[0, :, tt * LANES:(tt + 1) * LANES] = jnp.where(lane < half, out[2 * tt], out[2 * tt + 1])


def _moba_template(seq):
    pos = np.arange(seq, dtype=np.float64)
    tmpl = np.zeros((MOBA_HEADS, seq, LANES), np.float32)
    for h in range(MOBA_HEADS):
        a0 = HEAD_DIM if h % 2 == 0 else 0
        slope = 2.0 ** (-8.0 * (h + 1) / MOBA_HEADS)
        rest = slope * LOG2E * pos
        for c in range(3):
            part = rest.astype(np.float32).astype(bf16).astype(np.float64)
            tmpl[h, :, a0 + MOBA_AUG_POS + c] = part
            rest = rest - part
        tmpl[h, np.arange(seq), a0 + MOBA_AUG_BLK + np.arange(seq) // MOBA_BLOCK] = 1.0
    return jnp.asarray(tmpl.reshape(MOBA_HEADS // MOBA_STEP_HEADS, MOBA_STEP_HEADS, seq, LANES), bf16)


def _moba(qkv3):
    bsz, seq, _ = qkv3.shape
    blk = MOBA_BLOCK
    sh = MOBA_STEP_HEADS
    n_grp = MOBA_HEADS // sh
    wid = sh * HEAD_DIM
    assert seq % (2 * blk) == 0 and seq // blk <= HEAD_DIM - MOBA_AUG_BLK
    return pl.pallas_call(
        _moba_kernel,
        grid=(bsz, n_grp, seq // blk),
        in_specs=[pl.BlockSpec((1, blk, wid), lambda b, p, i: (b, i, p)),
                  pl.BlockSpec((1, seq, wid), lambda b, p, i: (b, 0, n_grp + p)),
                  pl.BlockSpec((1, seq, wid), lambda b, p, i: (b, 0, 2 * n_grp + p)),
                  pl.BlockSpec((1, sh, seq, LANES), lambda b, p, i: (p, 0, 0, 0))],
        out_specs=pl.BlockSpec((1, blk, wid), lambda b, p, i: (b, i, p)),
        out_shape=jax.ShapeDtypeStruct((bsz, seq, D_MOBA), f32),
        scratch_shapes=[pltpu.VMEM((sh, seq, LANES), bf16), pltpu.VMEM((sh, seq, LANES), bf16),
                        pltpu.VMEM((sh // 2, seq // blk, LANES), f32)],
        compiler_params=pltpu.CompilerParams(
            dimension_semantics=("parallel", "parallel", "arbitrary"), vmem_limit_bytes=VMEM_LIMIT),
    )(qkv3, qkv3, qkv3, _moba_template(seq))


S5_SUPER = 4
S5_SEGS = 8
S5_COLS = 512


def _s5_kernel(x_ref, wu_ref, perm_ref, bd_ref, cre_ref, cim_ref, lam_ref, d_ref, gw_ref, gb_ref, o_ref,
               hr_s, hi_s, carry_s):
    tb = x_ref.shape[0]
    seg = tb // S5_SEGS
    n_state = lam_ref.shape[1]
    sgw = n_state // S5_SUPER
    uw = D_SSM // S5_SUPER

    @pl.when(pl.program_id(1) == 0)
    def _():
        carry_s[...] = jnp.zeros_like(carry_s)

    u = _mm(x_ref[...].astype(bf16), wu_ref[0])
    perm = perm_ref[...]
    ub = _mm(perm, u.astype(bf16)).astype(bf16)
    for sg in range(S5_SUPER):
        bu = _mm(ub[:, sg * uw:(sg + 1) * uw], bd_ref[sg])
        for j in range(sgw // LANES):
            hr_s[sg * (sgw // LANES) + j] = bu[:, j * LANES:(j + 1) * LANES]
            hi_s[sg * (sgw // LANES) + j] = bu[:, sgw + j * LANES:sgw + (j + 1) * LANES]

    tpc = S5_COLS // LANES

    def load(ref, cc, idx):
        return jnp.concatenate([ref[cc * tpc + j, idx, :] for j in range(tpc)], axis=1)

    def store(ref, cc, idx, val):
        for j in range(tpc):
            ref[cc * tpc + j, idx, :] = val[:, j * LANES:(j + 1) * LANES]

    for cc in range(n_state // S5_COLS):
        cols = slice(cc * S5_COLS, (cc + 1) * S5_COLS)
        lr = lam_ref[0:1, cols]
        li = lam_ref[1:2, cols]

        def local(r, carry, cc=cc, lr=lr, li=li):
            hr, hi = carry
            idx = pl.ds(pl.multiple_of(r * S5_SEGS, S5_SEGS), S5_SEGS)
            nr = lr * hr - li * hi + load(hr_s, cc, idx)
            ni = lr * hi + li * hr + load(hi_s, cc, idx)
            store(hr_s, cc, idx, nr)
            store(hi_s, cc, idx, ni)
            return nr, ni

        zero = jnp.zeros((S5_SEGS, S5_COLS), f32)
        er, ei = lax.fori_loop(0, seg, local, (zero, zero), unroll=4)
        pr, pi = lr, li
        for _ in range(seg.bit_length() - 1):
            pr, pi = pr * pr - pi * pi, 2.0 * pr * pi
        cr = carry_s[0:1, cols]
        ci = carry_s[1:2, cols]
        in_r, in_i = [], []
        for s in range(S5_SEGS):
            in_r.append(cr)
            in_i.append(ci)
            cr, ci = pr * cr - pi * ci + er[s:s + 1, :], pr * ci + pi * cr + ei[s:s + 1, :]
        carry_s[0:1, cols] = cr
        carry_s[1:2, cols] = ci

        def fix(r, carry, cc=cc, lr=lr, li=li):
            qr, qi = carry
            qr, qi = lr * qr - li * qi, lr * qi + li * qr
            idx = pl.ds(pl.multiple_of(r * S5_SEGS, S5_SEGS), S5_SEGS)
            store(hr_s, cc, idx, load(hr_s, cc, idx) + qr)
            store(hi_s, cc, idx, load(hi_s, cc, idx) + qi)
            return qr, qi

        lax.fori_loop(0, seg, fix, (jnp.concatenate(in_r, axis=0), jnp.concatenate(in_i, axis=0)),
                      unroll=4)

    slab = lambda ref, sg: jnp.concatenate(
        [ref[sg * (sgw // LANES) + j] for j in range(sgw // LANES)], axis=1).astype(bf16)
    ys = [_mm(slab(hr_s, sg), cre_ref[sg]) + _mm(slab(hi_s, sg), cim_ref[sg]) for sg in range(S5_SUPER)]
    yh, yl = _split2(jnp.concatenate(ys, axis=1))
    y = _tn(perm, yh) + _tn(perm, yl) + d_ref[...] * u
    y = 0.5 * y * (1.0 + jnp.tanh(0.7978845608028654 * (y + 0.044715 * (y * y * y))))
    zg = _mm(y.astype(bf16), gw_ref[...]) + gb_ref[...]
    o_ref[0] = zg[:, :D_SSM] * jax.nn.sigmoid(zg[:, D_SSM:])


def _s5_params(p):
    a_re = p['a_re'].astype(f32)
    a_im = p['a_im'].astype(f32)
    dt = jnp.exp(p['log_dt'].astype(f32))[:, None]
    mag = jnp.exp(a_re * dt)
    lam_re = mag * jnp.cos(a_im * dt)
    lam_im = mag * jnp.sin(a_im * dt)
    den = a_re * a_re + a_im * a_im
    nr = lam_re - 1.0
    coef_re = (nr * a_re + lam_im * a_im) / den
    coef_im = (lam_im * a_re - nr * a_im) / den
    b_re = p['b_re'].astype(f32)
    b_im = p['b_im'].astype(f32)
    bb_re = coef_re[..., None] * b_re - coef_im[..., None] * b_im
    bb_im = coef_re[..., None] * b_im + coef_im[..., None] * b_re
    gps = SSM_GROUPS // S5_SUPER
    eye = jnp.eye(gps, dtype=f32)
    n_state = SSM_GROUPS * SSM_STATE
    sgw = n_state // S5_SUPER
    uw = D_SSM // S5_SUPER

    def in_map(bb):
        bb = bb.reshape(S5_SUPER, gps, SSM_STATE, SSM_GROUP)
        return jnp.einsum('sgph,gk->sghkp', bb, eye).reshape(S5_SUPER, uw, sgw)

    def out_map(c):
        c = c.reshape(S5_SUPER, gps, SSM_GROUP, SSM_STATE)
        return jnp.einsum('sghp,gk->sgpkh', c, eye).reshape(S5_SUPER, sgw, uw)

    bd = jnp.concatenate([in_map(bb_re), in_map(bb_im)], axis=2)
    cre = out_map(p['c_re'].astype(f32))
    cim = -out_map(p['c_im'].astype(f32))
    lam = jnp.stack([lam_re.reshape(n_state), lam_im.reshape(n_state)], axis=0)
    return bd.astype(bf16), cre.astype(bf16), cim.astype(bf16), lam


def _s5(x2, w_all, layer, p, bsz, seq, tb=512):
    tb = min(tb, seq)
    n_l = seq // tb
    seg = tb // S5_SEGS
    assert tb % S5_SEGS == 0 and seg & (seg - 1) == 0
    n_state = SSM_GROUPS * SSM_STATE
    bd, cre, cim, lam = _s5_params(p)
    src = (np.arange(tb) % S5_SEGS) * seg + np.arange(tb) // S5_SEGS
    perm = jnp.asarray(np.arange(tb)[None, :] == src[:, None], bf16)
    args = [x2, w_all, perm, bd, cre, cim, lam, p['d'].reshape(1, -1).astype(f32), p['glu_w'].astype(bf16),
            p['glu_b'].reshape(1, -1).astype(f32)]
    full = lambda a: pl.BlockSpec(a.shape, lambda b, l: (0,) * a.ndim)
    return pl.pallas_call(
        _s5_kernel,
        grid=(bsz, n_l),
        in_specs=[pl.BlockSpec((tb, D_MODEL), lambda b, l: (b * n_l + l, 0)),
                  _w_window(layer, OFF_SSM, D_SSM)] + [full(a) for a in args[2:]],
        out_specs=pl.BlockSpec((1, tb, D_SSM), lambda b, l: (b, l, 0)),
        out_shape=jax.ShapeDtypeStruct((bsz, seq, D_SSM), f32),
        scratch_shapes=[pltpu.VMEM((n_state // LANES, tb, LANES), f32),
                        pltpu.VMEM((n_state // LANES, tb, LANES), f32),
                        pltpu.VMEM((2, n_state), f32)],
        compiler_params=pltpu.CompilerParams(
            dimension_semantics=("parallel", "arbitrary"), vmem_limit_bytes=VMEM_LIMIT),
    )(*args)


def _merge_kernel(x_ref, ya_ref, yb_ref, yc_ref, wg_ref, gb_ref, wa_ref, wb_ref, wc_ref, wo_ref,
                  lnw_ref, lnb_ref, o_ref):
    x = x_ref[...]
    xb = x.astype(bf16)
    merged = None
    for br, (y_ref, w_ref) in enumerate(((ya_ref, wa_ref), (yb_ref, wb_ref), (yc_ref, wc_ref))):
        cols = slice(br * D_MODEL, (br + 1) * D_MODEL)
        gate = jax.nn.sigmoid(_mm(xb, wg_ref[0, :, cols]) + gb_ref[:, cols])
        term = gate * _mm(y_ref[...].astype(bf16), w_ref[...])
        merged = term if merged is None else merged + term
    h = _mm(merged.astype(bf16), wo_ref[...])
    o_ref[...] = _layer_norm(DEEPNORM_ALPHA * x + h, lnw_ref[...], lnb_ref[...])


def _merge(x2, ya, yb, yc, w_all, layer, gb, wa, wb, wc, wo, lnw, lnb, tm=512):
    m = x2.shape[0]
    rowblk = lambda n: pl.BlockSpec((tm, n), lambda i: (i, 0))
    full = lambda a: pl.BlockSpec(a.shape, lambda i: (0,) * a.ndim)
    consts = [gb, wa, wb, wc, wo, lnw, lnb]
    return pl.pallas_call(
        _merge_kernel,
        grid=(m // tm,),
        in_specs=[rowblk(D_MODEL), rowblk(D_RWKV), rowblk(D_MOBA), rowblk(D_SSM),
                  _w_window(layer, OFF_GATE, N_BRANCHES * D_MODEL)] + [full(a) for a in consts],
        out_specs=rowblk(D_MODEL),
        out_shape=jax.ShapeDtypeStruct((m, D_MODEL), f32),
        compiler_params=pltpu.CompilerParams(
            dimension_semantics=("parallel",), vmem_limit_bytes=VMEM_LIMIT),
    )(x2, ya, yb, yc, w_all, *consts)


MOE_STEP_EXPERTS = 4
MOE_CHUNK = 256
MOE_STEPS = N_EXPERTS // MOE_STEP_EXPERTS
STEPS_PER_GROUP = EXPERTS_PER_GROUP // MOE_STEP_EXPERTS


def _moe_kernel(x_ref, rwh_ref, rwl_ref, rb_ref, wg_ref, wu_ref, wd_ref, lnw_ref, lnb_ref, o_ref,
                xs_s, outs_s, wts_s, pos_s, posrow_s, meta_s):
    s = pl.program_id(1)
    tm = x_ref.shape[0]
    ck = MOE_CHUNK
    lane = lax.broadcasted_iota(jnp.int32, (tm, LANES), 1)
    lane_f = lane.astype(f32)

    @pl.when(s == 0)
    def _():
        xh, xl = _split2(x_ref[...])
        rwh = rwh_ref[...]
        logits = _mm(xh, rwh) + _mm(xh, rwl_ref[...]) + _mm(xl, rwh) + rb_ref[...]
        is_grp = (lane >= N_EXPERTS) & (lane < N_EXPERTS + N_EXPERT_GROUPS)
        gl = jnp.where(is_grp, logits, -jnp.inf)
        gmax = jnp.max(gl, axis=-1, keepdims=True)
        gidx = jnp.min(jnp.where(gl == gmax, lane_f, 1e9), axis=-1, keepdims=True) - float(N_EXPERTS)
        p_group = 1.0 / jnp.sum(jnp.where(is_grp, jnp.exp(gl - gmax), 0.0), axis=-1, keepdims=True)
        grp_of_lane = jnp.floor(lane_f * (1.0 / EXPERTS_PER_GROUP))
        in_grp = (lane < N_EXPERTS) & (grp_of_lane == gidx)
        el = jnp.where(in_grp, logits, -jnp.inf)
        m1 = jnp.max(el, axis=-1, keepdims=True)
        i1 = jnp.min(jnp.where(el == m1, lane_f, 1e9), axis=-1, keepdims=True)
        el2 = jnp.where(lane_f == i1, -jnp.inf, el)
        m2 = jnp.max(el2, axis=-1, keepdims=True)
        i2 = jnp.min(jnp.where(el2 == m2, lane_f, 1e9), axis=-1, keepdims=True)
        e2 = jnp.exp(m2 - m1)
        w1 = p_group / (1.0 + e2)
        w2 = p_group * e2 / (1.0 + e2)
        wt = jnp.where(lane_f == i1, w1, 0.0) + jnp.where(lane_f == i2, w2, 0.0)

        ohg = lane_f == gidx
        ri = lax.broadcasted_iota(jnp.int32, (tm, tm), 0)
        ci = lax.broadcasted_iota(jnp.int32, (tm, tm), 1)
        tri = jnp.where(ri >= ci, 1.0, 0.0).astype(bf16)
        cg = _mm(tri, jnp.where(ohg, 1.0, 0.0).astype(bf16))
        cnt = cg[tm - 1:tm, :]
        padded = jnp.ceil(cnt * (1.0 / ck)) * ck
        ui = lax.broadcasted_iota(jnp.int32, (LANES, LANES), 0)
        uj = lax.broadcasted_iota(jnp.int32, (LANES, LANES), 1)
        upper = jnp.where(ui < uj, 1.0, 0.0).astype(bf16)
        off = _mm(jnp.broadcast_to(padded, (8, LANES)).astype(bf16), upper)[0:1, :]
        rank = jnp.sum(jnp.where(ohg, cg - 1.0, 0.0), axis=-1, keepdims=True)
        pos = jnp.sum(jnp.where(ohg, off, 0.0), axis=-1, keepdims=True) + rank
        pos_s[...] = jnp.broadcast_to(pos, (tm, LANES))
        pa = jnp.floor(pos * (1.0 / 64.0))
        pb = pos - 64.0 * pa
        cols = jnp.where(lane == 0, pa, jnp.where(lane == 1, pb, 0.0)).astype(bf16)
        r8 = lax.broadcasted_iota(jnp.int32, (8, LANES), 0)
        l8 = lax.broadcasted_iota(jnp.int32, (8, LANES), 1)
        sel8 = jnp.where((r8 == 0) & (l8 == 0), 64.0, jnp.where((r8 == 0) & (l8 == 1), 1.0, 0.0))
        posrow_s[...] = _nt(sel8.astype(bf16), cols)
        off_i = off.astype(jnp.int32)
        nck_i = (padded * (1.0 / ck)).astype(jnp.int32)
        total = 0
        for g in range(N_EXPERT_GROUPS):
            meta_s[g] = off_i[0, g]
            meta_s[N_EXPERT_GROUPS + g] = nck_i[0, g]
            total = total + nck_i[0, g]
        meta_s[2 * N_EXPERT_GROUPS] = total

        wth, wtl = _split2(wt)
        rows_f = lax.broadcasted_iota(jnp.int32, (ck, tm), 0).astype(f32)
        posrow = posrow_s[0:1, :]

        def fill(c, carry):
            r0 = pl.multiple_of(c * ck, ck)
            pc = jnp.where(rows_f + r0.astype(f32) == posrow, 1.0, 0.0).astype(bf16)
            xs_s[pl.ds(r0, ck), :] = _mm(pc, xh).astype(bf16)
            wts_s[pl.ds(r0, ck), :] = _mm(pc, wth) + _mm(pc, wtl)
            outs_s[pl.ds(r0, ck), :] = jnp.zeros((ck, D_MODEL), f32)
            return carry

        lax.fori_loop(0, total, fill, 0)

    grp = s // STEPS_PER_GROUP
    off_g = meta_s[grp]
    lane_c = lax.broadcasted_iota(jnp.int32, (ck, LANES), 1)

    def chunk(c, carry):
        rows = pl.ds(pl.multiple_of(off_g + c * ck, ck), ck)
        xs = xs_s[rows, :]
        wts = wts_s[rows, :]
        gact = [_mm(xs, wg_ref[j]) for j in range(MOE_STEP_EXPERTS)]
        up = [_mm(xs, wu_ref[j]) for j in range(MOE_STEP_EXPERTS)]
        part = None
        for j in range(MOE_STEP_EXPERTS):
            wcol = jnp.sum(jnp.where(lane_c == s * MOE_STEP_EXPERTS + j, wts, 0.0), axis=-1, keepdims=True)
            hid = gact[j] * jax.nn.sigmoid(gact[j]) * up[j] * wcol
            out = _mm(hid.astype(bf16), wd_ref[j])
            part = out if part is None else part + out
        outs_s[rows, :] += part
        return carry

    lax.fori_loop(0, meta_s[N_EXPERT_GROUPS + grp], chunk, 0)

    @pl.when(s == MOE_STEPS - 1)
    def _():
        o_ref[...] = DEEPNORM_ALPHA * x_ref[...]
        pos_col = pos_s[:, 0:1]
        cols_f = lax.broadcasted_iota(jnp.int32, (tm, ck), 1).astype(f32)

        def unsort(c, carry):
            r0 = pl.multiple_of(c * ck, ck)
            pt = jnp.where(cols_f + r0.astype(f32) == pos_col, 1.0, 0.0).astype(bf16)
            o_ref[...] += _mm(pt, outs_s[pl.ds(r0, ck), :].astype(bf16))
            return carry

        lax.fori_loop(0, meta_s[2 * N_EXPERT_GROUPS], unsort, 0)
        o_ref[...] = _layer_norm(o_ref[...], lnw_ref[...], lnb_ref[...])


def _moe(x2, rw, rb, wg, wu, wd, lnw, lnb, tm=1024):
    m = x2.shape[0]
    tm = min(tm, m)
    rwh, rwl = _split2(rw)
    n_rows = tm + N_EXPERT_GROUPS * MOE_CHUNK
    full = lambda a: pl.BlockSpec(a.shape, lambda i, e: (0,) * a.ndim)
    return pl.pallas_call(
        _moe_kernel,
        grid=(m // tm, MOE_STEPS),
        in_specs=[pl.BlockSpec((tm, D_MODEL), lambda i, e: (i, 0)),
                  full(rwh), full(rwl), full(rb),
                  pl.BlockSpec((MOE_STEP_EXPERTS, D_MODEL, D_EXPERT), lambda i, e: (e, 0, 0)),
                  pl.BlockSpec((MOE_STEP_EXPERTS, D_MODEL, D_EXPERT), lambda i, e: (e, 0, 0)),
                  pl.BlockSpec((MOE_STEP_EXPERTS, D_EXPERT, D_MODEL), lambda i, e: (e, 0, 0)),
                  full(lnw), full(lnb)],
        out_specs=pl.BlockSpec((tm, D_MODEL), lambda i, e: (i, 0)),
        out_shape=jax.ShapeDtypeStruct((m, D_MODEL), f32),
        scratch_shapes=[pltpu.VMEM((n_rows, D_MODEL), bf16), pltpu.VMEM((n_rows, D_MODEL), f32),
                        pltpu.VMEM((n_rows, LANES), f32), pltpu.VMEM((tm, LANES), f32),
                        pltpu.VMEM((8, tm), f32), pltpu.SMEM((2 * N_EXPERT_GROUPS + 1,), jnp.int32)],
        compiler_params=pltpu.CompilerParams(
            dimension_semantics=("parallel", "arbitrary"), vmem_limit_bytes=VMEM_LIMIT),
    )(x2, rwh, rwl, rb, wg, wu, wd, lnw, lnb)


def _router_weights(router_group_w, router_group_b, router_expert_w, router_expert_b):
    pad = LANES - N_EXPERTS - N_EXPERT_GROUPS
    rw = jnp.concatenate([router_expert_w.astype(f32), router_group_w.astype(f32),
                          jnp.zeros((D_MODEL, pad), f32)], axis=1)
    rb = jnp.concatenate([router_expert_b.astype(f32), router_group_b.astype(f32),
                          jnp.zeros((pad,), f32)]).reshape(1, LANES)
    return rw, rb


def kernel(x, w_in, rwkv_mu, rwkv_w0, rwkv_w2, rwkv_a0, rwkv_a2, rwkv_g2, rwkv_k_k, rwkv_k_a, rwkv_r_k, rwkv_ln_w, rwkv_ln_b, ssm_a_re, ssm_a_im, ssm_b_re, ssm_b_im, ssm_c_re, ssm_c_im, ssm_d, ssm_log_dt, ssm_glu_w, ssm_glu_b, w_up_rwkv, w_up_moba, w_up_ssm, gate_b, w_out, ln1_w, ln1_b, router_group_w, router_group_b, router_expert_w, router_expert_b, expert_w_gate, expert_w_up, expert_w_down, ln2_w, ln2_b):
    bsz, seq, _ = x.shape
    x2 = x.reshape(bsz * seq, D_MODEL).astype(f32)
    row = lambda a: a.reshape(1, -1).astype(f32)
    w_all = w_in.astype(bf16)
    for l in range(DEPTH):
        qkv = _project(x2, w_all, l, OFF_MOBA, MOBA_COLS)
        y_a = _rwkv(x2, w_all, l,
                    dict(mu=rwkv_mu[l], w0=rwkv_w0[l], w2=rwkv_w2[l], a0=rwkv_a0[l], a2=rwkv_a2[l],
                         g2=rwkv_g2[l], k_k=rwkv_k_k[l], k_a=rwkv_k_a[l], r_k=rwkv_r_k[l],
                         ln_w=rwkv_ln_w[l], ln_b=rwkv_ln_b[l]), bsz, seq)
        y_b = _moba(qkv.reshape(bsz, seq, MOBA_COLS))
        y_c = _s5(x2, w_all, l,
                  dict(a_re=ssm_a_re[l], a_im=ssm_a_im[l], b_re=ssm_b_re[l], b_im=ssm_b_im[l],
                       c_re=ssm_c_re[l], c_im=ssm_c_im[l], d=ssm_d[l], log_dt=ssm_log_dt[l],
                       glu_w=ssm_glu_w[l], glu_b=ssm_glu_b[l]), bsz, seq)
        x2 = _merge(x2, y_a.reshape(-1, D_RWKV), y_b.reshape(-1, D_MOBA), y_c.reshape(-1, D_SSM),
                    w_all, l, row(gate_b[l]),
                    w_up_rwkv[l].astype(bf16), w_up_moba[l].astype(bf16), w_up_ssm[l].astype(bf16),
                    w_out[l].astype(bf16), row(ln1_w[l]), row(ln1_b[l]))
        rw, rb = _router_weights(router_group_w[l], router_group_b[l], router_expert_w[l],
                                 router_expert_b[l])
        x2 = _moe(x2, rw, rb, expert_w_gate[l].astype(bf16), expert_w_up[l].astype(bf16),
                  expert_w_down[l].astype(bf16), row(ln2_w[l]), row(ln2_b[l]))
    return x2.reshape(bsz, seq, D_MODEL)
```

```python
import functools

import jax
import jax.numpy as jnp
import numpy as np
from jax import lax
from jax.experimental import pallas as pl
from jax.experimental.pallas import tpu as pltpu

f32 = jnp.float32
bf16 = jnp.bfloat16

D_MODEL = 1024
DEPTH = 4
HEAD_DIM = 64
D_RWKV = 512
RWKV_HEADS = D_RWKV // HEAD_DIM
DECAY_LORA = 64
ICL_LORA = 64
GATE_LORA = 128
RWKV_GN_EPS = 64e-5
D_MOBA = 512
MOBA_HEADS = D_MOBA // HEAD_DIM
MOBA_BLOCK = 256
MOBA_TOPK = 3
D_SSM = 512
SSM_GROUP = 16
SSM_GROUPS = D_SSM // SSM_GROUP
SSM_STATE = 64
N_BRANCHES = 3
N_EXPERT_GROUPS = 4
EXPERTS_PER_GROUP = 8
N_EXPERTS = N_EXPERT_GROUPS * EXPERTS_PER_GROUP
D_EXPERT = D_MODEL // 4
LN_EPS = 1e-5
DEEPNORM_ALPHA = (2 * DEPTH) ** 0.25
NEG_INF = -1e30
RWKV_COLS = 3 * D_RWKV + DECAY_LORA + ICL_LORA + GATE_LORA
MOBA_COLS = 3 * D_MOBA
OFF_MOBA = RWKV_COLS
OFF_SSM = OFF_MOBA + MOBA_COLS
OFF_GATE = OFF_SSM + D_SSM

LANES = 128
MXU_TILE = 256
WKV_CHUNK = 64
WKV_PAIR = 4
WKV_BASE = 4
VMEM_LIMIT = 48 * 1024 * 1024


def _nt(a, b):
    return lax.dot_general(a, b, (((1,), (1,)), ((), ())), preferred_element_type=f32)


def _tn(a, b):
    return lax.dot_general(a, b, (((0,), (0,)), ((), ())), preferred_element_type=f32)


def _mm(a, b):
    return jnp.dot(a, b, preferred_element_type=f32)


def _split2(x):
    hi = x.astype(bf16)
    lo = (x - hi.astype(f32)).astype(bf16)
    return hi, lo


def _layer_norm(y, w, b):
    mu = jnp.mean(y, axis=-1, keepdims=True)
    d = y - mu
    var = jnp.mean(d * d, axis=-1, keepdims=True)
    return d * lax.rsqrt(var + LN_EPS) * w + b


def _proj_kernel(x_ref, w_ref, o_ref):
    o_ref[...] = _mm(x_ref[...].astype(bf16), w_ref[0])


def _w_window(layer, off, width):
    return pl.BlockSpec((pl.Element(1), pl.Element(D_MODEL), pl.Element(width)),
                        lambda *_: (layer, 0, off))


def _project(x2, w_all, layer, off, n, tm=1024):
    m, k = x2.shape
    return pl.pallas_call(
        _proj_kernel,
        grid=(m // tm,),
        in_specs=[pl.BlockSpec((tm, k), lambda i: (i, 0)), _w_window(layer, off, n)],
        out_specs=pl.BlockSpec((tm, n), lambda i: (i, 0)),
        out_shape=jax.ShapeDtypeStruct((m, n), f32),
        compiler_params=pltpu.CompilerParams(
            dimension_semantics=("parallel",), vmem_limit_bytes=VMEM_LIMIT),
    )(x2, w_all)


def _rwkv_kernel(x_ref, wz_ref, mu_ref, w0_ref, w2_ref, a0_ref, a2_ref, g2_ref, kk_ref, ka_ref, rk_ref,
                 lnw_ref, lnb_ref, seg_ref, tri_ref, o_ref,
                 zlast_s, state_s, r_s, k_s, v_s, a_s, b_s, lw_s, g_s, bon_s, y_s,
                 rt_s, bh_s, kh_s, tw_s, arb_s, pc_s):
    tb = x_ref.shape[0]
    n_chunks = tb // WKV_CHUNK
    c64 = WKV_CHUNK

    @pl.when(pl.program_id(1) == 0)
    def _():
        zlast_s[...] = jnp.zeros_like(zlast_s)
        state_s[...] = jnp.zeros_like(state_s)

    seg = seg_ref[...]

    def segsum(x):
        hi, lo = _split2(x)
        w = seg.shape[0]
        return jnp.concatenate([_mm(hi[:, c:c + w], seg) + _mm(lo[:, c:c + w], seg)
                                for c in range(0, D_RWKV, w)], axis=1)

    z = _mm(x_ref[...].astype(bf16), wz_ref[0])
    row = lax.broadcasted_iota(jnp.int32, z.shape, 0)
    z_prev = jnp.where(row == 0, zlast_s[...], pltpu.roll(z, 1, axis=0))
    zlast_s[...] = z[tb - 1:tb, :]
    zz = z + (z_prev - z) * mu_ref[...]
    r = zz[:, 0:D_RWKV]
    k = zz[:, D_RWKV:2 * D_RWKV]
    v = zz[:, 2 * D_RWKV:3 * D_RWKV]
    o1 = 3 * D_RWKV
    xw = zz[:, o1:o1 + DECAY_LORA]
    xa = zz[:, o1 + DECAY_LORA:o1 + DECAY_LORA + ICL_LORA]
    xg = zz[:, o1 + DECAY_LORA + ICL_LORA:]
    wl = w0_ref[...] + _mm(jnp.tanh(xw).astype(bf16), w2_ref[...])
    lw_s[...] = -float(np.exp(-0.5)) * jax.nn.sigmoid(wl)
    a_icl = jax.nn.sigmoid(a0_ref[...] + _mm(xa.astype(bf16), a2_ref[...]))
    g_s[...] = _mm(jax.nn.sigmoid(xg).astype(bf16), g2_ref[...])
    kk = k * kk_ref[...]
    kk = kk * lax.rsqrt(jnp.maximum(segsum(kk * kk), 1e-24))
    k2 = k * (1.0 + (a_icl - 1.0) * ka_ref[...])
    r_s[...] = r
    k_s[...] = k2
    v_s[...] = v
    a_s[...] = -kk
    b_s[...] = kk * a_icl
    bon_s[...] = segsum(r * k2 * rk_ref[...])

    ri = lax.broadcasted_iota(jnp.int32, (c64, c64), 0)
    ci = lax.broadcasted_iota(jnp.int32, (c64, c64), 1)
    strict = ri > ci
    incl = ri >= ci
    eye = jnp.where(ri == ci, 1.0, 0.0).astype(f32)
    tri = tri_ref[...]
    blk_mask = [strict & (ri // WKV_BASE == ci // WKV_BASE)]
    width = WKV_BASE
    while width < c64:
        blk_mask.append((ri // (2 * width) == ci // (2 * width)) & (ri % (2 * width) >= width)
                        & (ci % (2 * width) < width))
        width *= 2

    heads = range(RWKV_HEADS)
    sls = [slice(h * HEAD_DIM, (h + 1) * HEAD_DIM) for h in heads]

    def chunk_terms(c):
        rows = pl.ds(pl.multiple_of(c * c64, c64), c64)
        lwc = lw_s[rows, :]
        lhi, llo = _split2(lwc)
        cl = _mm(tri, lhi) + _mm(tri, llo)
        cl_last = cl[c64 - 1:c64, :]
        e_neg = jnp.exp(-cl)
        e_tot = jnp.exp(cl_last - cl)
        ac = a_s[rows, :]
        bc = b_s[rows, :]
        kc = k_s[rows, :]
        r_t = r_s[rows, :] * jnp.exp(cl)
        rt_s[rows, :] = r_t
        bh_s[rows, :] = bc * e_tot
        kh_s[rows, :] = kc * e_tot
        pc_s[pl.ds(c, 1), :] = jnp.exp(cl_last)
        return rows, ac * jnp.exp(cl - lwc), r_t, bc * e_neg, kc * e_neg, v_s[rows, :]

    def intra_body(c2, carry):
        terms = [chunk_terms(c2 * WKV_PAIR + j) for j in range(WKV_PAIR)]
        probs = [(j, h) for j in range(WKV_PAIR) for h in heads]
        cat = lambda j, p, q, h: jnp.concatenate([terms[j][p][:, sls[h]], terms[j][q][:, sls[h]]],
                                                 axis=0).astype(bf16)
        gm = [_nt(cat(j, 1, 2, h), cat(j, 3, 4, h)) for j, h in probs]
        a_ab = [jnp.where(strict, g[:c64, :c64], 0.0) for g in gm]
        a_ak = [jnp.where(strict, g[:c64, c64:], 0.0).astype(bf16) for g in gm]
        a_rb = [jnp.where(incl, g[c64:, :c64], 0.0).astype(bf16) for g in gm]
        a_rk = [jnp.where(incl, g[c64:, c64:], 0.0).astype(bf16) for g in gm]
        vb = [terms[j][5][:, sls[h]].astype(bf16) for j, h in probs]
        akv = [_mm(a, v) for a, v in zip(a_ak, vb)]
        yloc = [_mm(a, v) for a, v in zip(a_rk, vb)]
        base = [jnp.where(blk_mask[0], a, 0.0) for a in a_ab]
        bb = [x.astype(bf16) for x in base]
        sq = [_mm(x, x) for x in bb]
        t_inv = [eye + b + s + _mm(b.astype(bf16), s.astype(bf16)) for b, s in zip(base, sq)]
        for lvl in range(1, len(blk_mask)):
            tb16 = [t.astype(bf16) for t in t_inv]
            low = [_mm(jnp.where(blk_mask[lvl], a, 0.0).astype(bf16), t) for a, t in zip(a_ab, tb16)]
            t_inv = [t + _mm(tb, lo.astype(bf16)) for t, tb, lo in zip(t_inv, tb16, low)]
        rhs = [jnp.concatenate([terms[j][1][:, sls[h]], akv[n]], axis=1).astype(bf16)
               for n, (j, h) in enumerate(probs)]
        tw = [_mm(t.astype(bf16), r) for t, r in zip(t_inv, rhs)]
        for j in range(WKV_PAIR):
            mine = slice(j * RWKV_HEADS, (j + 1) * RWKV_HEADS)
            rows = terms[j][0]
            tw_s[rows, :] = jnp.concatenate(tw[mine], axis=1)
            arb_s[rows, :] = jnp.concatenate(a_rb[mine], axis=1)
            y_s[rows, :] = jnp.concatenate(yloc[mine], axis=1)
        return carry

    lax.fori_loop(0, n_chunks // WKV_PAIR, intra_body, 0)

    def state_body(c, carry):
        rows = pl.ds(pl.multiple_of(c * c64, c64), c64)
        twc = tw_s[rows, :]
        arbc = arb_s[rows, :]
        r_t = rt_s[rows, :]
        b_h = bh_s[rows, :]
        k_h = kh_s[rows, :]
        vc = v_s[rows, :]
        yl = y_s[rows, :]
        p_c = pc_s[pl.ds(c, 1), :]
        s_old = [state_s[h] for h in heads]
        wr = [jnp.concatenate([twc[:, h * LANES:h * LANES + c64], r_t[:, sls[h]]], axis=0).astype(bf16)
              for h in heads]
        wrs = [_nt(wr[h], s_old[h].astype(bf16)) for h in heads]
        u = [wrs[h][:c64] + twc[:, h * LANES + c64:(h + 1) * LANES] for h in heads]
        ys = [wrs[h][c64:] + _mm(arbc[:, sls[h]], u[h].astype(bf16)) + yl[:, sls[h]] for h in heads]
        uv = [jnp.concatenate([u[h], vc[:, sls[h]]], axis=0).astype(bf16) for h in heads]
        bkh = [jnp.concatenate([b_h[:, sls[h]], k_h[:, sls[h]]], axis=0).astype(bf16) for h in heads]
        for h in heads:
            state_s[h] = s_old[h] * p_c[:, sls[h]] + _tn(uv[h], bkh[h])
        y_s[rows, :] = jnp.concatenate(ys, axis=1)
        return carry

    lax.fori_loop(0, n_chunks, state_body, 0)

    y = y_s[...]
    mean = segsum(y) * (1.0 / HEAD_DIM)
    d = y - mean
    var = segsum(d * d) * (1.0 / HEAD_DIM)
    yn = d * lax.rsqrt(var + RWKV_GN_EPS) * lnw_ref[...] + lnb_ref[...]
    o_ref[0] = (yn + bon_s[...] * v_s[...]) * g_s[...]


def _rwkv(x2, w_all, layer, p, bsz, seq, tb=512):
    tb = min(tb, seq)
    n_l = seq // tb
    seg = (np.arange(MXU_TILE)[:, None] // HEAD_DIM == np.arange(MXU_TILE)[None, :] // HEAD_DIM)
    seg = jnp.asarray(seg, bf16)
    tri = jnp.asarray(np.tril(np.ones((WKV_CHUNK, WKV_CHUNK))), bf16)
    row = lambda a: a.reshape(1, -1).astype(f32)
    args = [x2, w_all, row(p['mu']), row(p['w0']), p['w2'].astype(bf16), row(p['a0']), p['a2'].astype(bf16),
            p['g2'].astype(bf16), row(p['k_k']), row(p['k_a']), row(p['r_k']), row(p['ln_w']),
            row(p['ln_b']), seg, tri]
    full = lambda a: pl.BlockSpec(a.shape, lambda b, l: (0,) * a.ndim)
    in_specs = ([pl.BlockSpec((tb, D_MODEL), lambda b, l: (b * n_l + l, 0)), _w_window(layer, 0, RWKV_COLS)]
                + [full(a) for a in args[2:]])
    big = lambda: pltpu.VMEM((tb, D_RWKV), f32)
    return pl.pallas_call(
        _rwkv_kernel,
        grid=(bsz, n_l),
        in_specs=in_specs,
        out_specs=pl.BlockSpec((1, tb, D_RWKV), lambda b, l: (b, l, 0)),
        out_shape=jax.ShapeDtypeStruct((bsz, seq, D_RWKV), f32),
        scratch_shapes=[pltpu.VMEM((1, RWKV_COLS), f32),
                        pltpu.VMEM((RWKV_HEADS, HEAD_DIM, HEAD_DIM), f32),
                        big(), big(), big(), big(), big(), big(), big(), big(), big(),
                        big(), big(), big(), pltpu.VMEM((tb, RWKV_HEADS * LANES), f32),
                        pltpu.VMEM((tb, D_RWKV), bf16), pltpu.VMEM((tb // WKV_CHUNK, D_RWKV), f32)],
        compiler_params=pltpu.CompilerParams(
            dimension_semantics=("parallel", "arbitrary"), vmem_limit_bytes=VMEM_LIMIT),
    )(*args)


MOBA_AUG_POS = 0
MOBA_AUG_BLK = 6
LOG2E = 1.4426950408889634


MOBA_STEP_HEADS = 4


def _moba_kernel(q_ref, k_ref, v_ref, tmpl_ref, o_ref, kaug_s, vaug_s, kmean_s):
    blk = MOBA_BLOCK
    half = HEAD_DIM
    seq = k_ref.shape[1]
    n_blk = seq // blk
    qi = pl.program_id(2)
    heads = range(MOBA_STEP_HEADS)
    tile = [slice((hh // 2) * LANES, (hh // 2 + 1) * LANES) for hh in heads]
    aug0 = [half if hh % 2 == 0 else 0 for hh in heads]

    @pl.when(qi == 0)
    def _():
        lane_l = lax.broadcasted_iota(jnp.int32, (seq, LANES), 1)
        for hh in heads:
            kb = k_ref[0, :, tile[hh]].astype(bf16)
            vb = v_ref[0, :, tile[hh]].astype(bf16)
            in_head = (lane_l < half) if hh % 2 == 0 else (lane_l >= half)
            kaug_s[hh] = jnp.where(in_head, kb, tmpl_ref[0, hh])
            ones_lane = jnp.where(lane_l == aug0[hh], 1.0, 0.0).astype(bf16)
            vaug_s[hh] = jnp.where(in_head, vb, ones_lane)
        for n in range(n_blk):
            km = jnp.mean(k_ref[0, n * blk:(n + 1) * blk, :], axis=0, keepdims=True)
            for tt in range(MOBA_STEP_HEADS // 2):
                kmean_s[tt, n:n + 1, :] = km[:, tt * LANES:(tt + 1) * LANES]

    ri = lax.broadcasted_iota(jnp.int32, (blk, blk), 0)
    ci = lax.broadcasted_iota(jnp.int32, (blk, blk), 1)
    causal = ri >= ci
    lane = lax.broadcasted_iota(jnp.int32, (blk, LANES), 1)
    lane_f = lane.astype(f32)
    qf = [q_ref[0, :, tile[hh]] * (HEAD_DIM ** -0.5 * LOG2E) for hh in heads]
    in_head = [(lane < half) if hh % 2 == 0 else (lane >= half) for hh in heads]
    blk_lane = [lane - (aug0[hh] + MOBA_AUG_BLK) for hh in heads]

    qs = [_split2(jnp.where(in_head[hh], qf[hh], 0.0)) for hh in heads]
    kms = [_split2(kmean_s[tt]) for tt in range(MOBA_STEP_HEADS // 2)]
    gate = [_nt(kms[hh // 2][0], qs[hh][0]) + _nt(kms[hh // 2][1], qs[hh][0])
            + _nt(kms[hh // 2][0], qs[hh][1]) for hh in heads]
    is_pos = [(lane >= aug0[hh] + MOBA_AUG_POS) & (lane < aug0[hh] + MOBA_AUG_POS + 3) for hh in heads]
    own = [jnp.where(is_pos[hh], 1.0, 0.0) for hh in heads]
    q_own = [jnp.where(in_head[hh], qf[hh], own[hh]).astype(bf16) for hh in heads]

    start = pl.multiple_of(qi * blk, blk)
    s0 = [_nt(q_own[hh], kaug_s[hh, pl.ds(start, blk), :]) for hh in heads]
    s0 = [jnp.where(causal, s, NEG_INF) for s in s0]
    m0 = [jnp.max(s, axis=-1, keepdims=True) for s in s0]
    p0 = [jnp.exp2(s - m).astype(bf16) for s, m in zip(s0, m0)]
    a0 = [_mm(p0[hh], vaug_s[hh, pl.ds(start, blk), :]) for hh in heads]

    brow = lax.broadcasted_iota(jnp.int32, (n_blk, blk), 0)
    brow_f = brow.astype(f32)
    g = [jnp.where(brow < qi, x, -jnp.inf) for x in gate]
    picked = [jnp.zeros((n_blk, blk), f32) for _ in heads]
    for _ in range(MOBA_TOPK):
        mx = [jnp.max(x, axis=0, keepdims=True) for x in g]
        idx = [jnp.min(jnp.where(g[hh] == mx[hh], brow_f, 1e9), axis=0, keepdims=True) for hh in heads]
        hit = [brow_f == i for i in idx]
        picked = [jnp.where(hit[hh], 1.0, picked[hh]) for hh in heads]
        g = [jnp.where(hit[hh], -jnp.inf, g[hh]) for hh in heads]
    sel_t = [jnp.where(brow < qi, p, 0.0).astype(bf16) for p in picked]
    lane_b = lax.broadcasted_iota(jnp.int32, (n_blk, LANES), 1)
    brow_b = lax.broadcasted_iota(jnp.int32, (n_blk, LANES), 0)
    q_aug = []
    for hh in heads:
        place = jnp.where(lane_b == brow_b + (aug0[hh] + MOBA_AUG_BLK), 1.0, 0.0).astype(bf16)
        sel = _tn(sel_t[hh], place) > 0.5
        is_blk = (blk_lane[hh] >= 0) & (blk_lane[hh] < n_blk)
        aug = jnp.where(is_blk, jnp.where(sel, 0.0, NEG_INF), own[hh])
        q_aug.append(jnp.where(in_head[hh], qf[hh], aug).astype(bf16))

    def make_body(width, base):
        def body(t, carry):
            ms, accs = carry
            rows = pl.ds(pl.multiple_of(base + t * (width * blk), 2 * blk), width * blk)
            s = [_nt(q_aug[hh], kaug_s[hh, rows, :]) for hh in heads]
            m_new = [jnp.maximum(ms[hh], jnp.max(s[hh], axis=-1, keepdims=True)) for hh in heads]
            p = [jnp.exp2(s[hh] - m_new[hh]).astype(bf16) for hh in heads]
            pv = [_mm(p[hh], vaug_s[hh, rows, :]) for hh in heads]
            accs = [accs[hh] * jnp.exp2(ms[hh] - m_new[hh]) + pv[hh] for hh in heads]
            return (tuple(m_new), tuple(accs))
        return body

    n_pairs = (qi + 1) // 2
    n_quads = n_pairs // 2
    carry = lax.fori_loop(0, n_quads, make_body(4, 0), (tuple(m0), tuple(a0)))
    _, accs = lax.fori_loop(0, n_pairs - 2 * n_quads, make_body(2, n_quads * (4 * blk)), carry)
    out = [accs[hh] / accs[hh][:, aug0[hh]:aug0[hh] + 1] for hh in heads]
    for tt in range(MOBA_STEP_HEADS // 2):
        o_ref[0, :, tt * LANES:(tt + 1) * LANES] = jnp.where(lane < half, out[2 * tt], out[2 * tt + 1])


def _moba_template(seq):
    pos = np.arange(seq, dtype=np.float64)
    tmpl = np.zeros((MOBA_HEADS, seq, LANES), np.float32)
    for h in range(MOBA_HEADS):
        a0 = HEAD_DIM if h % 2 == 0 else 0
        slope = 2.0 ** (-8.0 * (h + 1) / MOBA_HEADS)
        rest = slope * LOG2E * pos
        for c in range(3):
            part = rest.astype(np.float32).astype(bf16).astype(np.float64)
            tmpl[h, :, a0 + MOBA_AUG_POS + c] = part
            rest = rest - part
        tmpl[h, np.arange(seq), a0 + MOBA_AUG_BLK + np.arange(seq) // MOBA_BLOCK] = 1.0
    return jnp.asarray(tmpl.reshape(MOBA_HEADS // MOBA_STEP_HEADS, MOBA_STEP_HEADS, seq, LANES), bf16)


def _moba(qkv3):
    bsz, seq, _ = qkv3.shape
    blk = MOBA_BLOCK
    sh = MOBA_STEP_HEADS
    n_grp = MOBA_HEADS // sh
    wid = sh * HEAD_DIM
    assert seq % (2 * blk) == 0 and seq // blk <= HEAD_DIM - MOBA_AUG_BLK
    return pl.pallas_call(
        _moba_kernel,
        grid=(bsz, n_grp, seq // blk),
        in_specs=[pl.BlockSpec((1, blk, wid), lambda b, p, i: (b, i, p)),
                  pl.BlockSpec((1, seq, wid), lambda b, p, i: (b, 0, n_grp + p)),
                  pl.BlockSpec((1, seq, wid), lambda b, p, i: (b, 0, 2 * n_grp + p)),
                  pl.BlockSpec((1, sh, seq, LANES), lambda b, p, i: (p, 0, 0, 0))],
        out_specs=pl.BlockSpec((1, blk, wid), lambda b, p, i: (b, i, p)),
        out_shape=jax.ShapeDtypeStruct((bsz, seq, D_MOBA), f32),
        scratch_shapes=[pltpu.VMEM((sh, seq, LANES), bf16), pltpu.VMEM((sh, seq, LANES), bf16),
                        pltpu.VMEM((sh // 2, seq // blk, LANES), f32)],
        compiler_params=pltpu.CompilerParams(
            dimension_semantics=("parallel", "parallel", "arbitrary"), vmem_limit_bytes=VMEM_LIMIT),
    )(qkv3, qkv3, qkv3, _moba_template(seq))


S5_SUPER = 4
S5_SEGS = 8
S5_COLS = 512


def _s5_kernel(x_ref, wu_ref, perm_ref, bd_ref, cre_ref, cim_ref, lam_ref, d_ref, gw_ref, gb_ref, o_ref,
               hr_s, hi_s, carry_s):
    tb = x_ref.shape[0]
    seg = tb // S5_SEGS
    n_state = lam_ref.shape[1]
    sgw = n_state // S5_SUPER
    uw = D_SSM // S5_SUPER

    @pl.when(pl.program_id(1) == 0)
    def _():
        carry_s[...] = jnp.zeros_like(carry_s)

    u = _mm(x_ref[...].astype(bf16), wu_ref[0])
    perm = perm_ref[...]
    ub = _mm(perm, u.astype(bf16)).astype(bf16)
    for sg in range(S5_SUPER):
        bu = _mm(ub[:, sg * uw:(sg + 1) * uw], bd_ref[sg])
        for j in range(sgw // LANES):
            hr_s[sg * (sgw // LANES) + j] = bu[:, j * LANES:(j + 1) * LANES]
            hi_s[sg * (sgw // LANES) + j] = bu[:, sgw + j * LANES:sgw + (j + 1) * LANES]

    tpc = S5_COLS // LANES

    def load(ref, cc, idx):
        return jnp.concatenate([ref[cc * tpc + j, idx, :] for j in range(tpc)], axis=1)

    def store(ref, cc, idx, val):
        for j in range(tpc):
            ref[cc * tpc + j, idx, :] = val[:, j * LANES:(j + 1) * LANES]

    for cc in range(n_state // S5_COLS):
        cols = slice(cc * S5_COLS, (cc + 1) * S5_COLS)
        lr = lam_ref[0:1, cols]
        li = lam_ref[1:2, cols]

        def local(r, carry, cc=cc, lr=lr, li=li):
            hr, hi = carry
            idx = pl.ds(pl.multiple_of(r * S5_SEGS, S5_SEGS), S5_SEGS)
            nr = lr * hr - li * hi + load(hr_s, cc, idx)
            ni = lr * hi + li * hr + load(hi_s, cc, idx)
            store(hr_s, cc, idx, nr)
            store(hi_s, cc, idx, ni)
            return nr, ni

        zero = jnp.zeros((S5_SEGS, S5_COLS), f32)
        er, ei = lax.fori_loop(0, seg, local, (zero, zero), unroll=4)
        pr, pi = lr, li
        for _ in range(seg.bit_length() - 1):
            pr, pi = pr * pr - pi * pi, 2.0 * pr * pi
        cr = carry_s[0:1, cols]
        ci = carry_s[1:2, cols]
        in_r, in_i = [], []
        for s in range(S5_SEGS):
            in_r.append(cr)
            in_i.append(ci)
            cr, ci = pr * cr - pi * ci + er[s:s + 1, :], pr * ci + pi * cr + ei[s:s + 1, :]
        carry_s[0:1, cols] = cr
        carry_s[1:2, cols] = ci

        def fix(r, carry, cc=cc, lr=lr, li=li):
            qr, qi = carry
            qr, qi = lr * qr - li * qi, lr * qi + li * qr
            idx = pl.ds(pl.multiple_of(r * S5_SEGS, S5_SEGS), S5_SEGS)
            store(hr_s, cc, idx, load(hr_s, cc, idx) + qr)
            store(hi_s, cc, idx, load(hi_s, cc, idx) + qi)
            return qr, qi

        lax.fori_loop(0, seg, fix, (jnp.concatenate(in_r, axis=0), jnp.concatenate(in_i, axis=0)),
                      unroll=4)

    slab = lambda ref, sg: jnp.concatenate(
        [ref[sg * (sgw // LANES) + j] for j in range(sgw // LANES)], axis=1).astype(bf16)
    ys = [_mm(slab(hr_s, sg), cre_ref[sg]) + _mm(slab(hi_s, sg), cim_ref[sg]) for sg in range(S5_SUPER)]
    yh, yl = _split2(jnp.concatenate(ys, axis=1))
    y = _tn(perm, yh) + _tn(perm, yl) + d_ref[...] * u
    y = 0.5 * y * (1.0 + jnp.tanh(0.7978845608028654 * (y + 0.044715 * (y * y * y))))
    zg = _mm(y.astype(bf16), gw_ref[...]) + gb_ref[...]
    o_ref[0] = zg[:, :D_SSM] * jax.nn.sigmoid(zg[:, D_SSM:])


def _s5_params(p):
    a_re = p['a_re'].astype(f32)
    a_im = p['a_im'].astype(f32)
    dt = jnp.exp(p['log_dt'].astype(f32))[:, None]
    mag = jnp.exp(a_re * dt)
    lam_re = mag * jnp.cos(a_im * dt)
    lam_im = mag * jnp.sin(a_im * dt)
    den = a_re * a_re + a_im * a_im
    nr = lam_re - 1.0
    coef_re = (nr * a_re + lam_im * a_im) / den
    coef_im = (lam_im * a_re - nr * a_im) / den
    b_re = p['b_re'].astype(f32)
    b_im = p['b_im'].astype(f32)
    bb_re = coef_re[..., None] * b_re - coef_im[..., None] * b_im
    bb_im = coef_re[..., None] * b_im + coef_im[..., None] * b_re
    gps = SSM_GROUPS // S5_SUPER
    eye = jnp.eye(gps, dtype=f32)
    n_state = SSM_GROUPS * SSM_STATE
    sgw = n_state // S5_SUPER
    uw = D_SSM // S5_SUPER

    def in_map(bb):
        bb = bb.reshape(S5_SUPER, gps, SSM_STATE, SSM_GROUP)
        return jnp.einsum('sgph,gk->sghkp', bb, eye).reshape(S5_SUPER, uw, sgw)

    def out_map(c):
        c = c.reshape(S5_SUPER, gps, SSM_GROUP, SSM_STATE)
        return jnp.einsum('sghp,gk->sgpkh', c, eye).reshape(S5_SUPER, sgw, uw)

    bd = jnp.concatenate([in_map(bb_re), in_map(bb_im)], axis=2)
    cre = out_map(p['c_re'].astype(f32))
    cim = -out_map(p['c_im'].astype(f32))
    lam = jnp.stack([lam_re.reshape(n_state), lam_im.reshape(n_state)], axis=0)
    return bd.astype(bf16), cre.astype(bf16), cim.astype(bf16), lam


def _s5(x2, w_all, layer, p, bsz, seq, tb=512):
    tb = min(tb, seq)
    n_l = seq // tb
    seg = tb // S5_SEGS
    assert tb % S5_SEGS == 0 and seg & (seg - 1) == 0
    n_state = SSM_GROUPS * SSM_STATE
    bd, cre, cim, lam = _s5_params(p)
    src = (np.arange(tb) % S5_SEGS) * seg + np.arange(tb) // S5_SEGS
    perm = jnp.asarray(np.arange(tb)[None, :] == src[:, None], bf16)
    args = [x2, w_all, perm, bd, cre, cim, lam, p['d'].reshape(1, -1).astype(f32), p['glu_w'].astype(bf16),
            p['glu_b'].reshape(1, -1).astype(f32)]
    full = lambda a: pl.BlockSpec(a.shape, lambda b, l: (0,) * a.ndim)
    return pl.pallas_call(
        _s5_kernel,
        grid=(bsz, n_l),
        in_specs=[pl.BlockSpec((tb, D_MODEL), lambda b, l: (b * n_l + l, 0)),
                  _w_window(layer, OFF_SSM, D_SSM)] + [full(a) for a in args[2:]],
        out_specs=pl.BlockSpec((1, tb, D_SSM), lambda b, l: (b, l, 0)),
        out_shape=jax.ShapeDtypeStruct((bsz, seq, D_SSM), f32),
        scratch_shapes=[pltpu.VMEM((n_state // LANES, tb, LANES), f32),
                        pltpu.VMEM((n_state // LANES, tb, LANES), f32),
                        pltpu.VMEM((2, n_state), f32)],
        compiler_params=pltpu.CompilerParams(
            dimension_semantics=("parallel", "arbitrary"), vmem_limit_bytes=VMEM_LIMIT),
    )(*args)


def _merge_kernel(x_ref, ya_ref, yb_ref, yc_ref, wg_ref, gb_ref, wa_ref, wb_ref, wc_ref, wo_ref,
                  lnw_ref, lnb_ref, o_ref):
    x = x_ref[...]
    xb = x.astype(bf16)
    merged = None
    for br, (y_ref, w_ref) in enumerate(((ya_ref, wa_ref), (yb_ref, wb_ref), (yc_ref, wc_ref))):
        cols = slice(br * D_MODEL, (br + 1) * D_MODEL)
        gate = jax.nn.sigmoid(_mm(xb, wg_ref[0, :, cols]) + gb_ref[:, cols])
        term = gate * _mm(y_ref[...].astype(bf16), w_ref[...])
        merged = term if merged is None else merged + term
    h = _mm(merged.astype(bf16), wo_ref[...])
    o_ref[...] = _layer_norm(DEEPNORM_ALPHA * x + h, lnw_ref[...], lnb_ref[...])


def _merge(x2, ya, yb, yc, w_all, layer, gb, wa, wb, wc, wo, lnw, lnb, tm=512):
    m = x2.shape[0]
    rowblk = lambda n: pl.BlockSpec((tm, n), lambda i: (i, 0))
    full = lambda a: pl.BlockSpec(a.shape, lambda i: (0,) * a.ndim)
    consts = [gb, wa, wb, wc, wo, lnw, lnb]
    return pl.pallas_call(
        _merge_kernel,
        grid=(m // tm,),
        in_specs=[rowblk(D_MODEL), rowblk(D_RWKV), rowblk(D_MOBA), rowblk(D_SSM),
                  _w_window(layer, OFF_GATE, N_BRANCHES * D_MODEL)] + [full(a) for a in consts],
        out_specs=rowblk(D_MODEL),
        out_shape=jax.ShapeDtypeStruct((m, D_MODEL), f32),
        compiler_params=pltpu.CompilerParams(
            dimension_semantics=("parallel",), vmem_limit_bytes=VMEM_LIMIT),
    )(x2, ya, yb, yc, w_all, *consts)


MOE_STEP_EXPERTS = 4
MOE_CHUNK = 256
MOE_STEPS = N_EXPERTS // MOE_STEP_EXPERTS
STEPS_PER_GROUP = EXPERTS_PER_GROUP // MOE_STEP_EXPERTS


def _moe_kernel(x_ref, rwh_ref, rwl_ref, rb_ref, wg_ref, wu_ref, wd_ref, lnw_ref, lnb_ref, o_ref,
                xs_s, outs_s, wts_s, pos_s, posrow_s, meta_s):
    s = pl.program_id(1)
    tm = x_ref.shape[0]
    ck = MOE_CHUNK
    lane = lax.broadcasted_iota(jnp.int32, (tm, LANES), 1)
    lane_f = lane.astype(f32)

    @pl.when(s == 0)
    def _():
        xh, xl = _split2(x_ref[...])
        rwh = rwh_ref[...]
        logits = _mm(xh, rwh) + _mm(xh, rwl_ref[...]) + _mm(xl, rwh) + rb_ref[...]
        is_grp = (lane >= N_EXPERTS) & (lane < N_EXPERTS + N_EXPERT_GROUPS)
        gl = jnp.where(is_grp, logits, -jnp.inf)
        gmax = jnp.max(gl, axis=-1, keepdims=True)
        gidx = jnp.min(jnp.where(gl == gmax, lane_f, 1e9), axis=-1, keepdims=True) - float(N_EXPERTS)
        p_group = 1.0 / jnp.sum(jnp.where(is_grp, jnp.exp(gl - gmax), 0.0), axis=-1, keepdims=True)
        grp_of_lane = jnp.floor(lane_f * (1.0 / EXPERTS_PER_GROUP))
        in_grp = (lane < N_EXPERTS) & (grp_of_lane == gidx)
        el = jnp.where(in_grp, logits, -jnp.inf)
        m1 = jnp.max(el, axis=-1, keepdims=True)
        i1 = jnp.min(jnp.where(el == m1, lane_f, 1e9), axis=-1, keepdims=True)
        el2 = jnp.where(lane_f == i1, -jnp.inf, el)
        m2 = jnp.max(el2, axis=-1, keepdims=True)
        i2 = jnp.min(jnp.where(el2 == m2, lane_f, 1e9), axis=-1, keepdims=True)
        e2 = jnp.exp(m2 - m1)
        w1 = p_group / (1.0 + e2)
        w2 = p_group * e2 / (1.0 + e2)
        wt = jnp.where(lane_f == i1, w1, 0.0) + jnp.where(lane_f == i2, w2, 0.0)

        ohg = lane_f == gidx
        ri = lax.broadcasted_iota(jnp.int32, (tm, tm), 0)
        ci = lax.broadcasted_iota(jnp.int32, (tm, tm), 1)
        tri = jnp.where(ri >= ci, 1.0, 0.0).astype(bf16)
        cg = _mm(tri, jnp.where(ohg, 1.0, 0.0).astype(bf16))
        cnt = cg[tm - 1:tm, :]
        padded = jnp.ceil(cnt * (1.0 / ck)) * ck
        ui = lax.broadcasted_iota(jnp.int32, (LANES, LANES), 0)
        uj = lax.broadcasted_iota(jnp.int32, (LANES, LANES), 1)
        upper = jnp.where(ui < uj, 1.0, 0.0).astype(bf16)
        off = _mm(jnp.broadcast_to(padded, (8, LANES)).astype(bf16), upper)[0:1, :]
        rank = jnp.sum(jnp.where(ohg, cg - 1.0, 0.0), axis=-1, keepdims=True)
        pos = jnp.sum(jnp.where(ohg, off, 0.0), axis=-1, keepdims=True) + rank
        pos_s[...] = jnp.broadcast_to(pos, (tm, LANES))
        pa = jnp.floor(pos * (1.0 / 64.0))
        pb = pos - 64.0 * pa
        cols = jnp.where(lane == 0, pa, jnp.where(lane == 1, pb, 0.0)).astype(bf16)
        r8 = lax.broadcasted_iota(jnp.int32, (8, LANES), 0)
        l8 = lax.broadcasted_iota(jnp.int32, (8, LANES), 1)
        sel8 = jnp.where((r8 == 0) & (l8 == 0), 64.0, jnp.where((r8 == 0) & (l8 == 1), 1.0, 0.0))
        posrow_s[...] = _nt(sel8.astype(bf16), cols)
        off_i = off.astype(jnp.int32)
        nck_i = (padded * (1.0 / ck)).astype(jnp.int32)
        total = 0
        for g in range(N_EXPERT_GROUPS):
            meta_s[g] = off_i[0, g]
            meta_s[N_EXPERT_GROUPS + g] = nck_i[0, g]
            total = total + nck_i[0, g]
        meta_s[2 * N_EXPERT_GROUPS] = total

        wth, wtl = _split2(wt)
        rows_f = lax.broadcasted_iota(jnp.int32, (ck, tm), 0).astype(f32)
        posrow = posrow_s[0:1, :]

        def fill(c, carry):
            r0 = pl.multiple_of(c * ck, ck)
            pc = jnp.where(rows_f + r0.astype(f32) == posrow, 1.0, 0.0).astype(bf16)
            xs_s[pl.ds(r0, ck), :] = _mm(pc, xh).astype(bf16)
            wts_s[pl.ds(r0, ck), :] = _mm(pc, wth) + _mm(pc, wtl)
            outs_s[pl.ds(r0, ck), :] = jnp.zeros((ck, D_MODEL), f32)
            return carry

        lax.fori_loop(0, total, fill, 0)

    grp = s // STEPS_PER_GROUP
    off_g = meta_s[grp]
    lane_c = lax.broadcasted_iota(jnp.int32, (ck, LANES), 1)

    def chunk(c, carry):
        rows = pl.ds(pl.multiple_of(off_g + c * ck, ck), ck)
        xs = xs_s[rows, :]
        wts = wts_s[rows, :]
        gact = [_mm(xs, wg_ref[j]) for j in range(MOE_STEP_EXPERTS)]
        up = [_mm(xs, wu_ref[j]) for j in range(MOE_STEP_EXPERTS)]
        part = None
        for j in range(MOE_STEP_EXPERTS):
            wcol = jnp.sum(jnp.where(lane_c == s * MOE_STEP_EXPERTS + j, wts, 0.0), axis=-1, keepdims=True)
            hid = gact[j] * jax.nn.sigmoid(gact[j]) * up[j] * wcol
            out = _mm(hid.astype(bf16), wd_ref[j])
            part = out if part is None else part + out
        outs_s[rows, :] += part
        return carry

    lax.fori_loop(0, meta_s[N_EXPERT_GROUPS + grp], chunk, 0)

    @pl.when(s == MOE_STEPS - 1)
    def _():
        o_ref[...] = DEEPNORM_ALPHA * x_ref[...]
        pos_col = pos_s[:, 0:1]
        cols_f = lax.broadcasted_iota(jnp.int32, (tm, ck), 1).astype(f32)

        def unsort(c, carry):
            r0 = pl.multiple_of(c * ck, ck)
            pt = jnp.where(cols_f + r0.astype(f32) == pos_col, 1.0, 0.0).astype(bf16)
            o_ref[...] += _mm(pt, outs_s[pl.ds(r0, ck), :].astype(bf16))
            return carry

        lax.fori_loop(0, meta_s[2 * N_EXPERT_GROUPS], unsort, 0)
        o_ref[...] = _layer_norm(o_ref[...], lnw_ref[...], lnb_ref[...])


def _moe(x2, rw, rb, wg, wu, wd, lnw, lnb, tm=1024):
    m = x2.shape[0]
    tm = min(tm, m)
    rwh, rwl = _split2(rw)
    n_rows = tm + N_EXPERT_GROUPS * MOE_CHUNK
    full = lambda a: pl.BlockSpec(a.shape, lambda i, e: (0,) * a.ndim)
    return pl.pallas_call(
        _moe_kernel,
        grid=(m // tm, MOE_STEPS),
        in_specs=[pl.BlockSpec((tm, D_MODEL), lambda i, e: (i, 0)),
                  full(rwh), full(rwl), full(rb),
                  pl.BlockSpec((MOE_STEP_EXPERTS, D_MODEL, D_EXPERT), lambda i, e: (e, 0, 0)),
                  pl.BlockSpec((MOE_STEP_EXPERTS, D_MODEL, D_EXPERT), lambda i, e: (e, 0, 0)),
                  pl.BlockSpec((MOE_STEP_EXPERTS, D_EXPERT, D_MODEL), lambda i, e: (e, 0, 0)),
                  full(lnw), full(lnb)],
        out_specs=pl.BlockSpec((tm, D_MODEL), lambda i, e: (i, 0)),
        out_shape=jax.ShapeDtypeStruct((m, D_MODEL), f32),
        scratch_shapes=[pltpu.VMEM((n_rows, D_MODEL), bf16), pltpu.VMEM((n_rows, D_MODEL), f32),
                        pltpu.VMEM((n_rows, LANES), f32), pltpu.VMEM((tm, LANES), f32),
                        pltpu.VMEM((8, tm), f32), pltpu.SMEM((2 * N_EXPERT_GROUPS + 1,), jnp.int32)],
        compiler_params=pltpu.CompilerParams(
            dimension_semantics=("parallel", "arbitrary"), vmem_limit_bytes=VMEM_LIMIT),
    )(x2, rwh, rwl, rb, wg, wu, wd, lnw, lnb)


def _router_weights(router_group_w, router_group_b, router_expert_w, router_expert_b):
    pad = LANES - N_EXPERTS - N_EXPERT_GROUPS
    rw = jnp.concatenate([router_expert_w.astype(f32), router_group_w.astype(f32),
                          jnp.zeros((D_MODEL, pad), f32)], axis=1)
    rb = jnp.concatenate([router_expert_b.astype(f32), router_group_b.astype(f32),
                          jnp.zeros((pad,), f32)]).reshape(1, LANES)
    return rw, rb


def kernel(x, w_in, rwkv_mu, rwkv_w0, rwkv_w2, rwkv_a0, rwkv_a2, rwkv_g2, rwkv_k_k, rwkv_k_a, rwkv_r_k, rwkv_ln_w, rwkv_ln_b, ssm_a_re, ssm_a_im, ssm_b_re, ssm_b_im, ssm_c_re, ssm_c_im, ssm_d, ssm_log_dt, ssm_glu_w, ssm_glu_b, w_up_rwkv, w_up_moba, w_up_ssm, gate_b, w_out, ln1_w, ln1_b, router_group_w, router_group_b, router_expert_w, router_expert_b, expert_w_gate, expert_w_up, expert_w_down, ln2_w, ln2_b):
    bsz, seq, _ = x.shape
    x2 = x.reshape(bsz * seq, D_MODEL).astype(f32)
    row = lambda a: a.reshape(1, -1).astype(f32)
    w_all = w_in.astype(bf16)
    for l in range(DEPTH):
        qkv = _project(x2, w_all, l, OFF_MOBA, MOBA_COLS)
        y_a = _rwkv(x2, w_all, l,
                    dict(mu=rwkv_mu[l], w0=rwkv_w0[l], w2=rwkv_w2[l], a0=rwkv_a0[l], a2=rwkv_a2[l],
                         g2=rwkv_g2[l], k_k=rwkv_k_k[l], k_a=rwkv_k_a[l], r_k=rwkv_r_k[l],
                         ln_w=rwkv_ln_w[l], ln_b=rwkv_ln_b[l]), bsz, seq)
        y_b = _moba(qkv.reshape(bsz, seq, MOBA_COLS))
        y_c = _s5(x2, w_all, l,
                  dict(a_re=ssm_a_re[l], a_im=ssm_a_im[l], b_re=ssm_b_re[l], b_im=ssm_b_im[l],
                       c_re=ssm_c_re[l], c_im=ssm_c_im[l], d=ssm_d[l], log_dt=ssm_log_dt[l],
                       glu_w=ssm_glu_w[l], glu_b=ssm_glu_b[l]), bsz, seq)
        x2 = _merge(x2, y_a.reshape(-1, D_RWKV), y_b.reshape(-1, D_MOBA), y_c.reshape(-1, D_SSM),
                    w_all, l, row(gate_b[l]),
                    w_up_rwkv[l].astype(bf16), w_up_moba[l].astype(bf16), w_up_ssm[l].astype(bf16),
                    w_out[l].astype(bf16), row(ln1_w[l]), row(ln1_b[l]))
        rw, rb = _router_weights(router_group_w[l], router_group_b[l], router_expert_w[l],
                                 router_expert_b[l])
        x2 = _moe(x2, rw, rb, expert_w_gate[l].astype(bf16), expert_w_up[l].astype(bf16),
                  expert_w_down[l].astype(bf16), row(ln2_w[l]), row(ln2_b[l]))
    return x2.reshape(bsz, seq, D_MODEL)
```

```python
import jax
import jax.numpy as jnp
import numpy as np
from jax import lax
from jax.experimental import pallas as pl
from jax.experimental.pallas import tpu as pltpu

f32 = jnp.float32
bf16 = jnp.bfloat16

D_MODEL = 1024
DEPTH = 4
HEAD_DIM = 64
D_RWKV = 512
RWKV_HEADS = D_RWKV // HEAD_DIM
DECAY_LORA = 64
ICL_LORA = 64
GATE_LORA = 128
RWKV_GN_EPS = 64e-5
D_MOBA = 512
MOBA_HEADS = D_MOBA // HEAD_DIM
MOBA_BLOCK = 256
MOBA_TOPK = 3
D_SSM = 512
SSM_GROUP = 16
SSM_GROUPS = D_SSM // SSM_GROUP
SSM_STATE = 64
N_BRANCHES = 3
N_EXPERT_GROUPS = 4
EXPERTS_PER_GROUP = 8
N_EXPERTS = N_EXPERT_GROUPS * EXPERTS_PER_GROUP
D_EXPERT = D_MODEL // 4
LN_EPS = 1e-5
DEEPNORM_ALPHA = (2 * DEPTH) ** 0.25
NEG_INF = -1e30
RWKV_COLS = 3 * D_RWKV + DECAY_LORA + ICL_LORA + GATE_LORA
MOBA_COLS = 3 * D_MOBA
OFF_MOBA = RWKV_COLS
OFF_SSM = OFF_MOBA + MOBA_COLS
OFF_GATE = OFF_SSM + D_SSM

LANES = 128
MXU_TILE = 256
WKV_CHUNK = 64
WKV_PAIR = 4
WKV_BASE = 4
VMEM_LIMIT = 48 * 1024 * 1024


def _nt(a, b):
    return lax.dot_general(a, b, (((1,), (1,)), ((), ())), preferred_element_type=f32)


def _tn(a, b):
    return lax.dot_general(a, b, (((0,), (0,)), ((), ())), preferred_element_type=f32)


def _mm(a, b):
    return jnp.dot(a, b, preferred_element_type=f32)


def _split2(x):
    hi = x.astype(bf16)
    lo = (x - hi.astype(f32)).astype(bf16)
    return hi, lo


def _layer_norm(y, w, b):
    mu = jnp.mean(y, axis=-1, keepdims=True)
    d = y - mu
    var = jnp.mean(d * d, axis=-1, keepdims=True)
    return d * lax.rsqrt(var + LN_EPS) * w + b


def _proj_kernel(x_ref, w_ref, o_ref):
    o_ref[...] = _mm(x_ref[...].astype(bf16), w_ref[0])


def _w_window(layer, off, width):
    return pl.BlockSpec((pl.Element(1), pl.Element(D_MODEL), pl.Element(width)),
                        lambda *_: (layer, 0, off))


def _project(x2, w_all, layer, off, n, tm=1024):
    m, k = x2.shape
    return pl.pallas_call(
        _proj_kernel,
        grid=(m // tm,),
        in_specs=[pl.BlockSpec((tm, k), lambda i: (i, 0)), _w_window(layer, off, n)],
        out_specs=pl.BlockSpec((tm, n), lambda i: (i, 0)),
        out_shape=jax.ShapeDtypeStruct((m, n), f32),
        compiler_params=pltpu.CompilerParams(
            dimension_semantics=("parallel",), vmem_limit_bytes=VMEM_LIMIT),
    )(x2, w_all)


def _rwkv_kernel(x_ref, wz_ref, mu_ref, w0_ref, w2_ref, a0_ref, a2_ref, g2_ref, kk_ref, ka_ref, rk_ref,
                 lnw_ref, lnb_ref, seg_ref, tri_ref, o_ref,
                 zlast_s, state_s, r_s, k_s, v_s, a_s, b_s, lw_s, g_s, bon_s, y_s,
                 rt_s, bh_s, kh_s, tw_s, arb_s, pc_s):
    tb = x_ref.shape[0]
    n_chunks = tb // WKV_CHUNK
    c64 = WKV_CHUNK

    @pl.when(pl.program_id(1) == 0)
    def _():
        zlast_s[...] = jnp.zeros_like(zlast_s)
        state_s[...] = jnp.zeros_like(state_s)

    seg = seg_ref[...]

    def segsum(x):
        hi, lo = _split2(x)
        w = seg.shape[0]
        return jnp.concatenate([_mm(hi[:, c:c + w], seg) + _mm(lo[:, c:c + w], seg)
                                for c in range(0, D_RWKV, w)], axis=1)

    z = _mm(x_ref[...].astype(bf16), wz_ref[0])
    row = lax.broadcasted_iota(jnp.int32, z.shape, 0)
    z_prev = jnp.where(row == 0, zlast_s[...], pltpu.roll(z, 1, axis=0))
    zlast_s[...] = z[tb - 1:tb, :]
    zz = z + (z_prev - z) * mu_ref[...]
    r = zz[:, 0:D_RWKV]
    k = zz[:, D_RWKV:2 * D_RWKV]
    v = zz[:, 2 * D_RWKV:3 * D_RWKV]
    o1 = 3 * D_RWKV
    xw = zz[:, o1:o1 + DECAY_LORA]
    xa = zz[:, o1 + DECAY_LORA:o1 + DECAY_LORA + ICL_LORA]
    xg = zz[:, o1 + DECAY_LORA + ICL_LORA:]
    wl = w0_ref[...] + _mm(jnp.tanh(xw).astype(bf16), w2_ref[...])
    lw_s[...] = -float(np.exp(-0.5)) * jax.nn.sigmoid(wl)
    a_icl = jax.nn.sigmoid(a0_ref[...] + _mm(xa.astype(bf16), a2_ref[...]))
    g_s[...] = _mm(jax.nn.sigmoid(xg).astype(bf16), g2_ref[...])
    kk = k * kk_ref[...]
    kk = kk * lax.rsqrt(jnp.maximum(segsum(kk * kk), 1e-24))
    k2 = k * (1.0 + (a_icl - 1.0) * ka_ref[...])
    r_s[...] = r
    k_s[...] = k2
    v_s[...] = v
    a_s[...] = -kk
    b_s[...] = kk * a_icl
    bon_s[...] = segsum(r * k2 * rk_ref[...])

    ri = lax.broadcasted_iota(jnp.int32, (c64, c64), 0)
    ci = lax.broadcasted_iota(jnp.int32, (c64, c64), 1)
    strict = ri > ci
    incl = ri >= ci
    eye = jnp.where(ri == ci, 1.0, 0.0).astype(f32)
    tri = tri_ref[...]
    blk_mask = [strict & (ri // WKV_BASE == ci // WKV_BASE)]
    width = WKV_BASE
    while width < c64:
        blk_mask.append((ri // (2 * width) == ci // (2 * width)) & (ri % (2 * width) >= width)
                        & (ci % (2 * width) < width))
        width *= 2

    heads = range(RWKV_HEADS)
    sls = [slice(h * HEAD_DIM, (h + 1) * HEAD_DIM) for h in heads]

    def chunk_terms(c):
        rows = pl.ds(pl.multiple_of(c * c64, c64), c64)
        lwc = lw_s[rows, :]
        lhi, llo = _split2(lwc)
        cl = _mm(tri, lhi) + _mm(tri, llo)
        cl_last = cl[c64 - 1:c64, :]
        e_neg = jnp.exp(-cl)
        e_tot = jnp.exp(cl_last - cl)
        ac = a_s[rows, :]
        bc = b_s[rows, :]
        kc = k_s[rows, :]
        r_t = r_s[rows, :] * jnp.exp(cl)
        rt_s[rows, :] = r_t
        bh_s[rows, :] = bc * e_tot
        kh_s[rows, :] = kc * e_tot
        pc_s[pl.ds(c, 1), :] = jnp.exp(cl_last)
        return rows, ac * jnp.exp(cl - lwc), r_t, bc * e_neg, kc * e_neg, v_s[rows, :]

    def intra_body(c2, carry):
        terms = [chunk_terms(c2 * WKV_PAIR + j) for j in range(WKV_PAIR)]
        probs = [(j, h) for j in range(WKV_PAIR) for h in heads]
        cat = lambda j, p, q, h: jnp.concatenate([terms[j][p][:, sls[h]], terms[j][q][:, sls[h]]],
                                                 axis=0).astype(bf16)
        gm = [_nt(cat(j, 1, 2, h), cat(j, 3, 4, h)) for j, h in probs]
        a_ab = [jnp.where(strict, g[:c64, :c64], 0.0) for g in gm]
        a_ak = [jnp.where(strict, g[:c64, c64:], 0.0).astype(bf16) for g in gm]
        a_rb = [jnp.where(incl, g[c64:, :c64], 0.0).astype(bf16) for g in gm]
        a_rk = [jnp.where(incl, g[c64:, c64:], 0.0).astype(bf16) for g in gm]
        vb = [terms[j][5][:, sls[h]].astype(bf16) for j, h in probs]
        akv = [_mm(a, v) for a, v in zip(a_ak, vb)]
        yloc = [_mm(a, v) for a, v in zip(a_rk, vb)]
        base = [jnp.where(blk_mask[0], a, 0.0) for a in a_ab]
        bb = [x.astype(bf16) for x in base]
        sq = [_mm(x, x) for x in bb]
        t_inv = [eye + b + s + _mm(b.astype(bf16), s.astype(bf16)) for b, s in zip(base, sq)]
        for lvl in range(1, len(blk_mask)):
            tb16 = [t.astype(bf16) for t in t_inv]
            low = [_mm(jnp.where(blk_mask[lvl], a, 0.0).astype(bf16), t) for a, t in zip(a_ab, tb16)]
            t_inv = [t + _mm(tb, lo.astype(bf16)) for t, tb, lo in zip(t_inv, tb16, low)]
        rhs = [jnp.concatenate([terms[j][1][:, sls[h]], akv[n]], axis=1).astype(bf16)
               for n, (j, h) in enumerate(probs)]
        tw = [_mm(t.astype(bf16), r) for t, r in zip(t_inv, rhs)]
        for j in range(WKV_PAIR):
            mine = slice(j * RWKV_HEADS, (j + 1) * RWKV_HEADS)
            rows = terms[j][0]
            tw_s[rows, :] = jnp.concatenate(tw[mine], axis=1)
            arb_s[rows, :] = jnp.concatenate(a_rb[mine], axis=1)
            y_s[rows, :] = jnp.concatenate(yloc[mine], axis=1)
        return carry

    lax.fori_loop(0, n_chunks // WKV_PAIR, intra_body, 0)

    def state_body(c, carry):
        rows = pl.ds(pl.multiple_of(c * c64, c64), c64)
        twc = tw_s[rows, :]
        arbc = arb_s[rows, :]
        r_t = rt_s[rows, :]
        b_h = bh_s[rows, :]
        k_h = kh_s[rows, :]
        vc = v_s[rows, :]
        yl = y_s[rows, :]
        p_c = pc_s[pl.ds(c, 1), :]
        s_old = [state_s[h] for h in heads]
        wr = [jnp.concatenate([twc[:, h * LANES:h * LANES + c64], r_t[:, sls[h]]], axis=0).astype(bf16)
              for h in heads]
        wrs = [_nt(wr[h], s_old[h].astype(bf16)) for h in heads]
        u = [wrs[h][:c64] + twc[:, h * LANES + c64:(h + 1) * LANES] for h in heads]
        ys = [wrs[h][c64:] + _mm(arbc[:, sls[h]], u[h].astype(bf16)) + yl[:, sls[h]] for h in heads]
        uv = [jnp.concatenate([u[h], vc[:, sls[h]]], axis=0).astype(bf16) for h in heads]
        bkh = [jnp.concatenate([b_h[:, sls[h]], k_h[:, sls[h]]], axis=0).astype(bf16) for h in heads]
        for h in heads:
            state_s[h] = s_old[h] * p_c[:, sls[h]] + _tn(uv[h], bkh[h])
        y_s[rows, :] = jnp.concatenate(ys, axis=1)
        return carry

    lax.fori_loop(0, n_chunks, state_body, 0)

    y = y_s[...]
    mean = segsum(y) * (1.0 / HEAD_DIM)
    d = y - mean
    var = segsum(d * d) * (1.0 / HEAD_DIM)
    yn = d * lax.rsqrt(var + RWKV_GN_EPS) * lnw_ref[...] + lnb_ref[...]
    o_ref[0] = (yn + bon_s[...] * v_s[...]) * g_s[...]


def _rwkv(x2, w_all, layer, p, bsz, seq, tb=512):
    tb = min(tb, seq)
    n_l = seq // tb
    seg = (np.arange(MXU_TILE)[:, None] // HEAD_DIM == np.arange(MXU_TILE)[None, :] // HEAD_DIM)
    seg = jnp.asarray(seg, bf16)
    tri = jnp.asarray(np.tril(np.ones((WKV_CHUNK, WKV_CHUNK))), bf16)
    row = lambda a: a.reshape(1, -1).astype(f32)
    args = [x2, w_all, row(p['mu']), row(p['w0']), p['w2'].astype(bf16), row(p['a0']), p['a2'].astype(bf16),
            p['g2'].astype(bf16), row(p['k_k']), row(p['k_a']), row(p['r_k']), row(p['ln_w']),
            row(p['ln_b']), seg, tri]
    full = lambda a: pl.BlockSpec(a.shape, lambda b, l: (0,) * a.ndim)
    in_specs = ([pl.BlockSpec((tb, D_MODEL), lambda b, l: (b * n_l + l, 0)), _w_window(layer, 0, RWKV_COLS)]
                + [full(a) for a in args[2:]])
    big = lambda: pltpu.VMEM((tb, D_RWKV), f32)
    return pl.pallas_call(
        _rwkv_kernel,
        grid=(bsz, n_l),
        in_specs=in_specs,
        out_specs=pl.BlockSpec((1, tb, D_RWKV), lambda b, l: (b, l, 0)),
        out_shape=jax.ShapeDtypeStruct((bsz, seq, D_RWKV), f32),
        scratch_shapes=[pltpu.VMEM((1, RWKV_COLS), f32),
                        pltpu.VMEM((RWKV_HEADS, HEAD_DIM, HEAD_DIM), f32),
                        big(), big(), big(), big(), big(), big(), big(), big(), big(),
                        big(), big(), big(), pltpu.VMEM((tb, RWKV_HEADS * LANES), f32),
                        pltpu.VMEM((tb, D_RWKV), bf16), pltpu.VMEM((tb // WKV_CHUNK, D_RWKV), f32)],
        compiler_params=pltpu.CompilerParams(
            dimension_semantics=("parallel", "arbitrary"), vmem_limit_bytes=VMEM_LIMIT),
    )(*args)


MOBA_AUG_POS = 0
MOBA_AUG_BLK = 6
LOG2E = 1.4426950408889634
MOBA_STEP_HEADS = 4


def _moba_kernel(q_ref, k_ref, v_ref, tmpl_ref, o_ref, kaug_s, vaug_s, kmean_s):
    blk = MOBA_BLOCK
    half = HEAD_DIM
    seq = k_ref.shape[1]
    n_blk = seq // blk
    qi = pl.program_id(2)
    heads = range(MOBA_STEP_HEADS)
    tile = [slice((hh // 2) * LANES, (hh // 2 + 1) * LANES) for hh in heads]
    aug0 = [half if hh % 2 == 0 else 0 for hh in heads]

    @pl.when(qi == 0)
    def _():
        lane_l = lax.broadcasted_iota(jnp.int32, (seq, LANES), 1)
        for hh in heads:
            kb = k_ref[0, :, tile[hh]].astype(bf16)
            vb = v_ref[0, :, tile[hh]].astype(bf16)
            in_head = (lane_l < half) if hh % 2 == 0 else (lane_l >= half)
            kaug_s[hh] = jnp.where(in_head, kb, tmpl_ref[0, hh])
            ones_lane = jnp.where(lane_l == aug0[hh], 1.0, 0.0).astype(bf16)
            vaug_s[hh] = jnp.where(in_head, vb, ones_lane)
        for n in range(n_blk):
            km = jnp.mean(k_ref[0, n * blk:(n + 1) * blk, :], axis=0, keepdims=True)
            for tt in range(MOBA_STEP_HEADS // 2):
                kmean_s[tt, n:n + 1, :] = km[:, tt * LANES:(tt + 1) * LANES]

    ri = lax.broadcasted_iota(jnp.int32, (blk, blk), 0)
    ci = lax.broadcasted_iota(jnp.int32, (blk, blk), 1)
    causal = ri >= ci
    lane = lax.broadcasted_iota(jnp.int32, (blk, LANES), 1)
    qf = [q_ref[0, :, tile[hh]] * (HEAD_DIM ** -0.5 * LOG2E) for hh in heads]
    in_head = [(lane < half) if hh % 2 == 0 else (lane >= half) for hh in heads]
    blk_lane = [lane - (aug0[hh] + MOBA_AUG_BLK) for hh in heads]

    qs = [_split2(jnp.where(in_head[hh], qf[hh], 0.0)) for hh in heads]
    kms = [_split2(kmean_s[tt]) for tt in range(MOBA_STEP_HEADS // 2)]
    gate = [_nt(kms[hh // 2][0], qs[hh][0]) + _nt(kms[hh // 2][1], qs[hh][0])
            + _nt(kms[hh // 2][0], qs[hh][1]) for hh in heads]
    is_pos = [(lane >= aug0[hh] + MOBA_AUG_POS) & (lane < aug0[hh] + MOBA_AUG_POS + 3) for hh in heads]
    own = [jnp.where(is_pos[hh], 1.0, 0.0) for hh in heads]
    q_own = [jnp.where(in_head[hh], qf[hh], own[hh]).astype(bf16) for hh in heads]

    start = pl.multiple_of(qi * blk, blk)
    s0 = [_nt(q_own[hh], kaug_s[hh, pl.ds(start, blk), :]) for hh in heads]
    s0 = [jnp.where(causal, s, NEG_INF) for s in s0]
    m0 = [jnp.max(s, axis=-1, keepdims=True) for s in s0]
    p0 = [jnp.exp2(s - m).astype(bf16) for s, m in zip(s0, m0)]
    a0 = [_mm(p0[hh], vaug_s[hh, pl.ds(start, blk), :]) for hh in heads]

    brow = lax.broadcasted_iota(jnp.int32, (n_blk, blk), 0)
    brow_f = brow.astype(f32)
    g = [jnp.where(brow < qi, x, -jnp.inf) for x in gate]
    picked = [jnp.zeros((n_blk, blk), f32) for _ in heads]
    for _ in range(MOBA_TOPK):
        mx = [jnp.max(x, axis=0, keepdims=True) for x in g]
        idx = [jnp.min(jnp.where(g[hh] == mx[hh], brow_f, 1e9), axis=0, keepdims=True) for hh in heads]
        hit = [brow_f == i for i in idx]
        picked = [jnp.where(hit[hh], 1.0, picked[hh]) for hh in heads]
        g = [jnp.where(hit[hh], -jnp.inf, g[hh]) for hh in heads]
    sel_t = [jnp.where(brow < qi, p, 0.0).astype(bf16) for p in picked]
    lane_b = lax.broadcasted_iota(jnp.int32, (n_blk, LANES), 1)
    brow_b = lax.broadcasted_iota(jnp.int32, (n_blk, LANES), 0)
    q_aug = []
    for hh in heads:
        place = jnp.where(lane_b == brow_b + (aug0[hh] + MOBA_AUG_BLK), 1.0, 0.0).astype(bf16)
        sel = _tn(sel_t[hh], place) > 0.5
        is_blk = (blk_lane[hh] >= 0) & (blk_lane[hh] < n_blk)
        aug = jnp.where(is_blk, jnp.where(sel, 0.0, NEG_INF), own[hh])
        q_aug.append(jnp.where(in_head[hh], qf[hh], aug).astype(bf16))

    def make_body(width, base):
        def body(t, carry):
            ms, accs = carry
            rows = pl.ds(pl.multiple_of(base + t * (width * blk), 2 * blk), width * blk)
            s = [_nt(q_aug[hh], kaug_s[hh, rows, :]) for hh in heads]
            m_new = [jnp.maximum(ms[hh], jnp.max(s[hh], axis=-1, keepdims=True)) for hh in heads]
            p = [jnp.exp2(s[hh] - m_new[hh]).astype(bf16) for hh in heads]
            pv = [_mm(p[hh], vaug_s[hh, rows, :]) for hh in heads]
            accs = [accs[hh] * jnp.exp2(ms[hh] - m_new[hh]) + pv[hh] for hh in heads]
            return (tuple(m_new), tuple(accs))
        return body

    n_pairs = (qi + 1) // 2
    n_quads = n_pairs // 2
    carry = lax.fori_loop(0, n_quads, make_body(4, 0), (tuple(m0), tuple(a0)))
    _, accs = lax.fori_loop(0, n_pairs - 2 * n_quads, make_body(2, n_quads * (4 * blk)), carry)
    out = [accs[hh] / accs[hh][:, aug0[hh]:aug0[hh] + 1] for hh in heads]
    for tt in range(MOBA_STEP_HEADS // 2):
        o_ref[0, :, tt * LANES:(tt + 1) * LANES] = jnp.where(lane < half, out[2 * tt], out[2 * tt + 1])


def _moba_template(seq):
    pos = np.arange(seq, dtype=np.float64)
    tmpl = np.zeros((MOBA_HEADS, seq, LANES), np.float32)
    for h in range(MOBA_HEADS):
        a0 = HEAD_DIM if h % 2 == 0 else 0
        slope = 2.0 ** (-8.0 * (h + 1) / MOBA_HEADS)
        rest = slope * LOG2E * pos
        for c in range(3):
            part = rest.astype(np.float32).astype(bf16).astype(np.float64)
            tmpl[h, :, a0 + MOBA_AUG_POS + c] = part
            rest = rest - part
        tmpl[h, np.arange(seq), a0 + MOBA_AUG_BLK + np.arange(seq) // MOBA_BLOCK] = 1.0
    return jnp.asarray(tmpl.reshape(MOBA_HEADS // MOBA_STEP_HEADS, MOBA_STEP_HEADS, seq, LANES), bf16)


def _moba(qkv3):
    bsz, seq, _ = qkv3.shape
    blk = MOBA_BLOCK
    sh = MOBA_STEP_HEADS
    n_grp = MOBA_HEADS // sh
    wid = sh * HEAD_DIM
    assert seq % (2 * blk) == 0 and seq // blk <= HEAD_DIM - MOBA_AUG_BLK
    return pl.pallas_call(
        _moba_kernel,
        grid=(bsz, n_grp, seq // blk),
        in_specs=[pl.BlockSpec((1, blk, wid), lambda b, p, i: (b, i, p)),
                  pl.BlockSpec((1, seq, wid), lambda b, p, i: (b, 0, n_grp + p)),
                  pl.BlockSpec((1, seq, wid), lambda b, p, i: (b, 0, 2 * n_grp + p)),
                  pl.BlockSpec((1, sh, seq, LANES), lambda b, p, i: (p, 0, 0, 0))],
        out_specs=pl.BlockSpec((1, blk, wid), lambda b, p, i: (b, i, p)),
        out_shape=jax.ShapeDtypeStruct((bsz, seq, D_MOBA), f32),
        scratch_shapes=[pltpu.VMEM((sh, seq, LANES), bf16), pltpu.VMEM((sh, seq, LANES), bf16),
                        pltpu.VMEM((sh // 2, seq // blk, LANES), f32)],
        compiler_params=pltpu.CompilerParams(
            dimension_semantics=("parallel", "parallel", "arbitrary"), vmem_limit_bytes=VMEM_LIMIT),
    )(qkv3, qkv3, qkv3, _moba_template(seq))


S5_SUPER = 4
S5_SEGS = 8
S5_COLS = 512


def _s5_kernel(x_ref, wu_ref, perm_ref, bd_ref, cre_ref, cim_ref, lam_ref, d_ref, gw_ref, gb_ref, o_ref,
               hr_s, hi_s, carry_s):
    tb = x_ref.shape[0]
    seg = tb // S5_SEGS
    n_state = lam_ref.shape[1]
    sgw = n_state // S5_SUPER
    uw = D_SSM // S5_SUPER

    @pl.when(pl.program_id(1) == 0)
    def _():
        carry_s[...] = jnp.zeros_like(carry_s)

    u = _mm(x_ref[...].astype(bf16), wu_ref[0])
    perm = perm_ref[...]
    ub = _mm(perm, u.astype(bf16)).astype(bf16)
    for sg in range(S5_SUPER):
        bu = _mm(ub[:, sg * uw:(sg + 1) * uw], bd_ref[sg])
        for j in range(sgw // LANES):
            hr_s[sg * (sgw // LANES) + j] = bu[:, j * LANES:(j + 1) * LANES]
            hi_s[sg * (sgw // LANES) + j] = bu[:, sgw + j * LANES:sgw + (j + 1) * LANES]

    tpc = S5_COLS // LANES

    def load(ref, cc, idx):
        return jnp.concatenate([ref[cc * tpc + j, idx, :] for j in range(tpc)], axis=1)

    def store(ref, cc, idx, val):
        for j in range(tpc):
            ref[cc * tpc + j, idx, :] = val[:, j * LANES:(j + 1) * LANES]

    for cc in range(n_state // S5_COLS):
        cols = slice(cc * S5_COLS, (cc + 1) * S5_COLS)
        lr = lam_ref[0:1, cols]
        li = lam_ref[1:2, cols]

        def local(r, carry, cc=cc, lr=lr, li=li):
            hr, hi = carry
            idx = pl.ds(pl.multiple_of(r * S5_SEGS, S5_SEGS), S5_SEGS)
            nr = lr * hr - li * hi + load(hr_s, cc, idx)
            ni = lr * hi + li * hr + load(hi_s, cc, idx)
            store(hr_s, cc, idx, nr)
            store(hi_s, cc, idx, ni)
            return nr, ni

        zero = jnp.zeros((S5_SEGS, S5_COLS), f32)
        er, ei = lax.fori_loop(0, seg, local, (zero, zero), unroll=4)
        pr, pi = lr, li
        for _ in range(seg.bit_length() - 1):
            pr, pi = pr * pr - pi * pi, 2.0 * pr * pi
        cr = carry_s[0:1, cols]
        ci = carry_s[1:2, cols]
        in_r, in_i = [], []
        for s in range(S5_SEGS):
            in_r.append(cr)
            in_i.append(ci)
            cr, ci = pr * cr - pi * ci + er[s:s + 1, :], pr * ci + pi * cr + ei[s:s + 1, :]
        carry_s[0:1, cols] = cr
        carry_s[1:2, cols] = ci

        def fix(r, carry, cc=cc, lr=lr, li=li):
            qr, qi = carry
            qr, qi = lr * qr - li * qi, lr * qi + li * qr
            idx = pl.ds(pl.multiple_of(r * S5_SEGS, S5_SEGS), S5_SEGS)
            store(hr_s, cc, idx, load(hr_s, cc, idx) + qr)
            store(hi_s, cc, idx, load(hi_s, cc, idx) + qi)
            return qr, qi

        lax.fori_loop(0, seg, fix, (jnp.concatenate(in_r, axis=0), jnp.concatenate(in_i, axis=0)),
                      unroll=4)

    slab = lambda ref, sg: jnp.concatenate(
        [ref[sg * (sgw // LANES) + j] for j in range(sgw // LANES)], axis=1).astype(bf16)
    ys = [_mm(slab(hr_s, sg), cre_ref[sg]) + _mm(slab(hi_s, sg), cim_ref[sg]) for sg in range(S5_SUPER)]
    yh, yl = _split2(jnp.concatenate(ys, axis=1))
    y = _tn(perm, yh) + _tn(perm, yl) + d_ref[...] * u
    y = 0.5 * y * (1.0 + jnp.tanh(0.7978845608028654 * (y + 0.044715 * (y * y * y))))
    zg = _mm(y.astype(bf16), gw_ref[...]) + gb_ref[...]
    o_ref[0] = zg[:, :D_SSM] * jax.nn.sigmoid(zg[:, D_SSM:])


def _s5_params(p):
    a_re = p['a_re'].astype(f32)
    a_im = p['a_im'].astype(f32)
    dt = jnp.exp(p['log_dt'].astype(f32))[:, None]
    mag = jnp.exp(a_re * dt)
    lam_re = mag * jnp.cos(a_im * dt)
    lam_im = mag * jnp.sin(a_im * dt)
    den = a_re * a_re + a_im * a_im
    nr = lam_re - 1.0
    coef_re = (nr * a_re + lam_im * a_im) / den
    coef_im = (lam_im * a_re - nr * a_im) / den
    b_re = p['b_re'].astype(f32)
    b_im = p['b_im'].astype(f32)
    bb_re = coef_re[..., None] * b_re - coef_im[..., None] * b_im
    bb_im = coef_re[..., None] * b_im + coef_im[..., None] * b_re
    gps = SSM_GROUPS // S5_SUPER
    eye = jnp.eye(gps, dtype=f32)
    n_state = SSM_GROUPS * SSM_STATE
    sgw = n_state // S5_SUPER
    uw = D_SSM // S5_SUPER

    def in_map(bb):
        bb = bb.reshape(S5_SUPER, gps, SSM_STATE, SSM_GROUP)
        return jnp.einsum('sgph,gk->sghkp', bb, eye).reshape(S5_SUPER, uw, sgw)

    def out_map(c):
        c = c.reshape(S5_SUPER, gps, SSM_GROUP, SSM_STATE)
        return jnp.einsum('sghp,gk->sgpkh', c, eye).reshape(S5_SUPER, sgw, uw)

    bd = jnp.concatenate([in_map(bb_re), in_map(bb_im)], axis=2)
    cre = out_map(p['c_re'].astype(f32))
    cim = -out_map(p['c_im'].astype(f32))
    lam = jnp.stack([lam_re.reshape(n_state), lam_im.reshape(n_state)], axis=0)
    return bd.astype(bf16), cre.astype(bf16), cim.astype(bf16), lam


def _s5(x2, w_all, layer, p, bsz, seq, tb=512):
    tb = min(tb, seq)
    n_l = seq // tb
    seg = tb // S5_SEGS
    assert tb % S5_SEGS == 0 and seg & (seg - 1) == 0
    n_state = SSM_GROUPS * SSM_STATE
    bd, cre, cim, lam = _s5_params(p)
    src = (np.arange(tb) % S5_SEGS) * seg + np.arange(tb) // S5_SEGS
    perm = jnp.asarray(np.arange(tb)[None, :] == src[:, None], bf16)
    args = [x2, w_all, perm, bd, cre, cim, lam, p['d'].reshape(1, -1).astype(f32), p['glu_w'].astype(bf16),
            p['glu_b'].reshape(1, -1).astype(f32)]
    full = lambda a: pl.BlockSpec(a.shape, lambda b, l: (0,) * a.ndim)
    return pl.pallas_call(
        _s5_kernel,
        grid=(bsz, n_l),
        in_specs=[pl.BlockSpec((tb, D_MODEL), lambda b, l: (b * n_l + l, 0)),
                  _w_window(layer, OFF_SSM, D_SSM)] + [full(a) for a in args[2:]],
        out_specs=pl.BlockSpec((1, tb, D_SSM), lambda b, l: (b, l, 0)),
        out_shape=jax.ShapeDtypeStruct((bsz, seq, D_SSM), f32),
        scratch_shapes=[pltpu.VMEM((n_state // LANES, tb, LANES), f32),
                        pltpu.VMEM((n_state // LANES, tb, LANES), f32),
                        pltpu.VMEM((2, n_state), f32)],
        compiler_params=pltpu.CompilerParams(
            dimension_semantics=("parallel", "arbitrary"), vmem_limit_bytes=VMEM_LIMIT),
    )(*args)


def _merge_kernel(x_ref, ya_ref, yb_ref, yc_ref, wg_ref, gb_ref, wa_ref, wb_ref, wc_ref, wo_ref,
                  lnw_ref, lnb_ref, o_ref):
    x = x_ref[...]
    xb = x.astype(bf16)
    merged = None
    for br, (y_ref, w_ref) in enumerate(((ya_ref, wa_ref), (yb_ref, wb_ref), (yc_ref, wc_ref))):
        cols = slice(br * D_MODEL, (br + 1) * D_MODEL)
        gate = jax.nn.sigmoid(_mm(xb, wg_ref[0, :, cols]) + gb_ref[:, cols])
        term = gate * _mm(y_ref[...].astype(bf16), w_ref[...])
        merged = term if merged is None else merged + term
    h = _mm(merged.astype(bf16), wo_ref[...])
    o_ref[...] = _layer_norm(DEEPNORM_ALPHA * x + h, lnw_ref[...], lnb_ref[...])


def _merge(x2, ya, yb, yc, w_all, layer, gb, wa, wb, wc, wo, lnw, lnb, tm=512):
    m = x2.shape[0]
    rowblk = lambda n: pl.BlockSpec((tm, n), lambda i: (i, 0))
    full = lambda a: pl.BlockSpec(a.shape, lambda i: (0,) * a.ndim)
    consts = [gb, wa, wb, wc, wo, lnw, lnb]
    return pl.pallas_call(
        _merge_kernel,
        grid=(m // tm,),
        in_specs=[rowblk(D_MODEL), rowblk(D_RWKV), rowblk(D_MOBA), rowblk(D_SSM),
                  _w_window(layer, OFF_GATE, N_BRANCHES * D_MODEL)] + [full(a) for a in consts],
        out_specs=rowblk(D_MODEL),
        out_shape=jax.ShapeDtypeStruct((m, D_MODEL), f32),
        compiler_params=pltpu.CompilerParams(
            dimension_semantics=("parallel",), vmem_limit_bytes=VMEM_LIMIT),
    )(x2, ya, yb, yc, w_all, *consts)


MOE_STEP_EXPERTS = 4
MOE_CHUNK = 256
MOE_STEPS = N_EXPERTS // MOE_STEP_EXPERTS
STEPS_PER_GROUP = EXPERTS_PER_GROUP // MOE_STEP_EXPERTS


def _moe_kernel(x_ref, rwh_ref, rwl_ref, rb_ref, wg_ref, wu_ref, wd_ref, lnw_ref, lnb_ref, o_ref,
                xs_s, outs_s, wts_s, pos_s, posrow_s, meta_s):
    s = pl.program_id(1)
    tm = x_ref.shape[0]
    ck = MOE_CHUNK
    lane = lax.broadcasted_iota(jnp.int32, (tm, LANES), 1)
    lane_f = lane.astype(f32)

    @pl.when(s == 0)
    def _():
        xh, xl = _split2(x_ref[...])
        rwh = rwh_ref[...]
        logits = _mm(xh, rwh) + _mm(xh, rwl_ref[...]) + _mm(xl, rwh) + rb_ref[...]
        is_grp = (lane >= N_EXPERTS) & (lane < N_EXPERTS + N_EXPERT_GROUPS)
        gl = jnp.where(is_grp, logits, -jnp.inf)
        gmax = jnp.max(gl, axis=-1, keepdims=True)
        gidx = jnp.min(jnp.where(gl == gmax, lane_f, 1e9), axis=-1, keepdims=True) - float(N_EXPERTS)
        p_group = 1.0 / jnp.sum(jnp.where(is_grp, jnp.exp(gl - gmax), 0.0), axis=-1, keepdims=True)
        grp_of_lane = jnp.floor(lane_f * (1.0 / EXPERTS_PER_GROUP))
        in_grp = (lane < N_EXPERTS) & (grp_of_lane == gidx)
        el = jnp.where(in_grp, logits, -jnp.inf)
        m1 = jnp.max(el, axis=-1, keepdims=True)
        i1 = jnp.min(jnp.where(el == m1, lane_f, 1e9), axis=-1, keepdims=True)
        el2 = jnp.where(lane_f == i1, -jnp.inf, el)
        m2 = jnp.max(el2, axis=-1, keepdims=True)
        i2 = jnp.min(jnp.where(el2 == m2, lane_f, 1e9), axis=-1, keepdims=True)
        e2 = jnp.exp(m2 - m1)
        w1 = p_group / (1.0 + e2)
        w2 = p_group * e2 / (1.0 + e2)
        wt = jnp.where(lane_f == i1, w1, 0.0) + jnp.where(lane_f == i2, w2, 0.0)

        ohg = lane_f == gidx
        ri = lax.broadcasted_iota(jnp.int32, (tm, tm), 0)
        ci = lax.broadcasted_iota(jnp.int32, (tm, tm), 1)
        tri = jnp.where(ri >= ci, 1.0, 0.0).astype(bf16)
        cg = _mm(tri, jnp.where(ohg, 1.0, 0.0).astype(bf16))
        cnt = cg[tm - 1:tm, :]
        padded = jnp.ceil(cnt * (1.0 / ck)) * ck
        ui = lax.broadcasted_iota(jnp.int32, (LANES, LANES), 0)
        uj = lax.broadcasted_iota(jnp.int32, (LANES, LANES), 1)
        upper = jnp.where(ui < uj, 1.0, 0.0).astype(bf16)
        off = _mm(jnp.broadcast_to(padded, (8, LANES)).astype(bf16), upper)[0:1, :]
        rank = jnp.sum(jnp.where(ohg, cg - 1.0, 0.0), axis=-1, keepdims=True)
        pos = jnp.sum(jnp.where(ohg, off, 0.0), axis=-1, keepdims=True) + rank
        pos_s[...] = jnp.broadcast_to(pos, (tm, LANES))
        pa = jnp.floor(pos * (1.0 / 64.0))
        pb = pos - 64.0 * pa
        cols = jnp.where(lane == 0, pa, jnp.where(lane == 1, pb, 0.0)).astype(bf16)
        r8 = lax.broadcasted_iota(jnp.int32, (8, LANES), 0)
        l8 = lax.broadcasted_iota(jnp.int32, (8, LANES), 1)
        sel8 = jnp.where((r8 == 0) & (l8 == 0), 64.0, jnp.where((r8 == 0) & (l8 == 1), 1.0, 0.0))
        posrow_s[...] = _nt(sel8.astype(bf16), cols)
        off_i = off.astype(jnp.int32)
        nck_i = (padded * (1.0 / ck)).astype(jnp.int32)
        total = 0
        for g in range(N_EXPERT_GROUPS):
            meta_s[g] = off_i[0, g]
            meta_s[N_EXPERT_GROUPS + g] = nck_i[0, g]
            total = total + nck_i[0, g]
        meta_s[2 * N_EXPERT_GROUPS] = total

        wth, wtl = _split2(wt)
        rows_f = lax.broadcasted_iota(jnp.int32, (ck, tm), 0).astype(f32)
        posrow = posrow_s[0:1, :]

        def fill(c, carry):
            r0 = pl.multiple_of(c * ck, ck)
            pc = jnp.where(rows_f + r0.astype(f32) == posrow, 1.0, 0.0).astype(bf16)
            xs_s[pl.ds(r0, ck), :] = _mm(pc, xh).astype(bf16)
            wts_s[pl.ds(r0, ck), :] = _mm(pc, wth) + _mm(pc, wtl)
            outs_s[pl.ds(r0, ck), :] = jnp.zeros((ck, D_MODEL), f32)
            return carry

        lax.fori_loop(0, total, fill, 0)

    grp = s // STEPS_PER_GROUP
    off_g = meta_s[grp]
    lane_c = lax.broadcasted_iota(jnp.int32, (ck, LANES), 1)

    def chunk(c, carry):
        rows = pl.ds(pl.multiple_of(off_g + c * ck, ck), ck)
        xs = xs_s[rows, :]
        wts = wts_s[rows, :]
        gact = [_mm(xs, wg_ref[j]) for j in range(MOE_STEP_EXPERTS)]
        up = [_mm(xs, wu_ref[j]) for j in range(MOE_STEP_EXPERTS)]
        part = None
        for j in range(MOE_STEP_EXPERTS):
            wcol = jnp.sum(jnp.where(lane_c == s * MOE_STEP_EXPERTS + j, wts, 0.0), axis=-1, keepdims=True)
            hid = gact[j] * jax.nn.sigmoid(gact[j]) * up[j] * wcol
            out = _mm(hid.astype(bf16), wd_ref[j])
            part = out if part is None else part + out
        outs_s[rows, :] += part
        return carry

    lax.fori_loop(0, meta_s[N_EXPERT_GROUPS + grp], chunk, 0)

    @pl.when(s == MOE_STEPS - 1)
    def _():
        o_ref[...] = DEEPNORM_ALPHA * x_ref[...]
        pos_col = pos_s[:, 0:1]
        cols_f = lax.broadcasted_iota(jnp.int32, (tm, ck), 1).astype(f32)

        def unsort(c, carry):
            r0 = pl.multiple_of(c * ck, ck)
            pt = jnp.where(cols_f + r0.astype(f32) == pos_col, 1.0, 0.0).astype(bf16)
            o_ref[...] += _mm(pt, outs_s[pl.ds(r0, ck), :].astype(bf16))
            return carry

        lax.fori_loop(0, meta_s[2 * N_EXPERT_GROUPS], unsort, 0)
        o_ref[...] = _layer_norm(o_ref[...], lnw_ref[...], lnb_ref[...])


def _moe(x2, rw, rb, wg, wu, wd, lnw, lnb, tm=1024):
    m = x2.shape[0]
    tm = min(tm, m)
    rwh, rwl = _split2(rw)
    n_rows = tm + N_EXPERT_GROUPS * MOE_CHUNK
    full = lambda a: pl.BlockSpec(a.shape, lambda i, e: (0,) * a.ndim)
    return pl.pallas_call(
        _moe_kernel,
        grid=(m // tm, MOE_STEPS),
        in_specs=[pl.BlockSpec((tm, D_MODEL), lambda i, e: (i, 0)),
                  full(rwh), full(rwl), full(rb),
                  pl.BlockSpec((MOE_STEP_EXPERTS, D_MODEL, D_EXPERT), lambda i, e: (e, 0, 0)),
                  pl.BlockSpec((MOE_STEP_EXPERTS, D_MODEL, D_EXPERT), lambda i, e: (e, 0, 0)),
                  pl.BlockSpec((MOE_STEP_EXPERTS, D_EXPERT, D_MODEL), lambda i, e: (e, 0, 0)),
                  full(lnw), full(lnb)],
        out_specs=pl.BlockSpec((tm, D_MODEL), lambda i, e: (i, 0)),
        out_shape=jax.ShapeDtypeStruct((m, D_MODEL), f32),
        scratch_shapes=[pltpu.VMEM((n_rows, D_MODEL), bf16), pltpu.VMEM((n_rows, D_MODEL), f32),
                        pltpu.VMEM((n_rows, LANES), f32), pltpu.VMEM((tm, LANES), f32),
                        pltpu.VMEM((8, tm), f32), pltpu.SMEM((2 * N_EXPERT_GROUPS + 1,), jnp.int32)],
        compiler_params=pltpu.CompilerParams(
            dimension_semantics=("parallel", "arbitrary"), vmem_limit_bytes=VMEM_LIMIT),
    )(x2, rwh, rwl, rb, wg, wu, wd, lnw, lnb)


def _router_weights(router_group_w, router_group_b, router_expert_w, router_expert_b):
    pad = LANES - N_EXPERTS - N_EXPERT_GROUPS
    rw = jnp.concatenate([router_expert_w.astype(f32), router_group_w.astype(f32),
                          jnp.zeros((D_MODEL, pad), f32)], axis=1)
    rb = jnp.concatenate([router_expert_b.astype(f32), router_group_b.astype(f32),
                          jnp.zeros((pad,), f32)]).reshape(1, LANES)
    return rw, rb


def kernel(x, w_in, rwkv_mu, rwkv_w0, rwkv_w2, rwkv_a0, rwkv_a2, rwkv_g2, rwkv_k_k, rwkv_k_a, rwkv_r_k, rwkv_ln_w, rwkv_ln_b, ssm_a_re, ssm_a_im, ssm_b_re, ssm_b_im, ssm_c_re, ssm_c_im, ssm_d, ssm_log_dt, ssm_glu_w, ssm_glu_b, w_up_rwkv, w_up_moba, w_up_ssm, gate_b, w_out, ln1_w, ln1_b, router_group_w, router_group_b, router_expert_w, router_expert_b, expert_w_gate, expert_w_up, expert_w_down, ln2_w, ln2_b):
    bsz, seq, _ = x.shape
    x2 = x.reshape(bsz * seq, D_MODEL).astype(f32)
    row = lambda a: a.reshape(1, -1).astype(f32)
    w_all = w_in.astype(bf16)
    for l in range(DEPTH):
        qkv = _project(x2, w_all, l, OFF_MOBA, MOBA_COLS)
        y_a = _rwkv(x2, w_all, l,
                    dict(mu=rwkv_mu[l], w0=rwkv_w0[l], w2=rwkv_w2[l], a0=rwkv_a0[l], a2=rwkv_a2[l],
                         g2=rwkv_g2[l], k_k=rwkv_k_k[l], k_a=rwkv_k_a[l], r_k=rwkv_r_k[l],
                         ln_w=rwkv_ln_w[l], ln_b=rwkv_ln_b[l]), bsz, seq)
        y_b = _moba(qkv.reshape(bsz, seq, MOBA_COLS))
        y_c = _s5(x2, w_all, l,
                  dict(a_re=ssm_a_re[l], a_im=ssm_a_im[l], b_re=ssm_b_re[l], b_im=ssm_b_im[l],
                       c_re=ssm_c_re[l], c_im=ssm_c_im[l], d=ssm_d[l], log_dt=ssm_log_dt[l],
                       glu_w=ssm_glu_w[l], glu_b=ssm_glu_b[l]), bsz, seq)
        x2 = _merge(x2, y_a.reshape(-1, D_RWKV), y_b.reshape(-1, D_MOBA), y_c.reshape(-1, D_SSM),
                    w_all, l, row(gate_b[l]),
                    w_up_rwkv[l].astype(bf16), w_up_moba[l].astype(bf16), w_up_ssm[l].astype(bf16),
                    w_out[l].astype(bf16), row(ln1_w[l]), row(ln1_b[l]))
        rw, rb = _router_weights(router_group_w[l], router_group_b[l], router_expert_w[l],
                                 router_expert_b[l])
        x2 = _moe(x2, rw, rb, expert_w_gate[l].astype(bf16), expert_w_up[l].astype(bf16),
                  expert_w_down[l].astype(bf16), row(ln2_w[l]), row(ln2_b[l]))
    return x2.reshape(bsz, seq, D_MODEL)
```

```python
import jax
import jax.numpy as jnp
import numpy as np
from jax import lax
from jax.experimental import pallas as pl
from jax.experimental.pallas import tpu as pltpu

f32 = jnp.float32
bf16 = jnp.bfloat16

D_MODEL = 1024
DEPTH = 4
HEAD_DIM = 64
D_RWKV = 512
RWKV_HEADS = D_RWKV // HEAD_DIM
DECAY_LORA = 64
ICL_LORA = 64
GATE_LORA = 128
RWKV_GN_EPS = 64e-5
D_MOBA = 512
MOBA_HEADS = D_MOBA // HEAD_DIM
MOBA_BLOCK = 256
MOBA_TOPK = 3
D_SSM = 512
SSM_GROUP = 16
SSM_GROUPS = D_SSM // SSM_GROUP
SSM_STATE = 64
N_BRANCHES = 3
N_EXPERT_GROUPS = 4
EXPERTS_PER_GROUP = 8
N_EXPERTS = N_EXPERT_GROUPS * EXPERTS_PER_GROUP
D_EXPERT = D_MODEL // 4
LN_EPS = 1e-5
DEEPNORM_ALPHA = (2 * DEPTH) ** 0.25
NEG_INF = -1e30
RWKV_COLS = 3 * D_RWKV + DECAY_LORA + ICL_LORA + GATE_LORA
MOBA_COLS = 3 * D_MOBA
OFF_MOBA = RWKV_COLS
OFF_SSM = OFF_MOBA + MOBA_COLS
OFF_GATE = OFF_SSM + D_SSM

LANES = 128
MXU_TILE = 256
WKV_CHUNK = 64
WKV_PAIR = 4
WKV_BASE = 4
VMEM_LIMIT = 48 * 1024 * 1024


def _nt(a, b):
    return lax.dot_general(a, b, (((1,), (1,)), ((), ())), preferred_element_type=f32)


def _tn(a, b):
    return lax.dot_general(a, b, (((0,), (0,)), ((), ())), preferred_element_type=f32)


def _mm(a, b):
    return jnp.dot(a, b, preferred_element_type=f32)


def _split2(x):
    hi = x.astype(bf16)
    lo = (x - hi.astype(f32)).astype(bf16)
    return hi, lo


def _layer_norm(y, w, b):
    mu = jnp.mean(y, axis=-1, keepdims=True)
    d = y - mu
    var = jnp.mean(d * d, axis=-1, keepdims=True)
    return d * lax.rsqrt(var + LN_EPS) * w + b


def _proj_kernel(x_ref, w_ref, o_ref):
    o_ref[...] = _mm(x_ref[...].astype(bf16), w_ref[0])


def _w_window(layer, off, width):
    return pl.BlockSpec((pl.Element(1), pl.Element(D_MODEL), pl.Element(width)),
                        lambda *_: (layer, 0, off))


def _project(x2, w_all, layer, off, n, tm=1024):
    m, k = x2.shape
    return pl.pallas_call(
        _proj_kernel,
        grid=(m // tm,),
        in_specs=[pl.BlockSpec((tm, k), lambda i: (i, 0)), _w_window(layer, off, n)],
        out_specs=pl.BlockSpec((tm, n), lambda i: (i, 0)),
        out_shape=jax.ShapeDtypeStruct((m, n), f32),
        compiler_params=pltpu.CompilerParams(
            dimension_semantics=("parallel",), vmem_limit_bytes=VMEM_LIMIT),
    )(x2, w_all)


def _rwkv_kernel(x_ref, wz_ref, mu_ref, w0_ref, w2_ref, a0_ref, a2_ref, g2_ref, kk_ref, ka_ref, rk_ref,
                 lnw_ref, lnb_ref, seg_ref, tri_ref, o_ref,
                 zlast_s, state_s, r_s, k_s, v_s, a_s, b_s, lw_s, g_s, bon_s, y_s,
                 rt_s, bh_s, kh_s, tw_s, arb_s, pc_s):
    tb = x_ref.shape[0]
    n_chunks = tb // WKV_CHUNK
    c64 = WKV_CHUNK

    @pl.when(pl.program_id(1) == 0)
    def _():
        zlast_s[...] = jnp.zeros_like(zlast_s)
        state_s[...] = jnp.zeros_like(state_s)

    seg = seg_ref[...]

    def segsum(x):
        hi, lo = _split2(x)
        w = seg.shape[0]
        return jnp.concatenate([_mm(hi[:, c:c + w], seg) + _mm(lo[:, c:c + w], seg)
                                for c in range(0, D_RWKV, w)], axis=1)

    z = _mm(x_ref[...].astype(bf16), wz_ref[0])
    row = lax.broadcasted_iota(jnp.int32, z.shape, 0)
    z_prev = jnp.where(row == 0, zlast_s[...], pltpu.roll(z, 1, axis=0))
    zlast_s[...] = z[tb - 1:tb, :]
    zz = z + (z_prev - z) * mu_ref[...]
    r = zz[:, 0:D_RWKV]
    k = zz[:, D_RWKV:2 * D_RWKV]
    v = zz[:, 2 * D_RWKV:3 * D_RWKV]
    o1 = 3 * D_RWKV
    xw = zz[:, o1:o1 + DECAY_LORA]
    xa = zz[:, o1 + DECAY_LORA:o1 + DECAY_LORA + ICL_LORA]
    xg = zz[:, o1 + DECAY_LORA + ICL_LORA:]
    wl = w0_ref[...] + _mm(jnp.tanh(xw).astype(bf16), w2_ref[...])
    lw_s[...] = -float(np.exp(-0.5)) * jax.nn.sigmoid(wl)
    a_icl = jax.nn.sigmoid(a0_ref[...] + _mm(xa.astype(bf16), a2_ref[...]))
    g_s[...] = _mm(jax.nn.sigmoid(xg).astype(bf16), g2_ref[...])
    kk = k * kk_ref[...]
    kk = kk * lax.rsqrt(jnp.maximum(segsum(kk * kk), 1e-24))
    k2 = k * (1.0 + (a_icl - 1.0) * ka_ref[...])
    r_s[...] = r
    k_s[...] = k2
    v_s[...] = v
    a_s[...] = -kk
    b_s[...] = kk * a_icl
    bon_s[...] = segsum(r * k2 * rk_ref[...])

    ri = lax.broadcasted_iota(jnp.int32, (c64, c64), 0)
    ci = lax.broadcasted_iota(jnp.int32, (c64, c64), 1)
    strict = ri > ci
    incl = ri >= ci
    eye = jnp.where(ri == ci, 1.0, 0.0).astype(f32)
    tri = tri_ref[...]
    blk_mask = [strict & (ri // WKV_BASE == ci // WKV_BASE)]
    width = WKV_BASE
    while width < c64:
        blk_mask.append((ri // (2 * width) == ci // (2 * width)) & (ri % (2 * width) >= width)
                        & (ci % (2 * width) < width))
        width *= 2

    heads = range(RWKV_HEADS)
    sls = [slice(h * HEAD_DIM, (h + 1) * HEAD_DIM) for h in heads]

    def chunk_terms(c):
        rows = pl.ds(pl.multiple_of(c * c64, c64), c64)
        lwc = lw_s[rows, :]
        lhi, llo = _split2(lwc)
        cl = _mm(tri, lhi) + _mm(tri, llo)
        cl_last = cl[c64 - 1:c64, :]
        e_neg = jnp.exp(-cl)
        e_tot = jnp.exp(cl_last - cl)
        ac = a_s[rows, :]
        bc = b_s[rows, :]
        kc = k_s[rows, :]
        r_t = r_s[rows, :] * jnp.exp(cl)
        rt_s[rows, :] = r_t
        bh_s[rows, :] = bc * e_tot
        kh_s[rows, :] = kc * e_tot
        pc_s[pl.ds(c, 1), :] = jnp.exp(cl_last)
        return rows, ac * jnp.exp(cl - lwc), r_t, bc * e_neg, kc * e_neg, v_s[rows, :]

    def intra_body(c2, carry):
        terms = [chunk_terms(c2 * WKV_PAIR + j) for j in range(WKV_PAIR)]
        probs = [(j, h) for j in range(WKV_PAIR) for h in heads]
        cat = lambda j, p, q, h: jnp.concatenate([terms[j][p][:, sls[h]], terms[j][q][:, sls[h]]],
                                                 axis=0).astype(bf16)
        gm = [_nt(cat(j, 1, 2, h), cat(j, 3, 4, h)) for j, h in probs]
        a_ab = [jnp.where(strict, g[:c64, :c64], 0.0) for g in gm]
        a_ak = [jnp.where(strict, g[:c64, c64:], 0.0).astype(bf16) for g in gm]
        a_rb = [jnp.where(incl, g[c64:, :c64], 0.0).astype(bf16) for g in gm]
        a_rk = [jnp.where(incl, g[c64:, c64:], 0.0).astype(bf16) for g in gm]
        vb = [terms[j][5][:, sls[h]].astype(bf16) for j, h in probs]
        akv = [_mm(a, v) for a, v in zip(a_ak, vb)]
        yloc = [_mm(a, v) for a, v in zip(a_rk, vb)]
        base = [jnp.where(blk_mask[0], a, 0.0) for a in a_ab]
        bb = [x.astype(bf16) for x in base]
        sq = [_mm(x, x) for x in bb]
        t_inv = [eye + b + s + _mm(b.astype(bf16), s.astype(bf16)) for b, s in zip(base, sq)]
        for lvl in range(1, len(blk_mask)):
            tb16 = [t.astype(bf16) for t in t_inv]
            low = [_mm(jnp.where(blk_mask[lvl], a, 0.0).astype(bf16), t) for a, t in zip(a_ab, tb16)]
            t_inv = [t + _mm(tb, lo.astype(bf16)) for t, tb, lo in zip(t_inv, tb16, low)]
        rhs = [jnp.concatenate([terms[j][1][:, sls[h]], akv[n]], axis=1).astype(bf16)
               for n, (j, h) in enumerate(probs)]
        tw = [_mm(t.astype(bf16), r) for t, r in zip(t_inv, rhs)]
        for j in range(WKV_PAIR):
            mine = slice(j * RWKV_HEADS, (j + 1) * RWKV_HEADS)
            rows = terms[j][0]
            tw_s[rows, :] = jnp.concatenate(tw[mine], axis=1)
            arb_s[rows, :] = jnp.concatenate(a_rb[mine], axis=1)
            y_s[rows, :] = jnp.concatenate(yloc[mine], axis=1)
        return carry

    lax.fori_loop(0, n_chunks // WKV_PAIR, intra_body, 0)

    def state_body(c, carry):
        rows = pl.ds(pl.multiple_of(c * c64, c64), c64)
        twc = tw_s[rows, :]
        arbc = arb_s[rows, :]
        r_t = rt_s[rows, :]
        b_h = bh_s[rows, :]
        k_h = kh_s[rows, :]
        vc = v_s[rows, :]
        yl = y_s[rows, :]
        p_c = pc_s[pl.ds(c, 1), :]
        s_old = [state_s[h] for h in heads]
        wr = [jnp.concatenate([twc[:, h * LANES:h * LANES + c64], r_t[:, sls[h]]], axis=0).astype(bf16)
              for h in heads]
        wrs = [_nt(wr[h], s_old[h].astype(bf16)) for h in heads]
        u = [wrs[h][:c64] + twc[:, h * LANES + c64:(h + 1) * LANES] for h in heads]
        ys = [wrs[h][c64:] + _mm(arbc[:, sls[h]], u[h].astype(bf16)) + yl[:, sls[h]] for h in heads]
        uv = [jnp.concatenate([u[h], vc[:, sls[h]]], axis=0).astype(bf16) for h in heads]
        bkh = [jnp.concatenate([b_h[:, sls[h]], k_h[:, sls[h]]], axis=0).astype(bf16) for h in heads]
        for h in heads:
            state_s[h] = s_old[h] * p_c[:, sls[h]] + _tn(uv[h], bkh[h])
        y_s[rows, :] = jnp.concatenate(ys, axis=1)
        return carry

    lax.fori_loop(0, n_chunks, state_body, 0)

    y = y_s[...]
    mean = segsum(y) * (1.0 / HEAD_DIM)
    d = y - mean
    var = segsum(d * d) * (1.0 / HEAD_DIM)
    yn = d * lax.rsqrt(var + RWKV_GN_EPS) * lnw_ref[...] + lnb_ref[...]
    o_ref[0] = (yn + bon_s[...] * v_s[...]) * g_s[...]


def _rwkv(x2, w_all, layer, p, bsz, seq, tb=512):
    tb = min(tb, seq)
    n_l = seq // tb
    seg = (np.arange(MXU_TILE)[:, None] // HEAD_DIM == np.arange(MXU_TILE)[None, :] // HEAD_DIM)
    seg = jnp.asarray(seg, bf16)
    tri = jnp.asarray(np.tril(np.ones((WKV_CHUNK, WKV_CHUNK))), bf16)
    row = lambda a: a.reshape(1, -1).astype(f32)
    args = [x2, w_all, row(p['mu']), row(p['w0']), p['w2'].astype(bf16), row(p['a0']), p['a2'].astype(bf16),
            p['g2'].astype(bf16), row(p['k_k']), row(p['k_a']), row(p['r_k']), row(p['ln_w']),
            row(p['ln_b']), seg, tri]
    full = lambda a: pl.BlockSpec(a.shape, lambda b, l: (0,) * a.ndim)
    in_specs = ([pl.BlockSpec((tb, D_MODEL), lambda b, l: (b * n_l + l, 0)), _w_window(layer, 0, RWKV_COLS)]
                + [full(a) for a in args[2:]])
    big = lambda: pltpu.VMEM((tb, D_RWKV), f32)
    return pl.pallas_call(
        _rwkv_kernel,
        grid=(bsz, n_l),
        in_specs=in_specs,
        out_specs=pl.BlockSpec((1, tb, D_RWKV), lambda b, l: (b, l, 0)),
        out_shape=jax.ShapeDtypeStruct((bsz, seq, D_RWKV), f32),
        scratch_shapes=[pltpu.VMEM((1, RWKV_COLS), f32),
                        pltpu.VMEM((RWKV_HEADS, HEAD_DIM, HEAD_DIM), f32),
                        big(), big(), big(), big(), big(), big(), big(), big(), big(),
                        big(), big(), big(), pltpu.VMEM((tb, RWKV_HEADS * LANES), f32),
                        pltpu.VMEM((tb, D_RWKV), bf16), pltpu.VMEM((tb // WKV_CHUNK, D_RWKV), f32)],
        compiler_params=pltpu.CompilerParams(
            dimension_semantics=("parallel", "arbitrary"), vmem_limit_bytes=VMEM_LIMIT),
    )(*args)


MOBA_AUG_POS = 0
MOBA_AUG_BLK = 6
LOG2E = 1.4426950408889634
MOBA_STEP_HEADS = 4


def _moba_kernel(q_ref, k_ref, v_ref, tmpl_ref, o_ref, kaug_s, vaug_s, kmean_s):
    blk = MOBA_BLOCK
    half = HEAD_DIM
    seq = k_ref.shape[1]
    n_blk = seq // blk
    qi = pl.program_id(2)
    heads = range(MOBA_STEP_HEADS)
    tile = [slice((hh // 2) * LANES, (hh // 2 + 1) * LANES) for hh in heads]
    aug0 = [half if hh % 2 == 0 else 0 for hh in heads]

    @pl.when(qi == 0)
    def _():
        lane_l = lax.broadcasted_iota(jnp.int32, (seq, LANES), 1)
        for hh in heads:
            kb = k_ref[0, :, tile[hh]].astype(bf16)
            vb = v_ref[0, :, tile[hh]].astype(bf16)
            in_head = (lane_l < half) if hh % 2 == 0 else (lane_l >= half)
            kaug_s[hh] = jnp.where(in_head, kb, tmpl_ref[0, hh])
            ones_lane = jnp.where(lane_l == aug0[hh], 1.0, 0.0).astype(bf16)
            vaug_s[hh] = jnp.where(in_head, vb, ones_lane)
        for n in range(n_blk):
            km = jnp.mean(k_ref[0, n * blk:(n + 1) * blk, :], axis=0, keepdims=True)
            for tt in range(MOBA_STEP_HEADS // 2):
                kmean_s[tt, n:n + 1, :] = km[:, tt * LANES:(tt + 1) * LANES]

    ri = lax.broadcasted_iota(jnp.int32, (blk, blk), 0)
    ci = lax.broadcasted_iota(jnp.int32, (blk, blk), 1)
    causal = ri >= ci
    lane = lax.broadcasted_iota(jnp.int32, (blk, LANES), 1)
    qf = [q_ref[0, :, tile[hh]] * (HEAD_DIM ** -0.5 * LOG2E) for hh in heads]
    in_head = [(lane < half) if hh % 2 == 0 else (lane >= half) for hh in heads]
    blk_lane = [lane - (aug0[hh] + MOBA_AUG_BLK) for hh in heads]

    qs = [_split2(jnp.where(in_head[hh], qf[hh], 0.0)) for hh in heads]
    kms = [_split2(kmean_s[tt]) for tt in range(MOBA_STEP_HEADS // 2)]
    gate = [_nt(kms[hh // 2][0], qs[hh][0]) + _nt(kms[hh // 2][1], qs[hh][0])
            + _nt(kms[hh // 2][0], qs[hh][1]) for hh in heads]
    is_pos = [(lane >= aug0[hh] + MOBA_AUG_POS) & (lane < aug0[hh] + MOBA_AUG_POS + 3) for hh in heads]
    own = [jnp.where(is_pos[hh], 1.0, 0.0) for hh in heads]
    q_own = [jnp.where(in_head[hh], qf[hh], own[hh]).astype(bf16) for hh in heads]

    start = pl.multiple_of(qi * blk, blk)
    s0 = [_nt(q_own[hh], kaug_s[hh, pl.ds(start, blk), :]) for hh in heads]
    s0 = [jnp.where(causal, s, NEG_INF) for s in s0]
    m0 = [jnp.max(s, axis=-1, keepdims=True) for s in s0]
    p0 = [jnp.exp2(s - m).astype(bf16) for s, m in zip(s0, m0)]
    a0 = [_mm(p0[hh], vaug_s[hh, pl.ds(start, blk), :]) for hh in heads]

    brow = lax.broadcasted_iota(jnp.int32, (n_blk, blk), 0)
    brow_f = brow.astype(f32)
    g = [jnp.where(brow < qi, x, -jnp.inf) for x in gate]
    picked = [jnp.zeros((n_blk, blk), f32) for _ in heads]
    for _ in range(MOBA_TOPK):
        mx = [jnp.max(x, axis=0, keepdims=True) for x in g]
        idx = [jnp.min(jnp.where(g[hh] == mx[hh], brow_f, 1e9), axis=0, keepdims=True) for hh in heads]
        hit = [brow_f == i for i in idx]
        picked = [jnp.where(hit[hh], 1.0, picked[hh]) for hh in heads]
        g = [jnp.where(hit[hh], -jnp.inf, g[hh]) for hh in heads]
    sel_t = [jnp.where(brow < qi, p, 0.0).astype(bf16) for p in picked]
    lane_b = lax.broadcasted_iota(jnp.int32, (n_blk, LANES), 1)
    brow_b = lax.broadcasted_iota(jnp.int32, (n_blk, LANES), 0)
    q_aug = []
    for hh in heads:
        place = jnp.where(lane_b == brow_b + (aug0[hh] + MOBA_AUG_BLK), 1.0, 0.0).astype(bf16)
        sel = _tn(sel_t[hh], place) > 0.5
        is_blk = (blk_lane[hh] >= 0) & (blk_lane[hh] < n_blk)
        aug = jnp.where(is_blk, jnp.where(sel, 0.0, NEG_INF), own[hh])
        q_aug.append(jnp.where(in_head[hh], qf[hh], aug).astype(bf16))

    def make_body(width, base):
        def body(t, carry):
            ms, accs = carry
            rows = pl.ds(pl.multiple_of(base + t * (width * blk), 2 * blk), width * blk)
            s = [_nt(q_aug[hh], kaug_s[hh, rows, :]) for hh in heads]
            m_new = [jnp.maximum(ms[hh], jnp.max(s[hh], axis=-1, keepdims=True)) for hh in heads]
            p = [jnp.exp2(s[hh] - m_new[hh]).astype(bf16) for hh in heads]
            pv = [_mm(p[hh], vaug_s[hh, rows, :]) for hh in heads]
            accs = [accs[hh] * jnp.exp2(ms[hh] - m_new[hh]) + pv[hh] for hh in heads]
            return (tuple(m_new), tuple(accs))
        return body

    n_pairs = (qi + 1) // 2
    n_quads = n_pairs // 2
    carry = lax.fori_loop(0, n_quads, make_body(4, 0), (tuple(m0), tuple(a0)))
    _, accs = lax.fori_loop(0, n_pairs - 2 * n_quads, make_body(2, n_quads * (4 * blk)), carry)
    out = [accs[hh] / accs[hh][:, aug0[hh]:aug0[hh] + 1] for hh in heads]
    for tt in range(MOBA_STEP_HEADS // 2):
        o_ref[0, :, tt * LANES:(tt + 1) * LANES] = jnp.where(lane < half, out[2 * tt], out[2 * tt + 1])


def _moba_template(seq):
    pos = np.arange(seq, dtype=np.float64)
    tmpl = np.zeros((MOBA_HEADS, seq, LANES), np.float32)
    for h in range(MOBA_HEADS):
        a0 = HEAD_DIM if h % 2 == 0 else 0
        slope = 2.0 ** (-8.0 * (h + 1) / MOBA_HEADS)
        rest = slope * LOG2E * pos
        for c in range(3):
            part = rest.astype(np.float32).astype(bf16).astype(np.float64)
            tmpl[h, :, a0 + MOBA_AUG_POS + c] = part
            rest = rest - part
        tmpl[h, np.arange(seq), a0 + MOBA_AUG_BLK + np.arange(seq) // MOBA_BLOCK] = 1.0
    return jnp.asarray(tmpl.reshape(MOBA_HEADS // MOBA_STEP_HEADS, MOBA_STEP_HEADS, seq, LANES), bf16)


def _moba(qkv3):
    bsz, seq, _ = qkv3.shape
    blk = MOBA_BLOCK
    sh = MOBA_STEP_HEADS
    n_grp = MOBA_HEADS // sh
    wid = sh * HEAD_DIM
    assert seq % (2 * blk) == 0 and seq // blk <= HEAD_DIM - MOBA_AUG_BLK
    return pl.pallas_call(
        _moba_kernel,
        grid=(bsz, n_grp, seq // blk),
        in_specs=[pl.BlockSpec((1, blk, wid), lambda b, p, i: (b, i, p)),
                  pl.BlockSpec((1, seq, wid), lambda b, p, i: (b, 0, n_grp + p)),
                  pl.BlockSpec((1, seq, wid), lambda b, p, i: (b, 0, 2 * n_grp + p)),
                  pl.BlockSpec((1, sh, seq, LANES), lambda b, p, i: (p, 0, 0, 0))],
        out_specs=pl.BlockSpec((1, blk, wid), lambda b, p, i: (b, i, p)),
        out_shape=jax.ShapeDtypeStruct((bsz, seq, D_MOBA), f32),
        scratch_shapes=[pltpu.VMEM((sh, seq, LANES), bf16), pltpu.VMEM((sh, seq, LANES), bf16),
                        pltpu.VMEM((sh // 2, seq // blk, LANES), f32)],
        compiler_params=pltpu.CompilerParams(
            dimension_semantics=("parallel", "parallel", "arbitrary"), vmem_limit_bytes=VMEM_LIMIT),
    )(qkv3, qkv3, qkv3, _moba_template(seq))


S5_SUPER = 4
S5_SEGS = 8
S5_COLS = 512


def _s5_kernel(x_ref, wu_ref, perm_ref, unperm_ref, bd_ref, cre_ref, cim_ref, lam_ref, d_ref, gw_ref, gb_ref, o_ref,
               hr_s, hi_s, carry_s):
    tb = x_ref.shape[0]
    seg = tb // S5_SEGS
    n_state = lam_ref.shape[1]
    sgw = n_state // S5_SUPER
    uw = D_SSM // S5_SUPER

    @pl.when(pl.program_id(1) == 0)
    def _():
        carry_s[...] = jnp.zeros_like(carry_s)

    u = _mm(x_ref[...].astype(bf16), wu_ref[0])
    perm = perm_ref[...]
    ub = _mm(perm, u.astype(bf16)).astype(bf16)
    for sg in range(S5_SUPER):
        bu = _mm(ub[:, sg * uw:(sg + 1) * uw], bd_ref[sg])
        for j in range(sgw // LANES):
            hr_s[sg * (sgw // LANES) + j] = bu[:, j * LANES:(j + 1) * LANES]
            hi_s[sg * (sgw // LANES) + j] = bu[:, sgw + j * LANES:sgw + (j + 1) * LANES]

    tpc = S5_COLS // LANES

    def load(ref, cc, idx):
        return jnp.concatenate([ref[cc * tpc + j, idx, :] for j in range(tpc)], axis=1)

    def store(ref, cc, idx, val):
        for j in range(tpc):
            ref[cc * tpc + j, idx, :] = val[:, j * LANES:(j + 1) * LANES]

    for cc in range(n_state // S5_COLS):
        cols = slice(cc * S5_COLS, (cc + 1) * S5_COLS)
        lr = lam_ref[0:1, cols]
        li = lam_ref[1:2, cols]

        def local(r, carry, cc=cc, lr=lr, li=li):
            hr, hi = carry
            idx = pl.ds(pl.multiple_of(r * S5_SEGS, S5_SEGS), S5_SEGS)
            nr = lr * hr - li * hi + load(hr_s, cc, idx)
            ni = lr * hi + li * hr + load(hi_s, cc, idx)
            store(hr_s, cc, idx, nr)
            store(hi_s, cc, idx, ni)
            return nr, ni

        zero = jnp.zeros((S5_SEGS, S5_COLS), f32)
        er, ei = lax.fori_loop(0, seg, local, (zero, zero), unroll=4)
        pr, pi = lr, li
        for _ in range(seg.bit_length() - 1):
            pr, pi = pr * pr - pi * pi, 2.0 * pr * pi
        cr = carry_s[0:1, cols]
        ci = carry_s[1:2, cols]
        in_r, in_i = [], []
        for s in range(S5_SEGS):
            in_r.append(cr)
            in_i.append(ci)
            cr, ci = pr * cr - pi * ci + er[s:s + 1, :], pr * ci + pi * cr + ei[s:s + 1, :]
        carry_s[0:1, cols] = cr
        carry_s[1:2, cols] = ci

        def fix(r, carry, cc=cc, lr=lr, li=li):
            qr, qi = carry
            qr, qi = lr * qr - li * qi, lr * qi + li * qr
            idx = pl.ds(pl.multiple_of(r * S5_SEGS, S5_SEGS), S5_SEGS)
            store(hr_s, cc, idx, load(hr_s, cc, idx) + qr)
            store(hi_s, cc, idx, load(hi_s, cc, idx) + qi)
            return qr, qi

        lax.fori_loop(0, seg, fix, (jnp.concatenate(in_r, axis=0), jnp.concatenate(in_i, axis=0)),
                      unroll=4)

    slab = lambda ref, sg: jnp.concatenate(
        [ref[sg * (sgw // LANES) + j] for j in range(sgw // LANES)], axis=1).astype(bf16)
    ys = [_mm(slab(hr_s, sg), cre_ref[sg]) + _mm(slab(hi_s, sg), cim_ref[sg]) for sg in range(S5_SUPER)]
    yh, yl = _split2(jnp.concatenate(ys, axis=1))
    unperm = unperm_ref[...]
    y = _mm(unperm, yh) + _mm(unperm, yl) + d_ref[...] * u
    y = 0.5 * y * (1.0 + jnp.tanh(0.7978845608028654 * (y + 0.044715 * (y * y * y))))
    zg = _mm(y.astype(bf16), gw_ref[...]) + gb_ref[...]
    o_ref[0] = zg[:, :D_SSM] * jax.nn.sigmoid(zg[:, D_SSM:])


def _s5_params(p):
    a_re = p['a_re'].astype(f32)
    a_im = p['a_im'].astype(f32)
    dt = jnp.exp(p['log_dt'].astype(f32))[:, None]
    mag = jnp.exp(a_re * dt)
    lam_re = mag * jnp.cos(a_im * dt)
    lam_im = mag * jnp.sin(a_im * dt)
    den = a_re * a_re + a_im * a_im
    nr = lam_re - 1.0
    coef_re = (nr * a_re + lam_im * a_im) / den
    coef_im = (lam_im * a_re - nr * a_im) / den
    b_re = p['b_re'].astype(f32)
    b_im = p['b_im'].astype(f32)
    bb_re = coef_re[..., None] * b_re - coef_im[..., None] * b_im
    bb_im = coef_re[..., None] * b_im + coef_im[..., None] * b_re
    gps = SSM_GROUPS // S5_SUPER
    eye = jnp.eye(gps, dtype=f32)
    n_state = SSM_GROUPS * SSM_STATE
    sgw = n_state // S5_SUPER
    uw = D_SSM // S5_SUPER

    def in_map(bb):
        bb = bb.reshape(S5_SUPER, gps, SSM_STATE, SSM_GROUP)
        return jnp.einsum('sgph,gk->sghkp', bb, eye).reshape(S5_SUPER, uw, sgw)

    def out_map(c):
        c = c.reshape(S5_SUPER, gps, SSM_GROUP, SSM_STATE)
        return jnp.einsum('sghp,gk->sgpkh', c, eye).reshape(S5_SUPER, sgw, uw)

    bd = jnp.concatenate([in_map(bb_re), in_map(bb_im)], axis=2)
    cre = out_map(p['c_re'].astype(f32))
    cim = -out_map(p['c_im'].astype(f32))
    lam = jnp.stack([lam_re.reshape(n_state), lam_im.reshape(n_state)], axis=0)
    return bd.astype(bf16), cre.astype(bf16), cim.astype(bf16), lam


def _s5(x2, w_all, layer, p, bsz, seq, tb=512):
    tb = min(tb, seq)
    n_l = seq // tb
    seg = tb // S5_SEGS
    assert tb % S5_SEGS == 0 and seg & (seg - 1) == 0
    n_state = SSM_GROUPS * SSM_STATE
    bd, cre, cim, lam = _s5_params(p)
    src = (np.arange(tb) % S5_SEGS) * seg + np.arange(tb) // S5_SEGS
    perm_np = np.arange(tb)[None, :] == src[:, None]
    perm = jnp.asarray(perm_np, bf16)
    unperm = jnp.asarray(perm_np.T, bf16)
    args = [x2, w_all, perm, unperm, bd, cre, cim, lam, p['d'].reshape(1, -1).astype(f32), p['glu_w'].astype(bf16),
            p['glu_b'].reshape(1, -1).astype(f32)]
    full = lambda a: pl.BlockSpec(a.shape, lambda b, l: (0,) * a.ndim)
    return pl.pallas_call(
        _s5_kernel,
        grid=(bsz, n_l),
        in_specs=[pl.BlockSpec((tb, D_MODEL), lambda b, l: (b * n_l + l, 0)),
                  _w_window(layer, OFF_SSM, D_SSM)] + [full(a) for a in args[2:]],
        out_specs=pl.BlockSpec((1, tb, D_SSM), lambda b, l: (b, l, 0)),
        out_shape=jax.ShapeDtypeStruct((bsz, seq, D_SSM), f32),
        scratch_shapes=[pltpu.VMEM((n_state // LANES, tb, LANES), f32),
                        pltpu.VMEM((n_state // LANES, tb, LANES), f32),
                        pltpu.VMEM((2, n_state), f32)],
        compiler_params=pltpu.CompilerParams(
            dimension_semantics=("parallel", "arbitrary"), vmem_limit_bytes=VMEM_LIMIT),
    )(*args)


def _merge_kernel(x_ref, ya_ref, yb_ref, yc_ref, wg_ref, gb_ref, wa_ref, wb_ref, wc_ref, wo_ref,
                  lnw_ref, lnb_ref, o_ref):
    x = x_ref[...]
    xb = x.astype(bf16)
    merged = None
    for br, (y_ref, w_ref) in enumerate(((ya_ref, wa_ref), (yb_ref, wb_ref), (yc_ref, wc_ref))):
        cols = slice(br * D_MODEL, (br + 1) * D_MODEL)
        gate = jax.nn.sigmoid(_mm(xb, wg_ref[0, :, cols]) + gb_ref[:, cols])
        term = gate * _mm(y_ref[...].astype(bf16), w_ref[...])
        merged = term if merged is None else merged + term
    h = _mm(merged.astype(bf16), wo_ref[...])
    o_ref[...] = _layer_norm(DEEPNORM_ALPHA * x + h, lnw_ref[...], lnb_ref[...])


def _merge(x2, ya, yb, yc, w_all, layer, gb, wa, wb, wc, wo, lnw, lnb, tm=512):
    m = x2.shape[0]
    rowblk = lambda n: pl.BlockSpec((tm, n), lambda i: (i, 0))
    full = lambda a: pl.BlockSpec(a.shape, lambda i: (0,) * a.ndim)
    consts = [gb, wa, wb, wc, wo, lnw, lnb]
    return pl.pallas_call(
        _merge_kernel,
        grid=(m // tm,),
        in_specs=[rowblk(D_MODEL), rowblk(D_RWKV), rowblk(D_MOBA), rowblk(D_SSM),
                  _w_window(layer, OFF_GATE, N_BRANCHES * D_MODEL)] + [full(a) for a in consts],
        out_specs=rowblk(D_MODEL),
        out_shape=jax.ShapeDtypeStruct((m, D_MODEL), f32),
        compiler_params=pltpu.CompilerParams(
            dimension_semantics=("parallel",), vmem_limit_bytes=VMEM_LIMIT),
    )(x2, ya, yb, yc, w_all, *consts)


MOE_STEP_EXPERTS = 4
MOE_CHUNK = 256
MOE_STEPS = N_EXPERTS // MOE_STEP_EXPERTS
STEPS_PER_GROUP = EXPERTS_PER_GROUP // MOE_STEP_EXPERTS


def _moe_kernel(x_ref, rwh_ref, rwl_ref, rb_ref, wg_ref, wu_ref, wd_ref, lnw_ref, lnb_ref, o_ref,
                xs_s, outs_s, wts_s, pos_s, posrow_s, meta_s):
    s = pl.program_id(1)
    tm = x_ref.shape[0]
    ck = MOE_CHUNK
    lane = lax.broadcasted_iota(jnp.int32, (tm, LANES), 1)
    lane_f = lane.astype(f32)

    @pl.when(s == 0)
    def _():
        xh, xl = _split2(x_ref[...])
        rwh = rwh_ref[...]
        logits = _mm(xh, rwh) + _mm(xh, rwl_ref[...]) + _mm(xl, rwh) + rb_ref[...]
        is_grp = (lane >= N_EXPERTS) & (lane < N_EXPERTS + N_EXPERT_GROUPS)
        gl = jnp.where(is_grp, logits, -jnp.inf)
        gmax = jnp.max(gl, axis=-1, keepdims=True)
        gidx = jnp.min(jnp.where(gl == gmax, lane_f, 1e9), axis=-1, keepdims=True) - float(N_EXPERTS)
        p_group = 1.0 / jnp.sum(jnp.where(is_grp, jnp.exp(gl - gmax), 0.0), axis=-1, keepdims=True)
        grp_of_lane = jnp.floor(lane_f * (1.0 / EXPERTS_PER_GROUP))
        in_grp = (lane < N_EXPERTS) & (grp_of_lane == gidx)
        el = jnp.where(in_grp, logits, -jnp.inf)
        m1 = jnp.max(el, axis=-1, keepdims=True)
        i1 = jnp.min(jnp.where(el == m1, lane_f, 1e9), axis=-1, keepdims=True)
        el2 = jnp.where(lane_f == i1, -jnp.inf, el)
        m2 = jnp.max(el2, axis=-1, keepdims=True)
        i2 = jnp.min(jnp.where(el2 == m2, lane_f, 1e9), axis=-1, keepdims=True)
        e2 = jnp.exp(m2 - m1)
        w1 = p_group / (1.0 + e2)
        w2 = p_group * e2 / (1.0 + e2)
        wt = jnp.where(lane_f == i1, w1, 0.0) + jnp.where(lane_f == i2, w2, 0.0)

        ohg = lane_f == gidx
        ri = lax.broadcasted_iota(jnp.int32, (tm, tm), 0)
        ci = lax.broadcasted_iota(jnp.int32, (tm, tm), 1)
        tri = jnp.where(ri >= ci, 1.0, 0.0).astype(bf16)
        cg = _mm(tri, jnp.where(ohg, 1.0, 0.0).astype(bf16))
        cnt = cg[tm - 1:tm, :]
        padded = jnp.ceil(cnt * (1.0 / ck)) * ck
        ui = lax.broadcasted_iota(jnp.int32, (LANES, LANES), 0)
        uj = lax.broadcasted_iota(jnp.int32, (LANES, LANES), 1)
        upper = jnp.where(ui < uj, 1.0, 0.0).astype(bf16)
        off = _mm(jnp.broadcast_to(padded, (8, LANES)).astype(bf16), upper)[0:1, :]
        rank = jnp.sum(jnp.where(ohg, cg - 1.0, 0.0), axis=-1, keepdims=True)
        pos = jnp.sum(jnp.where(ohg, off, 0.0), axis=-1, keepdims=True) + rank
        pos_s[...] = jnp.broadcast_to(pos, (tm, LANES))
        pa = jnp.floor(pos * (1.0 / 64.0))
        pb = pos - 64.0 * pa
        cols = jnp.where(lane == 0, pa, jnp.where(lane == 1, pb, 0.0)).astype(bf16)
        r8 = lax.broadcasted_iota(jnp.int32, (8, LANES), 0)
        l8 = lax.broadcasted_iota(jnp.int32, (8, LANES), 1)
        sel8 = jnp.where((r8 == 0) & (l8 == 0), 64.0, jnp.where((r8 == 0) & (l8 == 1), 1.0, 0.0))
        posrow_s[...] = _nt(sel8.astype(bf16), cols)
        off_i = off.astype(jnp.int32)
        nck_i = (padded * (1.0 / ck)).astype(jnp.int32)
        total = 0
        for g in range(N_EXPERT_GROUPS):
            meta_s[g] = off_i[0, g]
            meta_s[N_EXPERT_GROUPS + g] = nck_i[0, g]
            total = total + nck_i[0, g]
        meta_s[2 * N_EXPERT_GROUPS] = total

        wth, wtl = _split2(wt)
        rows_f = lax.broadcasted_iota(jnp.int32, (ck, tm), 0).astype(f32)
        posrow = posrow_s[0:1, :]

        def fill(c, carry):
            r0 = pl.multiple_of(c * ck, ck)
            pc = jnp.where(rows_f + r0.astype(f32) == posrow, 1.0, 0.0).astype(bf16)
            xs_s[pl.ds(r0, ck), :] = _mm(pc, xh).astype(bf16)
            wts_s[pl.ds(r0, ck), :] = _mm(pc, wth) + _mm(pc, wtl)
            outs_s[pl.ds(r0, ck), :] = jnp.zeros((ck, D_MODEL), f32)
            return carry

        lax.fori_loop(0, total, fill, 0)

    grp = s // STEPS_PER_GROUP
    off_g = meta_s[grp]
    lane_c = lax.broadcasted_iota(jnp.int32, (ck, LANES), 1)

    def chunk(c, carry):
        rows = pl.ds(pl.multiple_of(off_g + c * ck, ck), ck)
        xs = xs_s[rows, :]
        wts = wts_s[rows, :]
        gact = [_mm(xs, wg_ref[j]) for j in range(MOE_STEP_EXPERTS)]
        up = [_mm(xs, wu_ref[j]) for j in range(MOE_STEP_EXPERTS)]
        part = None
        for j in range(MOE_STEP_EXPERTS):
            wcol = jnp.sum(jnp.where(lane_c == s * MOE_STEP_EXPERTS + j, wts, 0.0), axis=-1, keepdims=True)
            hid = gact[j] * jax.nn.sigmoid(gact[j]) * up[j] * wcol
            out = _mm(hid.astype(bf16), wd_ref[j])
            part = out if part is None else part + out
        outs_s[rows, :] += part
        return carry

    lax.fori_loop(0, meta_s[N_EXPERT_GROUPS + grp], chunk, 0)

    @pl.when(s == MOE_STEPS - 1)
    def _():
        o_ref[...] = DEEPNORM_ALPHA * x_ref[...]
        pos_col = pos_s[:, 0:1]
        cols_f = lax.broadcasted_iota(jnp.int32, (tm, ck), 1).astype(f32)

        def unsort(c, carry):
            r0 = pl.multiple_of(c * ck, ck)
            pt = jnp.where(cols_f + r0.astype(f32) == pos_col, 1.0, 0.0).astype(bf16)
            o_ref[...] += _mm(pt, outs_s[pl.ds(r0, ck), :].astype(bf16))
            return carry

        lax.fori_loop(0, meta_s[2 * N_EXPERT_GROUPS], unsort, 0)
        o_ref[...] = _layer_norm(o_ref[...], lnw_ref[...], lnb_ref[...])


def _moe(x2, rw, rb, wg, wu, wd, lnw, lnb, tm=1024):
    m = x2.shape[0]
    tm = min(tm, m)
    rwh, rwl = _split2(rw)
    n_rows = tm + N_EXPERT_GROUPS * MOE_CHUNK
    full = lambda a: pl.BlockSpec(a.shape, lambda i, e: (0,) * a.ndim)
    return pl.pallas_call(
        _moe_kernel,
        grid=(m // tm, MOE_STEPS),
        in_specs=[pl.BlockSpec((tm, D_MODEL), lambda i, e: (i, 0)),
                  full(rwh), full(rwl), full(rb),
                  pl.BlockSpec((MOE_STEP_EXPERTS, D_MODEL, D_EXPERT), lambda i, e: (e, 0, 0)),
                  pl.BlockSpec((MOE_STEP_EXPERTS, D_MODEL, D_EXPERT), lambda i, e: (e, 0, 0)),
                  pl.BlockSpec((MOE_STEP_EXPERTS, D_EXPERT, D_MODEL), lambda i, e: (e, 0, 0)),
                  full(lnw), full(lnb)],
        out_specs=pl.BlockSpec((tm, D_MODEL), lambda i, e: (i, 0)),
        out_shape=jax.ShapeDtypeStruct((m, D_MODEL), f32),
        scratch_shapes=[pltpu.VMEM((n_rows, D_MODEL), bf16), pltpu.VMEM((n_rows, D_MODEL), f32),
                        pltpu.VMEM((n_rows, LANES), f32), pltpu.VMEM((tm, LANES), f32),
                        pltpu.VMEM((8, tm), f32), pltpu.SMEM((2 * N_EXPERT_GROUPS + 1,), jnp.int32)],
        compiler_params=pltpu.CompilerParams(
            dimension_semantics=("parallel", "arbitrary"), vmem_limit_bytes=VMEM_LIMIT),
    )(x2, rwh, rwl, rb, wg, wu, wd, lnw, lnb)


def _router_weights(router_group_w, router_group_b, router_expert_w, router_expert_b):
    pad = LANES - N_EXPERTS - N_EXPERT_GROUPS
    rw = jnp.concatenate([router_expert_w.astype(f32), router_group_w.astype(f32),
                          jnp.zeros((D_MODEL, pad), f32)], axis=1)
    rb = jnp.concatenate([router_expert_b.astype(f32), router_group_b.astype(f32),
                          jnp.zeros((pad,), f32)]).reshape(1, LANES)
    return rw, rb


def kernel(x, w_in, rwkv_mu, rwkv_w0, rwkv_w2, rwkv_a0, rwkv_a2, rwkv_g2, rwkv_k_k, rwkv_k_a, rwkv_r_k, rwkv_ln_w, rwkv_ln_b, ssm_a_re, ssm_a_im, ssm_b_re, ssm_b_im, ssm_c_re, ssm_c_im, ssm_d, ssm_log_dt, ssm_glu_w, ssm_glu_b, w_up_rwkv, w_up_moba, w_up_ssm, gate_b, w_out, ln1_w, ln1_b, router_group_w, router_group_b, router_expert_w, router_expert_b, expert_w_gate, expert_w_up, expert_w_down, ln2_w, ln2_b):
    bsz, seq, _ = x.shape
    x2 = x.reshape(bsz * seq, D_MODEL).astype(f32)
    row = lambda a: a.reshape(1, -1).astype(f32)
    w_all = w_in.astype(bf16)
    for l in range(DEPTH):
        qkv = _project(x2, w_all, l, OFF_MOBA, MOBA_COLS)
        y_a = _rwkv(x2, w_all, l,
                    dict(mu=rwkv_mu[l], w0=rwkv_w0[l], w2=rwkv_w2[l], a0=rwkv_a0[l], a2=rwkv_a2[l],
                         g2=rwkv_g2[l], k_k=rwkv_k_k[l], k_a=rwkv_k_a[l], r_k=rwkv_r_k[l],
                         ln_w=rwkv_ln_w[l], ln_b=rwkv_ln_b[l]), bsz, seq)
        y_b = _moba(qkv.reshape(bsz, seq, MOBA_COLS))
        y_c = _s5(x2, w_all, l,
                  dict(a_re=ssm_a_re[l], a_im=ssm_a_im[l], b_re=ssm_b_re[l], b_im=ssm_b_im[l],
                       c_re=ssm_c_re[l], c_im=ssm_c_im[l], d=ssm_d[l], log_dt=ssm_log_dt[l],
                       glu_w=ssm_glu_w[l], glu_b=ssm_glu_b[l]), bsz, seq)
        x2 = _merge(x2, y_a.reshape(-1, D_RWKV), y_b.reshape(-1, D_MOBA), y_c.reshape(-1, D_SSM),
                    w_all, l, row(gate_b[l]),
                    w_up_rwkv[l].astype(bf16), w_up_moba[l].astype(bf16), w_up_ssm[l].astype(bf16),
                    w_out[l].astype(bf16), row(ln1_w[l]), row(ln1_b[l]))
        rw, rb = _router_weights(router_group_w[l], router_group_b[l], router_expert_w[l],
                                 router_expert_b[l])
        x2 = _moe(x2, rw, rb, expert_w_gate[l].astype(bf16), expert_w_up[l].astype(bf16),
                  expert_w_down[l].astype(bf16), row(ln2_w[l]), row(ln2_b[l]))
    return x2.reshape(bsz, seq, D_MODEL)
```
